```python
import jax
import jax.numpy as jnp
from jax import lax
import numpy as np

D_MODEL = 1024
BATCH = 16
SEQ = 256
DEPTH = 4
DEC_BATCH = 2
DEC_SEQ = 1024
PAST_LEN = 512

GRID_W = 64
HEAD_DIM = 64
A_HEADS = 8
A_KV_HEADS = 2
B_HEADS = 8
C_HEADS = 16
A_WIDTH = A_HEADS * HEAD_DIM
A_KV_WIDTH = A_KV_HEADS * HEAD_DIM
B_WIDTH = B_HEADS * HEAD_DIM
C_WIDTH = C_HEADS * HEAD_DIM
DECAY_LORA = 64
ICLR_LORA = 64
GATE_LORA = 128
B_SHIFT_WIDTH = 3 * B_WIDTH + 2 * DECAY_LORA + 2 * ICLR_LORA + GATE_LORA
EVEN_IN = A_WIDTH + 2 * A_KV_WIDTH + B_SHIFT_WIDTH
EVEN_SPLITS = (A_WIDTH, A_WIDTH + A_KV_WIDTH, A_WIDTH + 2 * A_KV_WIDTH)
B_SPLITS = (B_WIDTH, 2 * B_WIDTH, 3 * B_WIDTH, 3 * B_WIDTH + DECAY_LORA, 3 * B_WIDTH + 2 * DECAY_LORA,
            3 * B_WIDTH + 2 * DECAY_LORA + ICLR_LORA, 3 * B_WIDTH + 2 * DECAY_LORA + 2 * ICLR_LORA)
NA_ROWS = 8
NA_COLS = 16
D_FF = 2816
N_EXPERTS = 8
TOP_K = 2
D_EXPERT = 1408
N_EVEN = (DEPTH + 1) // 2
N_ODD = DEPTH // 2
Q_BLOCK = 128
ROPE_THETA = 10000.0
EPS = 1e-6
GN_EPS = 64e-5
NEG_INF = -1e30

kernel_name = 'hybrid_diffusion_prefix_trunk_step'


def rmsnorm(x, g):
    x32 = x.astype(jnp.float32)
    y = x32 * lax.rsqrt(jnp.mean(x32 * x32, axis=-1, keepdims=True) + EPS)
    return (y * g.astype(jnp.float32)).astype(x.dtype)


def adaln(cond, w, b):
    return jnp.split(jax.nn.silu(cond) @ w + b, 6, axis=-1)


def modulate(x, g, shift, scale):
    return rmsnorm(x, g) * (1 + scale[:, None]) + shift[:, None]


def token_shift_mix(z, mu):
    zp = jnp.pad(z, ((0, 0), (1, 1), (0, 0)))
    return z + mu * (0.5 * (zp[:, :-2] + zp[:, 2:]) - z)


def axial_rope(x):
    T, dh = x.shape[1], x.shape[-1]
    t = np.arange(T)
    n_f = dh // 4
    inv = ROPE_THETA ** (-np.arange(n_f, dtype=np.float32) / n_f)

    def rot(xa, pos):
        ang = pos[:, None].astype(np.float32) * inv[None, :]
        cos = jnp.asarray(np.cos(ang), x.dtype)[None, :, None, :]
        sin = jnp.asarray(np.sin(ang), x.dtype)[None, :, None, :]
        x1, x2 = xa[..., :n_f], xa[..., n_f:]
        return jnp.concatenate([x1 * cos - x2 * sin, x1 * sin + x2 * cos], axis=-1)

    half = dh // 2
    return jnp.concatenate([rot(x[..., :half], t // GRID_W), rot(x[..., half:], t % GRID_W)], axis=-1)


def gqa_attention(q, k, v):
    B, T, Hq, dh = q.shape
    Hkv = k.shape[2]
    G = Hq // Hkv
    nb = T // Q_BLOCK
    qb = q.reshape(B, nb, Q_BLOCK, Hkv, G, dh).transpose(1, 0, 2, 3, 4, 5)
    scale = dh ** -0.5

    def block(qi):
        s = jnp.einsum('bqkgd,bskd->bkgqs', qi, k).astype(jnp.float32) * scale
        p = jax.nn.softmax(s, axis=-1).astype(v.dtype)
        return jnp.einsum('bkgqs,bskd->bqkgd', p, v)

    o = lax.map(block, qb)
    return o.transpose(1, 0, 2, 3, 4, 5).reshape(B, T, Hq, dh)


def neighbourhood_attention(q, k, v, ck, cv, rpb):
    B, T, H, dh = q.shape
    rows = T // GRID_W
    kh = min(NA_ROWS, rows)
    kw = NA_COLS
    r = np.arange(rows)
    row_idx = np.clip(r - kh // 2, 0, rows - kh)[:, None] + np.arange(kh)[None, :]
    col = np.arange(GRID_W)
    cs = np.clip(col - kw // 2, 0, GRID_W - kw)
    col_in = (col[None, :] >= cs[:, None]) & (col[None, :] < cs[:, None] + kw)
    dr = row_idx - r[:, None] + (NA_ROWS - 1)
    dc = np.clip(col[None, :] - col[:, None], -(NA_COLS - 1), NA_COLS - 1) + (NA_COLS - 1)
    scale = dh ** -0.5
    qg = q.reshape(B, rows, GRID_W, H, dh)
    kg = k.reshape(B, rows, GRID_W, H, dh)[:, row_idx]
    vg = v.reshape(B, rows, GRID_W, H, dh)[:, row_idx]
    bias = rpb[:, dr[:, None, :, None], dc[None, :, None, :]].astype(jnp.float32)
    s_nb = jnp.einsum('brqhd,brkwhd->bhrqkw', qg, kg).astype(jnp.float32) * scale + bias[None]
    s_nb = jnp.where(jnp.asarray(col_in)[:, None, :], s_nb, NEG_INF)
    s_nb = s_nb.reshape(B, H, rows, GRID_W, kh * GRID_W)
    s_ctx = jnp.einsum('brqhd,bshd->bhrqs', qg, ck).astype(jnp.float32) * scale
    p = jax.nn.softmax(jnp.concatenate([s_nb, s_ctx], axis=-1), axis=-1).astype(v.dtype)
    p_nb = p[..., :kh * GRID_W].reshape(B, H, rows, GRID_W, kh, GRID_W)
    p_ctx = p[..., kh * GRID_W:]
    o = jnp.einsum('bhrqkw,brkwhd->brqhd', p_nb, vg) + jnp.einsum('bhrqs,bshd->brqhd', p_ctx, cv)
    return o.reshape(B, T, H, dh)


def _rwkv_step(S, xs):
    r, w, k, v, kk, a = xs
    sa = jnp.einsum('bhij,bhj->bhi', S, -kk)
    S = S * w[:, :, None, :] + sa[..., None] * (kk * a)[:, :, None, :] + v[..., None] * k[:, :, None, :]
    return S, jnp.einsum('bhij,bhj->bhi', S, r)


def rwkv7_bidir(zb, s0f, s0b, W, e):
    B, T, _ = zb.shape
    f32 = jnp.float32
    r, k, v, wdf, wdb, adf, adb, gd = jnp.split(zb, B_SPLITS, axis=-1)
    kk = (k * W['b_kk'][e]).reshape(B, T, B_HEADS, HEAD_DIM).astype(f32)
    kk = kk * lax.rsqrt(jnp.sum(kk * kk, axis=-1, keepdims=True) + 1e-12)
    kk_t = kk.transpose(1, 0, 2, 3)

    def heads_t(t):
        return t.reshape(B, T, B_HEADS, HEAD_DIM).astype(f32).transpose(1, 0, 2, 3)

    r_t, v_t = heads_t(r), heads_t(v)

    def direction(wd, ad, d, s0, reverse):
        wl = (W['b_w0'][e, d] + jnp.tanh(wd) @ W['b_w2'][e, d]).astype(f32)
        decay = jnp.exp(-jnp.exp(-jax.nn.softplus(-wl) - 0.5))
        a = jax.nn.sigmoid(W['b_a0'][e, d] + ad @ W['b_a2'][e, d])
        kd = k * (1 + (a - 1) * W['b_ka'][e])
        xs = (r_t, heads_t(decay), heads_t(kd), v_t, kk_t, heads_t(a))
        s, ys = lax.scan(_rwkv_step, s0.astype(f32), xs, reverse=reverse)
        return ys, s

    yf, sf = direction(wdf, adf, 0, s0f, False)
    yb, sb = direction(wdb, adb, 1, s0b, True)
    y = (yf + yb).transpose(1, 0, 2, 3)
    mu = jnp.mean(y, axis=-1, keepdims=True)
    var = jnp.mean(jnp.square(y - mu), axis=-1, keepdims=True)
    y = ((y - mu) * lax.rsqrt(var + GN_EPS)).reshape(B, T, B_WIDTH)
    y = (y * W['b_ln_g'][e] + W['b_ln_b'][e]).astype(zb.dtype)
    rh = r.reshape(B, T, B_HEADS, HEAD_DIM)
    kh_ = k.reshape(B, T, B_HEADS, HEAD_DIM)
    vh = v.reshape(B, T, B_HEADS, HEAD_DIM)
    bonus = (jnp.sum(rh * kh_ * W['b_rk'][e], axis=-1, keepdims=True) * vh).reshape(B, T, B_WIDTH)
    g = jax.nn.sigmoid(gd) @ W['b_g2'][e]
    return (y + bonus) * g, sf.astype(zb.dtype), sb.astype(zb.dtype)


def even_mixer(h, e, W, ctx):
    B, T, _ = h.shape
    z = h @ W['w_in_e'][e]
    qa, ka, va, zb = jnp.split(z, EVEN_SPLITS, axis=-1)
    q = rmsnorm(qa.reshape(B, T, A_HEADS, HEAD_DIM), W['a_q_gain'][e])
    k = rmsnorm(ka.reshape(B, T, A_KV_HEADS, HEAD_DIM), W['a_k_gain'][e])
    v = va.reshape(B, T, A_KV_HEADS, HEAD_DIM)
    if ctx is None:
        ya = gqa_attention(q, k, v)
        s0f = jnp.zeros((B, B_HEADS, HEAD_DIM, HEAD_DIM), jnp.float32)
        s0b = s0f
    else:
        ck, cv, s0f, s0b = ctx
        ya = gqa_attention(axial_rope(q), jnp.concatenate([axial_rope(k), ck], axis=1),
                           jnp.concatenate([v, cv], axis=1))
    yb, sf, sb = rwkv7_bidir(token_shift_mix(zb, W['b_mu'][e]), s0f, s0b, W, e)
    out = jnp.concatenate([ya.reshape(B, T, A_WIDTH), yb], axis=-1) @ W['w_out_e'][e]
    return out, (k, v, sf, sb)


def odd_mixer(h, o, W, ctx):
    B, T, _ = h.shape
    q, k, v = [t.reshape(B, T, C_HEADS, HEAD_DIM) for t in jnp.split(h @ W['w_in_o'][o], 3, axis=-1)]
    if ctx is None:
        y = gqa_attention(q, k, v)
    else:
        ck, cv = ctx
        y = neighbourhood_attention(q, k, v, ck, cv, W['c_rpb'][o])
    return y.reshape(B, T, C_WIDTH) @ W['w_out_o'][o], (k, v)


def swiglu(x, w_gu, w_dn):
    g, u = jnp.split(x @ w_gu, 2, axis=-1)
    return (jax.nn.silu(g) * u) @ w_dn


def moe_swiglu(h, router_w, router_b, w_gu, w_dn):
    B, T, D = h.shape
    xf = h.reshape(B * T, D)
    logits = (xf @ router_w).astype(jnp.float32)
    _, idx = lax.top_k(logits + router_b.astype(jnp.float32), TOP_K)
    wts = jax.nn.softmax(jnp.take_along_axis(logits, idx, axis=-1), axis=-1)
    gates = jnp.sum(jax.nn.one_hot(idx, N_EXPERTS, dtype=jnp.float32) * wts[..., None], axis=1).astype(h.dtype)
    y = jnp.zeros_like(xf)
    for ex in range(N_EXPERTS):
        y = y + gates[:, ex:ex + 1] * swiglu(xf, w_gu[ex], w_dn[ex])
    return y.reshape(B, T, D)


def trunk_layer(l, x, cond, W, ctx):
    sh1, sc1, g1, sh2, sc2, g2 = adaln(cond, W['ada_w'][l], W['ada_b'][l])
    h = modulate(x, W['norm1_g'][l], sh1, sc1)
    if l % 2 == 0:
        out, new = even_mixer(h, l // 2, W, ctx)
    else:
        out, new = odd_mixer(h, l // 2, W, ctx)
    x = x + g1[:, None] * out
    h = modulate(x, W['norm2_g'][l], sh2, sc2)
    if l % 2 == 0:
        f = swiglu(h, W['ffn_w_gu'][l // 2], W['ffn_w_dn'][l // 2])
    else:
        o = l // 2
        f = moe_swiglu(h, W['router_w'][o], W['router_b'][o], W['moe_w_gu'][o], W['moe_w_dn'][o])
    x = x + g2[:, None] * f
    return x, new


def setup_inputs(seed: int = 0) -> dict:
    key = jax.random.key(seed)
    ks = iter(jax.random.split(key, 48))

    def nrm(shape, scale):
        return jax.random.normal(next(ks), shape, jnp.float32) * scale

    D = D_MODEL
    return {
        'x_prompt': nrm((BATCH, SEQ, D), 1.0),
        'x_sample': nrm((DEC_BATCH, DEC_SEQ, D), 1.0),
        'cache_a_k': nrm((DEC_BATCH, N_EVEN, PAST_LEN, A_KV_HEADS, HEAD_DIM), 1.0),
        'cache_a_v': nrm((DEC_BATCH, N_EVEN, PAST_LEN, A_KV_HEADS, HEAD_DIM), 1.0),
        'state_b_fwd': nrm((DEC_BATCH, N_EVEN, B_HEADS, HEAD_DIM, HEAD_DIM), 0.3),
        'state_b_bwd': nrm((DEC_BATCH, N_EVEN, B_HEADS, HEAD_DIM, HEAD_DIM), 0.3),
        'cache_c_k': nrm((DEC_BATCH, N_ODD, PAST_LEN, C_HEADS, HEAD_DIM), 1.0),
        'cache_c_v': nrm((DEC_BATCH, N_ODD, PAST_LEN, C_HEADS, HEAD_DIM), 1.0),
        'c': nrm((DEC_BATCH, D), 1.0),
        'c_ctx': nrm((D,), 1.0),
        'ada_w': nrm((DEPTH, D, 6 * D), D ** -0.5),
        'ada_b': nrm((DEPTH, 6 * D), 0.02),
        'norm1_g': 1.0 + nrm((DEPTH, D), 0.02),
        'norm2_g': 1.0 + nrm((DEPTH, D), 0.02),
        'final_norm_g': 1.0 + nrm((D,), 0.02),
        'w_in_e': nrm((N_EVEN, D, EVEN_IN), D ** -0.5),
        'w_out_e': nrm((N_EVEN, A_WIDTH + B_WIDTH, D), (A_WIDTH + B_WIDTH) ** -0.5),
        'a_q_gain': 1.0 + nrm((N_EVEN, HEAD_DIM), 0.02),
        'a_k_gain': 1.0 + nrm((N_EVEN, HEAD_DIM), 0.02),
        'b_mu': jax.random.uniform(next(ks), (N_EVEN, B_SHIFT_WIDTH), jnp.float32),
        'b_w0': nrm((N_EVEN, 2, B_WIDTH), 0.5) - 0.5,
        'b_w2': nrm((N_EVEN, 2, DECAY_LORA, B_WIDTH), 0.5 * DECAY_LORA ** -0.5),
        'b_a0': nrm((N_EVEN, 2, B_WIDTH), 0.1),
        'b_a2': nrm((N_EVEN, 2, ICLR_LORA, B_WIDTH), 0.5 * ICLR_LORA ** -0.5),
        'b_g2': nrm((N_EVEN, GATE_LORA, B_WIDTH), GATE_LORA ** -0.5),
        'b_kk': 0.85 + nrm((N_EVEN, B_WIDTH), 0.05),
        'b_ka': 1.0 + nrm((N_EVEN, B_WIDTH), 0.05),
        'b_rk': nrm((N_EVEN, B_HEADS, HEAD_DIM), 0.05),
        'b_ln_g': 1.0 + nrm((N_EVEN, B_WIDTH), 0.02),
        'b_ln_b': nrm((N_EVEN, B_WIDTH), 0.02),
        'ffn_w_gu': nrm((N_EVEN, D, 2 * D_FF), D ** -0.5),
        'ffn_w_dn': nrm((N_EVEN, D_FF, D), D_FF ** -0.5),
        'w_in_o': nrm((N_ODD, D, 3 * C_WIDTH), D ** -0.5),
        'w_out_o': nrm((N_ODD, C_WIDTH, D), C_WIDTH ** -0.5),
        'c_rpb': nrm((N_ODD, C_HEADS, 2 * NA_ROWS - 1, 2 * NA_COLS - 1), 0.1),
        'router_w': nrm((N_ODD, D, N_EXPERTS), D ** -0.5),
        'router_b': nrm((N_ODD, N_EXPERTS), 0.01),
        'moe_w_gu': nrm((N_ODD, N_EXPERTS, D, 2 * D_EXPERT), D ** -0.5),
        'moe_w_dn': nrm((N_ODD, N_EXPERTS, D_EXPERT, D), D_EXPERT ** -0.5),
    }


def reference(x_prompt, x_sample, cache_a_k, cache_a_v, state_b_fwd, state_b_bwd, cache_c_k, cache_c_v,
              c, c_ctx, ada_w, ada_b, norm1_g, norm2_g, final_norm_g, w_in_e, w_out_e, a_q_gain, a_k_gain,
              b_mu, b_w0, b_w2, b_a0, b_a2, b_g2, b_kk, b_ka, b_rk, b_ln_g, b_ln_b, ffn_w_gu, ffn_w_dn,
              w_in_o, w_out_o, c_rpb, router_w, router_b, moe_w_gu, moe_w_dn):
    W = {'ada_w': ada_w, 'ada_b': ada_b, 'norm1_g': norm1_g, 'norm2_g': norm2_g,
         'w_in_e': w_in_e, 'w_out_e': w_out_e, 'a_q_gain': a_q_gain, 'a_k_gain': a_k_gain,
         'b_mu': b_mu, 'b_w0': b_w0, 'b_w2': b_w2, 'b_a0': b_a0, 'b_a2': b_a2, 'b_g2': b_g2,
         'b_kk': b_kk, 'b_ka': b_ka, 'b_rk': b_rk, 'b_ln_g': b_ln_g, 'b_ln_b': b_ln_b,
         'ffn_w_gu': ffn_w_gu, 'ffn_w_dn': ffn_w_dn, 'w_in_o': w_in_o, 'w_out_o': w_out_o,
         'c_rpb': c_rpb, 'router_w': router_w, 'router_b': router_b,
         'moe_w_gu': moe_w_gu, 'moe_w_dn': moe_w_dn}

    x_ctx = x_prompt
    a_k, a_v, s_f, s_b, c_k, c_v = [], [], [], [], [], []
    for l in range(DEPTH):
        x_ctx, new = trunk_layer(l, x_ctx, c_ctx[None, :], W, None)
        if l % 2 == 0:
            a_k.append(new[0]); a_v.append(new[1]); s_f.append(new[2]); s_b.append(new[3])
        else:
            c_k.append(new[0]); c_v.append(new[1])
    y_prompt = rmsnorm(x_ctx, final_norm_g)
    new_cache_a_k = jnp.stack(a_k, axis=1)
    new_cache_a_v = jnp.stack(a_v, axis=1)
    new_state_b_fwd = jnp.stack(s_f, axis=1)
    new_state_b_bwd = jnp.stack(s_b, axis=1)
    new_cache_c_k = jnp.stack(c_k, axis=1)
    new_cache_c_v = jnp.stack(c_v, axis=1)

    x_lat = x_sample
    for l in range(DEPTH):
        if l % 2 == 0:
            e = l // 2
            ctx = (cache_a_k[:, e], cache_a_v[:, e], state_b_fwd[:, e], state_b_bwd[:, e])
        else:
            o = l // 2
            ctx = (cache_c_k[:, o], cache_c_v[:, o])
        x_lat, _ = trunk_layer(l, x_lat, c, W, ctx)
    y_sample = rmsnorm(x_lat, final_norm_g)

    return (y_prompt, y_sample, new_cache_a_k, new_cache_a_v, new_state_b_fwd, new_state_b_bwd,
            new_cache_c_k, new_cache_c_v)
```

```python
import functools

import numpy as np
import jax
import jax.numpy as jnp
from jax import lax
from jax.experimental import pallas as pl
from jax.experimental.pallas import tpu as pltpu

F32 = jnp.float32
BF16 = jnp.bfloat16
HIGHEST = lax.Precision.HIGHEST

D = 1024
N_CTX, T_CTX = 16, 256
N_LAT, T_LAT = 2, 1024
ROWS_CTX = N_CTX * T_CTX
ROWS_LAT = N_LAT * T_LAT
ROWS = ROWS_CTX + ROWS_LAT
DEPTH = 4
GRID_W = 64
HD = 64
A_HEADS, A_KV_HEADS, B_HEADS, C_HEADS = 8, 2, 8, 16
A_W, A_KV_W, B_W, C_W = A_HEADS * HD, A_KV_HEADS * HD, B_HEADS * HD, C_HEADS * HD
A_IN = A_W + 2 * A_KV_W
LORA_W, LORA_A, LORA_G = 64, 64, 128
B_IN = 3 * B_W + 2 * LORA_W + 2 * LORA_A + LORA_G
PAST = 512
NA_ROWS, NA_COLS = 8, 16
D_FF = 2816
N_EXPERTS = 8
D_EXPERT = 1408
ROPE_THETA = 10000.0
EPS = 1e-6
GN_EPS = 64e-5
NEG_INF = -1e30
ATT_SCALE = HD ** -0.5

LANES = 128
SUBLANES = 8
VMEM_LIMIT = 56 * 1024 * 1024

N_MODS = 8


def _cparams(sem):
    return pltpu.CompilerParams(dimension_semantics=sem, vmem_limit_bytes=VMEM_LIMIT)


def _mod_index(row_start):
    return jnp.where(row_start < ROWS_CTX, 0, 1 + (row_start - ROWS_CTX) // T_LAT)


def _modnorm(x, g, shift, scale):
    ms = jnp.mean(x * x, axis=-1, keepdims=True)
    return (x * lax.rsqrt(ms + EPS) * g) * (1.0 + scale) + shift


def _silu(x):
    return x * jax.nn.sigmoid(x)


def _ada_kernel(c_ref, w_ref, b_ref, o_ref):
    s = _silu(c_ref[...]).astype(BF16)
    o_ref[0] = jnp.dot(s, w_ref[0].astype(BF16), preferred_element_type=F32) + b_ref[0]


def ada_all(cond, ada_w, ada_b):
    tn = 1536
    n = 6 * D
    return pl.pallas_call(
        _ada_kernel,
        grid=(DEPTH, n // tn),
        in_specs=[pl.BlockSpec((N_MODS, D), lambda l, j: (0, 0)),
                  pl.BlockSpec((1, D, tn), lambda l, j: (l, 0, j)),
                  pl.BlockSpec((1, 1, tn), lambda l, j: (l, 0, j))],
        out_specs=pl.BlockSpec((1, N_MODS, tn), lambda l, j: (l, 0, j)),
        out_shape=jax.ShapeDtypeStruct((DEPTH, N_MODS, n), F32),
        compiler_params=_cparams(("arbitrary", "arbitrary")),
        name="ada",
    )(cond, ada_w, ada_b.reshape(DEPTH, 1, n))


def _modnorm_kernel(x_ref, g_ref, mod_ref, h_ref, *, shift_idx, scale_idx):
    shift = mod_ref[0, shift_idx:shift_idx + 1, :]
    scale = mod_ref[0, scale_idx:scale_idx + 1, :]
    h_ref[...] = _modnorm(x_ref[...], g_ref[...], shift, scale).astype(BF16)


def modnorm(x, g, mods, shift_idx, scale_idx):
    tm = 512
    return pl.pallas_call(
        functools.partial(_modnorm_kernel, shift_idx=shift_idx, scale_idx=scale_idx),
        grid=(ROWS // tm,),
        in_specs=[pl.BlockSpec((tm, D), lambda i: (i, 0)),
                  pl.BlockSpec((1, D), lambda i: (0, 0)),
                  pl.BlockSpec((1, 6, D), lambda i: (_mod_index(i * tm), 0, 0))],
        out_specs=pl.BlockSpec((tm, D), lambda i: (i, 0)),
        out_shape=jax.ShapeDtypeStruct((ROWS, D), BF16),
        compiler_params=_cparams(("arbitrary",)),
        name="modnorm",
    )(x, g, mods)


def _mm_in_kernel(h_ref, w_ref, o_ref, wb_ref):
    @pl.when(pl.program_id(1) == 0)
    def _():
        wb_ref[...] = w_ref[...].astype(BF16)

    o_ref[...] = jnp.dot(h_ref[...], wb_ref[...], preferred_element_type=F32)


def mm_in(h, w, tn, tm=512):
    k, n = w.shape
    return pl.pallas_call(
        _mm_in_kernel,
        grid=(n // tn, ROWS // tm),
        in_specs=[pl.BlockSpec((tm, k), lambda j, i: (i, 0)),
                  pl.BlockSpec((k, tn), lambda j, i: (0, j))],
        out_specs=pl.BlockSpec((tm, tn), lambda j, i: (i, j)),
        out_shape=jax.ShapeDtypeStruct((ROWS, n), F32),
        scratch_shapes=[pltpu.VMEM((k, tn), BF16)],
        compiler_params=_cparams(("arbitrary", "arbitrary")),
        name="mm_in",
    )(h, w)


def _residual_epilogue(x, acc, mod_ref, nmod_ref, g_ref, xo_ref, ho_ref, gate_idx, shift_idx, scale_idx, final):
    gate = mod_ref[0, gate_idx:gate_idx + 1, :]
    xn = x + gate * acc
    if final:
        ms = jnp.mean(xn * xn, axis=-1, keepdims=True)
        ho_ref[...] = xn * lax.rsqrt(ms + EPS) * g_ref[...]
    else:
        xo_ref[...] = xn
        shift = nmod_ref[0, shift_idx:shift_idx + 1, :]
        scale = nmod_ref[0, scale_idx:scale_idx + 1, :]
        ho_ref[...] = _modnorm(xn, g_ref[...], shift, scale).astype(BF16)


def _mm_out_kernel(y_ref, w_ref, x_ref, mod_ref, nmod_ref, g_ref, xo_ref, ho_ref, wb_ref, *,
                   gate_idx, shift_idx, scale_idx):
    @pl.when(pl.program_id(0) == 0)
    def _():
        wb_ref[...] = w_ref[...].astype(BF16)

    acc = jnp.dot(y_ref[...], wb_ref[...], preferred_element_type=F32)
    _residual_epilogue(x_ref[...], acc, mod_ref, nmod_ref, g_ref, xo_ref, ho_ref,
                       gate_idx, shift_idx, scale_idx, False)


def mm_out(y, w, x, mods, nmods, g, gate_idx, shift_idx, scale_idx):
    tm = 512
    k = w.shape[0]
    modspec = pl.BlockSpec((1, 6, D), lambda i: (_mod_index(i * tm), 0, 0))
    return pl.pallas_call(
        functools.partial(_mm_out_kernel, gate_idx=gate_idx, shift_idx=shift_idx, scale_idx=scale_idx),
        grid=(ROWS // tm,),
        in_specs=[pl.BlockSpec((tm, k), lambda i: (i, 0)),
                  pl.BlockSpec((k, D), lambda i: (0, 0)),
                  pl.BlockSpec((tm, D), lambda i: (i, 0)),
                  modspec, modspec,
                  pl.BlockSpec((1, D), lambda i: (0, 0))],
        out_specs=[pl.BlockSpec((tm, D), lambda i: (i, 0)),
                   pl.BlockSpec((tm, D), lambda i: (i, 0))],
        out_shape=[jax.ShapeDtypeStruct((ROWS, D), F32),
                   jax.ShapeDtypeStruct((ROWS, D), BF16)],
        scratch_shapes=[pltpu.VMEM((k, D), BF16)],
        compiler_params=_cparams(("arbitrary",)),
        name="mm_out",
    )(y, w, x, mods, nmods, g)


FFN_TF = 256
FFN_NF = D_FF // FFN_TF


def _ffn_kernel(h_ref, wg_ref, wu_ref, wd_ref, x_ref, mod_ref, nmod_ref, g_ref, xo_ref, ho_ref,
                wg_s, wu_s, wd_s, acc_ref, *, gate_idx, shift_idx, scale_idx):
    i = pl.program_id(0)
    f = pl.program_id(1)

    @pl.when(i == 0)
    def _():
        wg_s[f] = wg_ref[...].astype(BF16)
        wu_s[f] = wu_ref[...].astype(BF16)
        wd_s[f] = wd_ref[...].astype(BF16)

    @pl.when(f == 0)
    def _():
        acc_ref[...] = jnp.zeros_like(acc_ref)

    h = h_ref[...]
    gp = jnp.dot(h, wg_s[f], preferred_element_type=F32)
    up = jnp.dot(h, wu_s[f], preferred_element_type=F32)
    a = (_silu(gp) * up).astype(BF16)
    acc_ref[...] += jnp.dot(a, wd_s[f], preferred_element_type=F32)

    @pl.when(f == FFN_NF - 1)
    def _():
        _residual_epilogue(x_ref[...], acc_ref[...], mod_ref, nmod_ref, g_ref, xo_ref, ho_ref,
                           gate_idx, shift_idx, scale_idx, False)


def ffn(h, w_gu, w_dn, x, mods, nmods, g, gate_idx, shift_idx, scale_idx):
    tm = 512
    tf, nf = FFN_TF, FFN_NF

    def once(i, f):
        return jnp.where(i == 0, f, nf - 1)

    modspec = pl.BlockSpec((1, 6, D), lambda i, f: (_mod_index(i * tm), 0, 0))
    return pl.pallas_call(
        functools.partial(_ffn_kernel, gate_idx=gate_idx, shift_idx=shift_idx, scale_idx=scale_idx),
        grid=(ROWS // tm, nf),
        in_specs=[pl.BlockSpec((tm, D), lambda i, f: (i, 0)),
                  pl.BlockSpec((D, tf), lambda i, f: (0, once(i, f))),
                  pl.BlockSpec((D, tf), lambda i, f: (0, nf + once(i, f))),
                  pl.BlockSpec((tf, D), lambda i, f: (once(i, f), 0)),
                  pl.BlockSpec((tm, D), lambda i, f: (i, 0)),
                  modspec, modspec,
                  pl.BlockSpec((1, D), lambda i, f: (0, 0))],
        out_specs=[pl.BlockSpec((tm, D), lambda i, f: (i, 0)),
                   pl.BlockSpec((tm, D), lambda i, f: (i, 0))],
        out_shape=[jax.ShapeDtypeStruct((ROWS, D), F32),
                   jax.ShapeDtypeStruct((ROWS, D), BF16)],
        scratch_shapes=[pltpu.VMEM((nf, D, tf), BF16),
                        pltpu.VMEM((nf, D, tf), BF16),
                        pltpu.VMEM((nf, tf, D), BF16),
                        pltpu.VMEM((tm, D), F32)],
        compiler_params=_cparams(("arbitrary", "arbitrary")),
        name="ffn",
    )(h, w_gu, w_gu, w_dn, x, mods, nmods, g)


def _router_kernel(x_ref, g_ref, mod_ref, rw_ref, rb_ref, gates_ref, *, shift_idx, scale_idx):
    shift = mod_ref[0, shift_idx:shift_idx + 1, :]
    scale = mod_ref[0, scale_idx:scale_idx + 1, :]
    h = _modnorm(x_ref[...], g_ref[...], shift, scale)
    logits = jnp.dot(h, rw_ref[...], precision=HIGHEST, preferred_element_type=F32)
    lane = lax.broadcasted_iota(jnp.int32, logits.shape, 1)
    sel = jnp.where(lane < N_EXPERTS, logits + rb_ref[...], -jnp.inf)
    m1 = jnp.max(sel, axis=-1, keepdims=True)
    i1 = jnp.min(jnp.where(sel == m1, lane, LANES), axis=-1, keepdims=True)
    sel2 = jnp.where(lane == i1, -jnp.inf, sel)
    m2 = jnp.max(sel2, axis=-1, keepdims=True)
    i2 = jnp.min(jnp.where(sel2 == m2, lane, LANES), axis=-1, keepdims=True)
    l1 = jnp.sum(jnp.where(lane == i1, logits, 0.0), axis=-1, keepdims=True)
    l2 = jnp.sum(jnp.where(lane == i2, logits, 0.0), axis=-1, keepdims=True)
    mx = jnp.maximum(l1, l2)
    e1 = jnp.exp(l1 - mx)
    e2 = jnp.exp(l2 - mx)
    den = e1 + e2
    gates_ref[...] = jnp.where(lane == i1, e1 / den, 0.0) + jnp.where(lane == i2, e2 / den, 0.0)


def router(x, g, mods, rw, rb, shift_idx, scale_idx):
    tm = 512
    rw_p = jnp.zeros((D, LANES), F32).at[:, :N_EXPERTS].set(rw)
    rb_p = jnp.zeros((1, LANES), F32).at[0, :N_EXPERTS].set(rb)
    return pl.pallas_call(
        functools.partial(_router_kernel, shift_idx=shift_idx, scale_idx=scale_idx),
        grid=(ROWS // tm,),
        in_specs=[pl.BlockSpec((tm, D), lambda i: (i, 0)),
                  pl.BlockSpec((1, D), lambda i: (0, 0)),
                  pl.BlockSpec((1, 6, D), lambda i: (_mod_index(i * tm), 0, 0)),
                  pl.BlockSpec((D, LANES), lambda i: (0, 0)),
                  pl.BlockSpec((1, LANES), lambda i: (0, 0))],
        out_specs=pl.BlockSpec((tm, LANES), lambda i: (i, 0)),
        out_shape=jax.ShapeDtypeStruct((ROWS, LANES), F32),
        compiler_params=_cparams(("arbitrary",)),
        name="router",
    )(x, g, mods, rw_p, rb_p)


def _moe_kernel(h_ref, gates_ref, wgu_ref, wdn_ref, yin_ref, yo_ref, wgu_s, wdn_s):
    e = pl.program_id(0)

    @pl.when(pl.program_id(1) == 0)
    def _():
        wgu_s[...] = wgu_ref[0].astype(BF16)
        wdn_s[...] = wdn_ref[0].astype(BF16)

    gu = jnp.dot(h_ref[...], wgu_s[...], preferred_element_type=F32)
    a = (_silu(gu[:, :D_EXPERT]) * gu[:, D_EXPERT:]).astype(BF16)
    y = jnp.dot(a, wdn_s[...], preferred_element_type=F32)
    gt = gates_ref[...]
    lane = lax.broadcasted_iota(jnp.int32, gt.shape, 1)
    gate = jnp.sum(jnp.where(lane == e, gt, 0.0), axis=-1, keepdims=True)
    yo_ref[...] = yin_ref[...] + gate * y


def moe_dense(h, gates, w_gu, w_dn):
    tm = 256
    y0 = jnp.zeros((ROWS, D), F32)
    return pl.pallas_call(
        _moe_kernel,
        grid=(N_EXPERTS, ROWS // tm),
        in_specs=[pl.BlockSpec((tm, D), lambda e, i: (i, 0)),
                  pl.BlockSpec((tm, LANES), lambda e, i: (i, 0)),
                  pl.BlockSpec((1, D, 2 * D_EXPERT), lambda e, i: (e, 0, 0), pipeline_mode=pl.Buffered(1)),
                  pl.BlockSpec((1, D_EXPERT, D), lambda e, i: (e, 0, 0), pipeline_mode=pl.Buffered(1)),
                  pl.BlockSpec((tm, D), lambda e, i: (i, 0))],
        out_specs=pl.BlockSpec((tm, D), lambda e, i: (i, 0)),
        out_shape=jax.ShapeDtypeStruct((ROWS, D), F32),
        scratch_shapes=[pltpu.VMEM((D, 2 * D_EXPERT), BF16),
                        pltpu.VMEM((D_EXPERT, D), BF16)],
        input_output_aliases={4: 0},
        compiler_params=_cparams(("arbitrary", "arbitrary")),
        name="moe",
    )(h, gates, w_gu, w_dn, y0)


def _resid_kernel(x_ref, y_ref, mod_ref, nmod_ref, g_ref, *out_refs, gate_idx, shift_idx, scale_idx, final):
    if final:
        xo_ref, ho_ref = None, out_refs[0]
    else:
        xo_ref, ho_ref = out_refs
    _residual_epilogue(x_ref[...], y_ref[...], mod_ref, nmod_ref, g_ref, xo_ref, ho_ref,
                       gate_idx, shift_idx, scale_idx, final)


def resid(x, y, mods, nmods, g, gate_idx, shift_idx, scale_idx, final):
    tm = 512
    modspec = pl.BlockSpec((1, 6, D), lambda i: (_mod_index(i * tm), 0, 0))
    rowspec = pl.BlockSpec((tm, D), lambda i: (i, 0))
    if final:
        out_specs = [rowspec]
        out_shape = [jax.ShapeDtypeStruct((ROWS, D), F32)]
    else:
        out_specs = [rowspec, rowspec]
        out_shape = [jax.ShapeDtypeStruct((ROWS, D), F32), jax.ShapeDtypeStruct((ROWS, D), BF16)]
    return pl.pallas_call(
        functools.partial(_resid_kernel, gate_idx=gate_idx, shift_idx=shift_idx, scale_idx=scale_idx,
                          final=final),
        grid=(ROWS // tm,),
        in_specs=[rowspec, rowspec, modspec, modspec, pl.BlockSpec((1, D), lambda i: (0, 0))],
        out_specs=out_specs,
        out_shape=out_shape,
        compiler_params=_cparams(("arbitrary",)),
        name="resid",
    )(x, y, mods, nmods, g)


def _softmax_pv(scores, values):
    m = None
    for s in scores:
        mi = jnp.max(s, axis=-1, keepdims=True)
        m = mi if m is None else jnp.maximum(m, mi)
    num, den = None, None
    for s, v in zip(scores, values):
        p = jnp.exp(s - m)
        li = jnp.sum(p, axis=-1, keepdims=True)
        oi = jnp.dot(p.astype(BF16), v, preferred_element_type=F32)
        num = oi if num is None else num + oi
        den = li if den is None else den + li
    return num / den


def _qk(q, k):
    return lax.dot_general(q, k, (((1,), (1,)), ((), ())), preferred_element_type=F32)


def _head_rmsnorm(x, gain, n_heads):
    parts = []
    for h in range(n_heads):
        xh = x[:, h * HD:(h + 1) * HD]
        ms = jnp.mean(xh * xh, axis=-1, keepdims=True)
        parts.append(xh * lax.rsqrt(ms + EPS) * gain)
    return parts


def _rope128(x, cos, sin_signed):
    lane = lax.broadcasted_iota(jnp.int32, x.shape, 1)
    up = pltpu.roll(x, LANES - 16, 1)
    dn = pltpu.roll(x, 16, 1)
    partner = jnp.where((lane % 32) < 16, up, dn)
    return x * cos + partner * sin_signed


def _rope_tables():
    t = np.arange(T_LAT)
    n_f = HD // 4
    inv = ROPE_THETA ** (-np.arange(n_f, dtype=np.float32) / n_f)
    cos = np.zeros((T_LAT, HD), np.float32)
    sin = np.zeros((T_LAT, HD), np.float32)
    for half, pos in ((0, t // GRID_W), (1, t % GRID_W)):
        ang = pos[:, None].astype(np.float32) * inv[None, :]
        c, s = np.cos(ang), np.sin(ang)
        base = half * (HD // 2)
        cos[:, base:base + n_f] = c
        cos[:, base + n_f:base + 2 * n_f] = c
        sin[:, base:base + n_f] = -s
        sin[:, base + n_f:base + 2 * n_f] = s
    return np.tile(cos, (1, 2)), np.tile(sin, (1, 2))


def _attn_a_ctx_kernel(q_ref, kv_ref, qg_ref, kg_ref, o_ref, ko_ref):
    q = q_ref[...]
    kv = kv_ref[...]
    k = kv[:, :A_KV_W]
    v = kv[:, A_KV_W:]
    kn = _head_rmsnorm(k, kg_ref[...], A_KV_HEADS)
    ko_ref[...] = jnp.concatenate(kn, axis=-1)
    qn = _head_rmsnorm(q, qg_ref[...], A_HEADS)
    group = A_HEADS // A_KV_HEADS
    outs = []
    for h in range(A_HEADS):
        g = h // group
        kh = kn[g].astype(BF16)
        vh = v[:, g * HD:(g + 1) * HD].astype(BF16)
        qh = (qn[h] * ATT_SCALE).astype(BF16)
        outs.append(_softmax_pv([_qk(qh, kh)], [vh]))
    o_ref[...] = jnp.concatenate(outs, axis=-1).astype(BF16)


def attn_a_ctx(z_a, qg, kg):
    t = T_CTX
    return pl.pallas_call(
        _attn_a_ctx_kernel,
        grid=(N_CTX,),
        in_specs=[pl.BlockSpec((t, A_W), lambda b: (b, 0)),
                  pl.BlockSpec((t, 2 * A_KV_W), lambda b: (b, A_W // (2 * A_KV_W))),
                  pl.BlockSpec((1, HD), lambda b: (0, 0)),
                  pl.BlockSpec((1, HD), lambda b: (0, 0))],
        out_specs=[pl.BlockSpec((t, A_W), lambda b: (b, 0)),
                   pl.BlockSpec((t, A_KV_W), lambda b: (b, 0))],
        out_shape=[jax.ShapeDtypeStruct((ROWS_CTX, A_W), BF16),
                   jax.ShapeDtypeStruct((ROWS_CTX, A_KV_W), F32)],
        compiler_params=_cparams(("arbitrary",)),
        name="attn_a_ctx",
    )(z_a, z_a, qg, kg)


A_LAT_TQ = 256


def _attn_a_lat_kernel(q_ref, kv_ref, ck_ref, cv_ref, qg_ref, kg_ref, cq_ref, sq_ref, ckk_ref, skk_ref,
                       o_ref, k_s, v_s):
    @pl.when(pl.program_id(1) == 0)
    def _():
        kv = kv_ref[...]
        kn = jnp.concatenate(_head_rmsnorm(kv[:, :A_KV_W], kg_ref[...], A_KV_HEADS), axis=-1)
        k_s[...] = _rope128(kn, ckk_ref[...], skk_ref[...]).astype(BF16)
        v_s[...] = kv[:, A_KV_W:].astype(BF16)

    qn = _head_rmsnorm(q_ref[...], qg_ref[...], A_HEADS)
    cos, sin = cq_ref[...], sq_ref[...]
    qr = []
    for c in range(A_HEADS // 2):
        slab = _rope128(jnp.concatenate(qn[2 * c:2 * c + 2], axis=-1), cos, sin)
        qr.append(slab[:, :HD])
        qr.append(slab[:, HD:])
    ck = ck_ref[0].astype(BF16)
    cv = cv_ref[0].astype(BF16)
    kk = k_s[...]
    vv = v_s[...]
    group = A_HEADS // A_KV_HEADS
    outs = []
    for h in range(A_HEADS):
        g = h // group
        sl = slice(g * HD, (g + 1) * HD)
        qh = (qr[h] * ATT_SCALE).astype(BF16)
        outs.append(_softmax_pv([_qk(qh, kk[:, sl]), _qk(qh, ck[:, sl])], [vv[:, sl], cv[:, sl]]))
    o_ref[...] = jnp.concatenate(outs, axis=-1).astype(BF16)


def attn_a_lat(z_a, ck, cv, qg, kg):
    tq = A_LAT_TQ
    nq = T_LAT // tq
    cos, sin = _rope_tables()
    cos, sin = jnp.asarray(cos), jnp.asarray(sin)
    row0 = ROWS_CTX // tq
    seq0 = ROWS_CTX // T_LAT
    return pl.pallas_call(
        _attn_a_lat_kernel,
        grid=(N_LAT, nq),
        in_specs=[pl.BlockSpec((tq, A_W), lambda b, i: (row0 + b * nq + i, 0)),
                  pl.BlockSpec((T_LAT, 2 * A_KV_W), lambda b, i: (seq0 + b, A_W // (2 * A_KV_W))),
                  pl.BlockSpec((1, PAST, A_KV_W), lambda b, i: (b, 0, 0)),
                  pl.BlockSpec((1, PAST, A_KV_W), lambda b, i: (b, 0, 0)),
                  pl.BlockSpec((1, HD), lambda b, i: (0, 0)),
                  pl.BlockSpec((1, HD), lambda b, i: (0, 0)),
                  pl.BlockSpec((tq, LANES), lambda b, i: (i, 0)),
                  pl.BlockSpec((tq, LANES), lambda b, i: (i, 0)),
                  pl.BlockSpec((T_LAT, LANES), lambda b, i: (0, 0)),
                  pl.BlockSpec((T_LAT, LANES), lambda b, i: (0, 0))],
        out_specs=pl.BlockSpec((tq, A_W), lambda b, i: (b * nq + i, 0)),
        out_shape=jax.ShapeDtypeStruct((ROWS_LAT, A_W), BF16),
        scratch_shapes=[pltpu.VMEM((T_LAT, A_KV_W), BF16),
                        pltpu.VMEM((T_LAT, A_KV_W), BF16)],
        compiler_params=_cparams(("arbitrary", "arbitrary")),
        name="attn_a_lat",
    )(z_a, z_a, ck, cv, qg, kg, cos, sin, cos, sin)


def _attn_c_ctx_kernel(q_ref, k_ref, v_ref, o_ref):
    q = q_ref[...]
    k = k_ref[...]
    v = v_ref[...]
    outs = []
    for h in range(LANES // HD):
        sl = slice(h * HD, (h + 1) * HD)
        qh = (q[:, sl] * ATT_SCALE).astype(BF16)
        outs.append(_softmax_pv([_qk(qh, k[:, sl].astype(BF16))], [v[:, sl].astype(BF16)]))
    o_ref[...] = jnp.concatenate(outs, axis=-1).astype(BF16)


def attn_c_ctx(z):
    t = T_CTX
    nhp = C_W // LANES
    return pl.pallas_call(
        _attn_c_ctx_kernel,
        grid=(N_CTX, nhp),
        in_specs=[pl.BlockSpec((t, LANES), lambda b, p: (b, p)),
                  pl.BlockSpec((t, LANES), lambda b, p: (b, nhp + p)),
                  pl.BlockSpec((t, LANES), lambda b, p: (b, 2 * nhp + p))],
        out_specs=pl.BlockSpec((t, LANES), lambda b, p: (b, p)),
        out_shape=jax.ShapeDtypeStruct((ROWS_CTX, C_W), BF16),
        compiler_params=_cparams(("arbitrary", "arbitrary")),
        name="attn_c_ctx",
    )(z, z, z)


NA_GRID_ROWS = T_LAT // GRID_W
NA_WIN = NA_ROWS * GRID_W
NA_PATTERNS = 8


def _na_row_start(r):
    return min(max(r - NA_ROWS // 2, 0), NA_GRID_ROWS - NA_ROWS)


def _na_pattern(r):
    p = _na_row_start(r) - r + (NA_ROWS - 1)
    return (NA_ROWS - 1) - p


def _na_bias_table(rpb):
    col = np.arange(GRID_W)
    cs = np.clip(col - NA_COLS // 2, 0, GRID_W - NA_COLS)
    col_in = (col[None, :] >= cs[:, None]) & (col[None, :] < cs[:, None] + NA_COLS)
    dc = np.clip(col[None, :] - col[:, None], -(NA_COLS - 1), NA_COLS - 1) + (NA_COLS - 1)
    dr = np.arange(NA_ROWS)[None, :] + (NA_ROWS - 1) - np.arange(NA_PATTERNS)[:, None]
    tab = rpb[:, dr[:, None, :, None], dc[None, :, None, :]]
    tab = jnp.where(jnp.asarray(col_in)[None, None, :, None, :], tab, NEG_INF)
    return tab.reshape(C_HEADS, NA_PATTERNS, GRID_W, NA_WIN)


def _attn_na_kernel(q_ref, k_ref, v_ref, ck_ref, cv_ref, bias_ref, o_ref):
    heads = []
    for h in range(LANES // HD):
        sl = slice(h * HD, (h + 1) * HD)
        q = (q_ref[:, sl] * ATT_SCALE).astype(BF16)
        k = k_ref[:, sl].astype(BF16)
        v = v_ref[:, sl].astype(BF16)
        ck = ck_ref[0][:, sl].astype(BF16)
        cv = cv_ref[0][:, sl].astype(BF16)
        rows = []
        for r in range(NA_GRID_ROWS):
            qr = q[r * GRID_W:(r + 1) * GRID_W]
            w0 = _na_row_start(r) * GRID_W
            kw = k[w0:w0 + NA_WIN]
            vw = v[w0:w0 + NA_WIN]
            s_nb = _qk(qr, kw) + bias_ref[h, _na_pattern(r)]
            s_cx = _qk(qr, ck)
            rows.append(_softmax_pv([s_nb, s_cx], [vw, cv]))
        heads.append(jnp.concatenate(rows, axis=0))
    o_ref[...] = jnp.concatenate(heads, axis=-1).astype(BF16)


def attn_na(z, ck, cv, bias):
    nhp = C_W // LANES
    hpb = LANES // HD
    seq0 = ROWS_CTX // T_LAT
    return pl.pallas_call(
        _attn_na_kernel,
        grid=(nhp, N_LAT),
        in_specs=[pl.BlockSpec((T_LAT, LANES), lambda p, b: (seq0 + b, p)),
                  pl.BlockSpec((T_LAT, LANES), lambda p, b: (seq0 + b, nhp + p)),
                  pl.BlockSpec((T_LAT, LANES), lambda p, b: (seq0 + b, 2 * nhp + p)),
                  pl.BlockSpec((1, PAST, LANES), lambda p, b: (b, 0, p)),
                  pl.BlockSpec((1, PAST, LANES), lambda p, b: (b, 0, p)),
                  pl.BlockSpec((hpb, NA_PATTERNS, GRID_W, NA_WIN), lambda p, b: (p, 0, 0, 0))],
        out_specs=pl.BlockSpec((T_LAT, LANES), lambda p, b: (b, p)),
        out_shape=jax.ShapeDtypeStruct((ROWS_LAT, C_W), BF16),
        compiler_params=_cparams(("arbitrary", "arbitrary")),
        name="attn_na",
    )(z, z, z, ck, cv, bias)


def _seg_sum(x, n_heads):
    parts = []
    for h in range(n_heads):
        s = jnp.sum(x[:, h * HD:(h + 1) * HD], axis=-1, keepdims=True)
        parts.append(jnp.broadcast_to(s, (x.shape[0], HD)))
    return jnp.concatenate(parts, axis=-1)


PREP_TM = 256


def _seq_len_at(row_start):
    return jnp.where(row_start < ROWS_CTX, T_CTX, T_LAT)


def _rwkv_prep_kernel(zb_ref, zprev_ref, znext_ref, mu_ref, kkw_ref, w0_ref, w2_ref, a0_ref, a2_ref, ka_ref,
                      rk_ref, g2_ref, r_ref, v_ref, kk_ref, w_ref, kd_ref, bb_ref, g_ref, bonus_ref):
    z = zb_ref[...]
    t = z.shape[0]
    start = pl.program_id(0) * t
    seq_len = _seq_len_at(start)
    pos = (start - jnp.where(start < ROWS_CTX, 0, ROWS_CTX)) % seq_len
    halo_prev = jnp.where(pos == 0, 0.0, zprev_ref[SUBLANES - 1:SUBLANES, :])
    halo_next = jnp.where(pos + t == seq_len, 0.0, znext_ref[0:1, :])
    row = lax.broadcasted_iota(jnp.int32, (t, 1), 0)
    prev = jnp.where(row == 0, halo_prev, pltpu.roll(z, 1, 0))
    nxt = jnp.where(row == t - 1, halo_next, pltpu.roll(z, t - 1, 0))
    m = z + mu_ref[...] * (0.5 * (prev + nxt) - z)
    r = m[:, :B_W]
    k = m[:, B_W:2 * B_W]
    v = m[:, 2 * B_W:3 * B_W]
    o = 3 * B_W
    wd = m[:, o:o + 2 * LORA_W]
    ad = m[:, o + 2 * LORA_W:o + 2 * LORA_W + 2 * LORA_A]
    gd = m[:, o + 2 * LORA_W + 2 * LORA_A:]

    kkr = k * kkw_ref[...]
    kk = kkr * lax.rsqrt(_seg_sum(kkr * kkr, B_HEADS) + 1e-12)

    wl = w0_ref[...] + jnp.dot(jnp.tanh(wd), w2_ref[...], precision=HIGHEST, preferred_element_type=F32)
    decay = jnp.exp(-float(np.exp(-0.5)) * jax.nn.sigmoid(wl))
    a = jax.nn.sigmoid(a0_ref[...] + jnp.dot(ad, a2_ref[...], precision=HIGHEST, preferred_element_type=F32))
    k2 = jnp.concatenate([k, k], axis=-1)
    ka2 = jnp.concatenate([ka_ref[...], ka_ref[...]], axis=-1)
    kk2 = jnp.concatenate([kk, kk], axis=-1)

    r_ref[...] = r
    v_ref[...] = v
    kk_ref[...] = kk
    w_ref[...] = decay
    kd_ref[...] = k2 * (1.0 + (a - 1.0) * ka2)
    bb_ref[...] = kk2 * a
    g_ref[...] = jnp.dot(jax.nn.sigmoid(gd).astype(BF16), g2_ref[...].astype(BF16), preferred_element_type=F32)
    bonus_ref[...] = _seg_sum(r * k * rk_ref[...], B_HEADS) * v


def rwkv_prep(z_b, params):
    tm = PREP_TM
    full = lambda a: pl.BlockSpec(a.shape, lambda s: (0,) * a.ndim)
    widths = (B_W, B_W, B_W, 2 * B_W, 2 * B_W, 2 * B_W, B_W, B_W)
    per = tm // SUBLANES
    last = ROWS // SUBLANES - 1
    return pl.pallas_call(
        _rwkv_prep_kernel,
        grid=(ROWS // tm,),
        in_specs=[pl.BlockSpec((tm, B_IN), lambda s: (s, 0)),
                  pl.BlockSpec((SUBLANES, B_IN), lambda s: (jnp.maximum(s * per - 1, 0), 0)),
                  pl.BlockSpec((SUBLANES, B_IN), lambda s: (jnp.minimum((s + 1) * per, last), 0))]
                 + [full(p) for p in params],
        out_specs=[pl.BlockSpec((tm, w), lambda s: (s, 0)) for w in widths],
        out_shape=[jax.ShapeDtypeStruct((ROWS, w), F32) for w in widths],
        compiler_params=_cparams(("arbitrary",)),
        name="rwkv_prep",
    )(z_b, z_b, z_b, *params)


def _rwkv_scan_kernel(r_ref, w_ref, k_ref, kk_ref, b_ref, v_ref, s0_ref, y_ref, st_ref, s_ref, *, ni, tc):
    c = pl.program_id(1)

    @pl.when(c == 0)
    def _():
        s_ref[...] = s0_ref[...]

    def bcast(ref, t, j):
        return jnp.broadcast_to(ref[t, pl.ds(j, 1), :], (SUBLANES, LANES))

    def step(t, carry):
        sa = [jnp.zeros((SUBLANES, LANES), F32) for _ in range(ni)]
        for j in range(HD):
            kkj = bcast(kk_ref, t, j)
            for g in range(ni):
                sa[g] = sa[g] - s_ref[j, pl.ds(g * SUBLANES, SUBLANES), :] * kkj
        vv = [v_ref[t, pl.ds(g * SUBLANES, SUBLANES), :] for g in range(ni)]
        ya = [jnp.zeros((SUBLANES, LANES), F32) for _ in range(ni)]
        for j in range(HD):
            wj = bcast(w_ref, t, j)
            bj = bcast(b_ref, t, j)
            kj = bcast(k_ref, t, j)
            rj = bcast(r_ref, t, j)
            for g in range(ni):
                sl = pl.ds(g * SUBLANES, SUBLANES)
                s_new = s_ref[j, sl, :] * wj + sa[g] * bj + vv[g] * kj
                s_ref[j, sl, :] = s_new
                ya[g] = ya[g] + s_new * rj
        for g in range(ni):
            y_ref[t, pl.ds(g * SUBLANES, SUBLANES), :] = ya[g]
        return carry

    lax.fori_loop(0, tc, step, 0)

    @pl.when(c == pl.num_programs(1) - 1)
    def _():
        st_ref[...] = s_ref[...]


def rwkv_scan(r, w, k, kk, bb, v, s0, tc):
    t, _, lanes = r.shape
    ni8 = v.shape[1]
    ni = ni8 // SUBLANES
    vec = pl.BlockSpec((tc, HD, LANES), lambda g, c: (c, 0, g))
    val = pl.BlockSpec((tc, ni8, LANES), lambda g, c: (c, 0, g))
    st = pl.BlockSpec((HD, ni8, LANES), lambda g, c: (0, 0, g))
    return pl.pallas_call(
        functools.partial(_rwkv_scan_kernel, ni=ni, tc=tc),
        grid=(lanes // LANES, t // tc),
        in_specs=[vec, vec, vec, vec, vec, val, st],
        out_specs=[val, st],
        out_shape=[jax.ShapeDtypeStruct((t, ni8, lanes), F32),
                   jax.ShapeDtypeStruct((HD, ni8, lanes), F32)],
        scratch_shapes=[pltpu.VMEM((HD, ni8, LANES), F32)],
        compiler_params=_cparams(("arbitrary", "arbitrary")),
        name="rwkv_scan",
    )(r, w, k, kk, bb, v, s0)


def _rwkv_post_kernel(yf_ref, yb_ref, bonus_ref, g_ref, lng_ref, lnb_ref, o_ref):
    y = yf_ref[...] + yb_ref[...]
    mu = _seg_sum(y, B_HEADS) * (1.0 / HD)
    yc = y - mu
    var = _seg_sum(yc * yc, B_HEADS) * (1.0 / HD)
    yn = yc * lax.rsqrt(var + GN_EPS) * lng_ref[...] + lnb_ref[...]
    o_ref[...] = ((yn + bonus_ref[...]) * g_ref[...]).astype(BF16)


def rwkv_post(yf, yb, bonus, g, lng, lnb):
    tm = 512
    rows = yf.shape[0]
    rowspec = pl.BlockSpec((tm, B_W), lambda i: (i, 0))
    vecspec = pl.BlockSpec((1, B_W), lambda i: (0, 0))
    return pl.pallas_call(
        _rwkv_post_kernel,
        grid=(rows // tm,),
        in_specs=[rowspec, rowspec, rowspec, rowspec, vecspec, vecspec],
        out_specs=rowspec,
        out_shape=jax.ShapeDtypeStruct((rows, B_W), BF16),
        compiler_params=_cparams(("arbitrary",)),
        name="rwkv_post",
    )(yf, yb, bonus, g, lng, lnb)


def _heads_time_major(x, n_seq, t):
    return x.reshape(n_seq, t, B_HEADS, HD).transpose(1, 3, 0, 2).reshape(t, HD, n_seq * B_HEADS)


def _both_dirs(xf, xb):
    return jnp.concatenate([xf, xb[::-1]], axis=-1)


def _from_scan(y, n_seq, t):
    return y.reshape(t, HD, n_seq, B_HEADS).transpose(2, 0, 3, 1).reshape(n_seq * t, B_W)


def rwkv_prep_all(z_b, prm, e):
    w2 = prm['b_w2'][e]
    a2 = prm['b_a2'][e]
    zero = jnp.zeros((LORA_W, B_W), F32)
    w2bd = jnp.concatenate([jnp.concatenate([w2[0], zero], 1), jnp.concatenate([zero, w2[1]], 1)], 0)
    a2bd = jnp.concatenate([jnp.concatenate([a2[0], zero], 1), jnp.concatenate([zero, a2[1]], 1)], 0)
    params = (prm['b_mu'][e][None], prm['b_kk'][e][None], prm['b_w0'][e].reshape(1, 2 * B_W), w2bd,
              prm['b_a0'][e].reshape(1, 2 * B_W), a2bd, prm['b_ka'][e][None],
              prm['b_rk'][e].reshape(1, B_W), prm['b_g2'][e])
    return rwkv_prep(z_b, params)


def rwkv_scan_group(prep, row0, n_seq, t, s0f, s0b, tc):
    r, v, kk, w, kd, bb = [a[row0:row0 + n_seq * t] for a in prep[:6]]

    tm = functools.partial(_heads_time_major, n_seq=n_seq, t=t)
    n = n_seq * B_HEADS
    rep = LANES // (2 * n) if 2 * n < LANES else 1

    def vecs(xf, xb):
        x = _both_dirs(tm(xf), tm(xb))
        return jnp.tile(x, (1, 1, rep)) if rep > 1 else x

    def fold_i(x):
        if rep == 1:
            return x
        lead = x.shape[:-2]
        nl = len(lead)
        x = x.reshape(*lead, HD // (rep * SUBLANES), rep, SUBLANES, 2 * n)
        x = jnp.swapaxes(x, nl + 1, nl + 2)
        return x.reshape(*lead, HD // rep, rep * 2 * n)

    def unfold_i(x):
        if rep == 1:
            return x
        lead = x.shape[:-2]
        nl = len(lead)
        x = x.reshape(*lead, HD // (rep * SUBLANES), SUBLANES, rep, 2 * n)
        x = jnp.swapaxes(x, nl + 1, nl + 2)
        return x.reshape(*lead, HD, 2 * n)

    rs = vecs(r, r)
    kks = vecs(kk, kk)
    ws = vecs(w[:, :B_W], w[:, B_W:])
    ks = vecs(kd[:, :B_W], kd[:, B_W:])
    bs = vecs(bb[:, :B_W], bb[:, B_W:])
    vs = fold_i(_both_dirs(tm(v), tm(v)))
    st = lambda s: s.transpose(3, 2, 0, 1).reshape(HD, HD, n)
    s0 = fold_i(jnp.concatenate([st(s0f), st(s0b)], axis=-1))

    y, s_fin = rwkv_scan(rs, ws, ks, kks, bs, vs, s0, tc)

    y = unfold_i(y)
    yf = _from_scan(y[:, :, :n], n_seq, t)
    yb = _from_scan(y[::-1, :, n:], n_seq, t)
    s_fin = unfold_i(s_fin)
    unst = lambda s: s.reshape(HD, HD, n_seq, B_HEADS).transpose(2, 3, 1, 0)
    sf, sb = unst(s_fin[:, :, :n]), unst(s_fin[:, :, n:])
    return yf, yb, sf, sb


def kernel(x_prompt, x_sample, cache_a_k, cache_a_v, state_b_fwd, state_b_bwd, cache_c_k, cache_c_v, c, c_ctx,
           ada_w, ada_b, norm1_g, norm2_g, final_norm_g, w_in_e, w_out_e, a_q_gain, a_k_gain, b_mu, b_w0, b_w2,
           b_a0, b_a2, b_g2, b_kk, b_ka, b_rk, b_ln_g, b_ln_b, ffn_w_gu, ffn_w_dn, w_in_o, w_out_o, c_rpb,
           router_w, router_b, moe_w_gu, moe_w_dn):
    prm = dict(b_mu=b_mu, b_w0=b_w0, b_w2=b_w2, b_a0=b_a0, b_a2=b_a2, b_g2=b_g2, b_kk=b_kk, b_ka=b_ka,
               b_rk=b_rk, b_ln_g=b_ln_g, b_ln_b=b_ln_b)
    x = jnp.concatenate([x_prompt.reshape(ROWS_CTX, D), x_sample.reshape(ROWS_LAT, D)], axis=0)
    cond = jnp.zeros((N_MODS, D), F32).at[0].set(c_ctx).at[1:1 + N_LAT].set(c)
    mods_all = ada_all(cond, ada_w, ada_b)
    mods = [mods_all[l].reshape(N_MODS, 6, D) for l in range(DEPTH)]

    h = modnorm(x, norm1_g[0][None], mods[0], 0, 1)

    new_a_k, new_a_v, new_sf, new_sb, new_c_k, new_c_v = [], [], [], [], [], []
    y_final = None
    for l in range(DEPTH):
        if l % 2 == 0:
            e = l // 2
            z_a = mm_in(h, w_in_e[e][:, :A_IN], tn=A_IN)
            z_b = mm_in(h, w_in_e[e][:, A_IN:], tn=B_IN // 3)
            qg, kg = a_q_gain[e][None], a_k_gain[e][None]
            ya_ctx, k_new = attn_a_ctx(z_a, qg, kg)
            ya_lat = attn_a_lat(z_a, cache_a_k[:, e].reshape(N_LAT, PAST, A_KV_W),
                                cache_a_v[:, e].reshape(N_LAT, PAST, A_KV_W), qg, kg)
            new_a_k.append(k_new.reshape(N_CTX, T_CTX, A_KV_HEADS, HD))
            new_a_v.append(z_a[:ROWS_CTX, A_W + A_KV_W:].reshape(N_CTX, T_CTX, A_KV_HEADS, HD))
            zeros = jnp.zeros((N_CTX, B_HEADS, HD, HD), F32)
            prep = rwkv_prep_all(z_b, prm, e)
            yf_c, yb_c, sf, sb = rwkv_scan_group(prep, 0, N_CTX, T_CTX, zeros, zeros, tc=32)
            yf_l, yb_l, _, _ = rwkv_scan_group(prep, ROWS_CTX, N_LAT, T_LAT, state_b_fwd[:, e],
                                               state_b_bwd[:, e], tc=64)
            new_sf.append(sf)
            new_sb.append(sb)
            y_b = rwkv_post(jnp.concatenate([yf_c, yf_l], axis=0), jnp.concatenate([yb_c, yb_l], axis=0),
                            prep[7], prep[6], b_ln_g[e][None], b_ln_b[e][None])
            y = jnp.concatenate([jnp.concatenate([ya_ctx, ya_lat], axis=0), y_b], axis=1)
            x, h = mm_out(y, w_out_e[e], x, mods[l], mods[l], norm2_g[l][None], 2, 3, 4)
            x, h = ffn(h, ffn_w_gu[e], ffn_w_dn[e], x, mods[l], mods[l + 1], norm1_g[l + 1][None], 5, 0, 1)
        else:
            o = l // 2
            z = mm_in(h, w_in_o[o], tn=C_W)
            y_ctx = attn_c_ctx(z)
            bias = _na_bias_table(c_rpb[o])
            y_lat = attn_na(z, cache_c_k[:, o].reshape(N_LAT, PAST, C_W),
                            cache_c_v[:, o].reshape(N_LAT, PAST, C_W), bias)
            new_c_k.append(z[:ROWS_CTX, C_W:2 * C_W].reshape(N_CTX, T_CTX, C_HEADS, HD))
            new_c_v.append(z[:ROWS_CTX, 2 * C_W:].reshape(N_CTX, T_CTX, C_HEADS, HD))
            y = jnp.concatenate([y_ctx, y_lat], axis=0)
            x, h = mm_out(y, w_out_o[o], x, mods[l], mods[l], norm2_g[l][None], 2, 3, 4)
            gates = router(x, norm2_g[l][None], mods[l], router_w[o], router_b[o], 3, 4)
            y_moe = moe_dense(h, gates, moe_w_gu[o], moe_w_dn[o])
            if l + 1 < DEPTH:
                x, h = resid(x, y_moe, mods[l], mods[l + 1], norm1_g[l + 1][None], 5, 0, 1, False)
            else:
                (y_final,) = resid(x, y_moe, mods[l], mods[l], final_norm_g[None], 5, 0, 1, True)

    y_prompt = y_final[:ROWS_CTX].reshape(N_CTX, T_CTX, D)
    y_sample = y_final[ROWS_CTX:].reshape(N_LAT, T_LAT, D)
    return (y_prompt, y_sample, jnp.stack(new_a_k, axis=1), jnp.stack(new_a_v, axis=1),
            jnp.stack(new_sf, axis=1), jnp.stack(new_sb, axis=1),
            jnp.stack(new_c_k, axis=1), jnp.stack(new_c_v, axis=1))
```

```python
import functools

import numpy as np
import jax
import jax.numpy as jnp
from jax import lax
from jax.experimental import pallas as pl
from jax.experimental.pallas import tpu as pltpu

F32 = jnp.float32
BF16 = jnp.bfloat16
HIGHEST = lax.Precision.HIGHEST

D = 1024
N_CTX, T_CTX = 16, 256
N_LAT, T_LAT = 2, 1024
ROWS_CTX = N_CTX * T_CTX
ROWS_LAT = N_LAT * T_LAT
ROWS = ROWS_CTX + ROWS_LAT
DEPTH = 4
GRID_W = 64
HD = 64
A_HEADS, A_KV_HEADS, B_HEADS, C_HEADS = 8, 2, 8, 16
A_W, A_KV_W, B_W, C_W = A_HEADS * HD, A_KV_HEADS * HD, B_HEADS * HD, C_HEADS * HD
A_IN = A_W + 2 * A_KV_W
LORA_W, LORA_A, LORA_G = 64, 64, 128
B_IN = 3 * B_W + 2 * LORA_W + 2 * LORA_A + LORA_G
PAST = 512
NA_ROWS, NA_COLS = 8, 16
D_FF = 2816
N_EXPERTS = 8
D_EXPERT = 1408
ROPE_THETA = 10000.0
EPS = 1e-6
GN_EPS = 64e-5
NEG_INF = -1e30
ATT_SCALE = HD ** -0.5

LANES = 128
SUBLANES = 8
VMEM_LIMIT = 56 * 1024 * 1024

N_MODS = 8


def _cparams(sem):
    return pltpu.CompilerParams(dimension_semantics=sem, vmem_limit_bytes=VMEM_LIMIT)


def _mod_index(row_start):
    return jnp.where(row_start < ROWS_CTX, 0, 1 + (row_start - ROWS_CTX) // T_LAT)


def _modnorm(x, g, shift, scale):
    ms = jnp.mean(x * x, axis=-1, keepdims=True)
    return (x * lax.rsqrt(ms + EPS) * g) * (1.0 + scale) + shift


def _silu(x):
    return x * jax.nn.sigmoid(x)


def _ada_kernel(c_ref, w_ref, b_ref, o_ref):
    s = _silu(c_ref[...]).astype(BF16)
    o_ref[0] = jnp.dot(s, w_ref[0].astype(BF16), preferred_element_type=F32) + b_ref[0]


def ada_all(cond, ada_w, ada_b):
    tn = 1536
    n = 6 * D
    return pl.pallas_call(
        _ada_kernel,
        grid=(DEPTH, n // tn),
        in_specs=[pl.BlockSpec((N_MODS, D), lambda l, j: (0, 0)),
                  pl.BlockSpec((1, D, tn), lambda l, j: (l, 0, j)),
                  pl.BlockSpec((1, 1, tn), lambda l, j: (l, 0, j))],
        out_specs=pl.BlockSpec((1, N_MODS, tn), lambda l, j: (l, 0, j)),
        out_shape=jax.ShapeDtypeStruct((DEPTH, N_MODS, n), F32),
        compiler_params=_cparams(("arbitrary", "arbitrary")),
        name="ada",
    )(cond, ada_w, ada_b.reshape(DEPTH, 1, n))


def _modnorm_kernel(x_ref, g_ref, mod_ref, h_ref, *, shift_idx, scale_idx):
    shift = mod_ref[0, shift_idx:shift_idx + 1, :]
    scale = mod_ref[0, scale_idx:scale_idx + 1, :]
    h_ref[...] = _modnorm(x_ref[...], g_ref[...], shift, scale).astype(BF16)


def modnorm(x, g, mods, shift_idx, scale_idx):
    tm = 512
    return pl.pallas_call(
        functools.partial(_modnorm_kernel, shift_idx=shift_idx, scale_idx=scale_idx),
        grid=(ROWS // tm,),
        in_specs=[pl.BlockSpec((tm, D), lambda i: (i, 0)),
                  pl.BlockSpec((1, D), lambda i: (0, 0)),
                  pl.BlockSpec((1, 6, D), lambda i: (_mod_index(i * tm), 0, 0))],
        out_specs=pl.BlockSpec((tm, D), lambda i: (i, 0)),
        out_shape=jax.ShapeDtypeStruct((ROWS, D), BF16),
        compiler_params=_cparams(("arbitrary",)),
        name="modnorm",
    )(x, g, mods)


def _mm_in_kernel(h_ref, w_ref, o_ref, wb_ref):
    @pl.when(pl.program_id(1) == 0)
    def _():
        wb_ref[...] = w_ref[...].astype(BF16)

    o_ref[...] = jnp.dot(h_ref[...], wb_ref[...], preferred_element_type=F32)


def mm_in(h, w, tn, tm=512):
    k, n = w.shape
    return pl.pallas_call(
        _mm_in_kernel,
        grid=(n // tn, ROWS // tm),
        in_specs=[pl.BlockSpec((tm, k), lambda j, i: (i, 0)),
                  pl.BlockSpec((k, tn), lambda j, i: (0, j))],
        out_specs=pl.BlockSpec((tm, tn), lambda j, i: (i, j)),
        out_shape=jax.ShapeDtypeStruct((ROWS, n), F32),
        scratch_shapes=[pltpu.VMEM((k, tn), BF16)],
        compiler_params=_cparams(("arbitrary", "arbitrary")),
        name="mm_in",
    )(h, w)


def _residual_epilogue(x, acc, mod_ref, nmod_ref, g_ref, xo_ref, ho_ref, gate_idx, shift_idx, scale_idx, final):
    gate = mod_ref[0, gate_idx:gate_idx + 1, :]
    xn = x + gate * acc
    if final:
        ms = jnp.mean(xn * xn, axis=-1, keepdims=True)
        ho_ref[...] = xn * lax.rsqrt(ms + EPS) * g_ref[...]
    else:
        xo_ref[...] = xn
        shift = nmod_ref[0, shift_idx:shift_idx + 1, :]
        scale = nmod_ref[0, scale_idx:scale_idx + 1, :]
        ho_ref[...] = _modnorm(xn, g_ref[...], shift, scale).astype(BF16)


def _mm_out_kernel(y_ref, w_ref, x_ref, mod_ref, nmod_ref, g_ref, xo_ref, ho_ref, wb_ref, *,
                   gate_idx, shift_idx, scale_idx):
    @pl.when(pl.program_id(0) == 0)
    def _():
        wb_ref[...] = w_ref[...].astype(BF16)

    acc = jnp.dot(y_ref[...], wb_ref[...], preferred_element_type=F32)
    _residual_epilogue(x_ref[...], acc, mod_ref, nmod_ref, g_ref, xo_ref, ho_ref,
                       gate_idx, shift_idx, scale_idx, False)


def mm_out(y, w, x, mods, nmods, g, gate_idx, shift_idx, scale_idx):
    tm = 512
    k = w.shape[0]
    modspec = pl.BlockSpec((1, 6, D), lambda i: (_mod_index(i * tm), 0, 0))
    return pl.pallas_call(
        functools.partial(_mm_out_kernel, gate_idx=gate_idx, shift_idx=shift_idx, scale_idx=scale_idx),
        grid=(ROWS // tm,),
        in_specs=[pl.BlockSpec((tm, k), lambda i: (i, 0)),
                  pl.BlockSpec((k, D), lambda i: (0, 0)),
                  pl.BlockSpec((tm, D), lambda i: (i, 0)),
                  modspec, modspec,
                  pl.BlockSpec((1, D), lambda i: (0, 0))],
        out_specs=[pl.BlockSpec((tm, D), lambda i: (i, 0)),
                   pl.BlockSpec((tm, D), lambda i: (i, 0))],
        out_shape=[jax.ShapeDtypeStruct((ROWS, D), F32),
                   jax.ShapeDtypeStruct((ROWS, D), BF16)],
        scratch_shapes=[pltpu.VMEM((k, D), BF16)],
        compiler_params=_cparams(("arbitrary",)),
        name="mm_out",
    )(y, w, x, mods, nmods, g)


FFN_TF = 256
FFN_NF = D_FF // FFN_TF


def _ffn_kernel(h_ref, wg_ref, wu_ref, wd_ref, x_ref, mod_ref, nmod_ref, g_ref, xo_ref, ho_ref,
                wg_s, wu_s, wd_s, acc_ref, *, gate_idx, shift_idx, scale_idx):
    i = pl.program_id(0)
    f = pl.program_id(1)

    @pl.when(i == 0)
    def _():
        wg_s[f] = wg_ref[...].astype(BF16)
        wu_s[f] = wu_ref[...].astype(BF16)
        wd_s[f] = wd_ref[...].astype(BF16)

    @pl.when(f == 0)
    def _():
        acc_ref[...] = jnp.zeros_like(acc_ref)

    h = h_ref[...]
    gp = jnp.dot(h, wg_s[f], preferred_element_type=F32)
    up = jnp.dot(h, wu_s[f], preferred_element_type=F32)
    a = (_silu(gp) * up).astype(BF16)
    acc_ref[...] += jnp.dot(a, wd_s[f], preferred_element_type=F32)

    @pl.when(f == FFN_NF - 1)
    def _():
        _residual_epilogue(x_ref[...], acc_ref[...], mod_ref, nmod_ref, g_ref, xo_ref, ho_ref,
                           gate_idx, shift_idx, scale_idx, False)


def ffn(h, w_gu, w_dn, x, mods, nmods, g, gate_idx, shift_idx, scale_idx):
    tm = 512
    tf, nf = FFN_TF, FFN_NF

    def once(i, f):
        return jnp.where(i == 0, f, nf - 1)

    modspec = pl.BlockSpec((1, 6, D), lambda i, f: (_mod_index(i * tm), 0, 0))
    return pl.pallas_call(
        functools.partial(_ffn_kernel, gate_idx=gate_idx, shift_idx=shift_idx, scale_idx=scale_idx),
        grid=(ROWS // tm, nf),
        in_specs=[pl.BlockSpec((tm, D), lambda i, f: (i, 0)),
                  pl.BlockSpec((D, tf), lambda i, f: (0, once(i, f))),
                  pl.BlockSpec((D, tf), lambda i, f: (0, nf + once(i, f))),
                  pl.BlockSpec((tf, D), lambda i, f: (once(i, f), 0)),
                  pl.BlockSpec((tm, D), lambda i, f: (i, 0)),
                  modspec, modspec,
                  pl.BlockSpec((1, D), lambda i, f: (0, 0))],
        out_specs=[pl.BlockSpec((tm, D), lambda i, f: (i, 0)),
                   pl.BlockSpec((tm, D), lambda i, f: (i, 0))],
        out_shape=[jax.ShapeDtypeStruct((ROWS, D), F32),
                   jax.ShapeDtypeStruct((ROWS, D), BF16)],
        scratch_shapes=[pltpu.VMEM((nf, D, tf), BF16),
                        pltpu.VMEM((nf, D, tf), BF16),
                        pltpu.VMEM((nf, tf, D), BF16),
                        pltpu.VMEM((tm, D), F32)],
        compiler_params=_cparams(("arbitrary", "arbitrary")),
        name="ffn",
    )(h, w_gu, w_gu, w_dn, x, mods, nmods, g)


def _router_kernel(x_ref, g_ref, mod_ref, rw_ref, rb_ref, gates_ref, *, shift_idx, scale_idx):
    shift = mod_ref[0, shift_idx:shift_idx + 1, :]
    scale = mod_ref[0, scale_idx:scale_idx + 1, :]
    h = _modnorm(x_ref[...], g_ref[...], shift, scale)
    logits = jnp.dot(h, rw_ref[...], precision=HIGHEST, preferred_element_type=F32)
    lane = lax.broadcasted_iota(jnp.int32, logits.shape, 1)
    sel = jnp.where(lane < N_EXPERTS, logits + rb_ref[...], -jnp.inf)
    m1 = jnp.max(sel, axis=-1, keepdims=True)
    i1 = jnp.min(jnp.where(sel == m1, lane, LANES), axis=-1, keepdims=True)
    sel2 = jnp.where(lane == i1, -jnp.inf, sel)
    m2 = jnp.max(sel2, axis=-1, keepdims=True)
    i2 = jnp.min(jnp.where(sel2 == m2, lane, LANES), axis=-1, keepdims=True)
    l1 = jnp.sum(jnp.where(lane == i1, logits, 0.0), axis=-1, keepdims=True)
    l2 = jnp.sum(jnp.where(lane == i2, logits, 0.0), axis=-1, keepdims=True)
    mx = jnp.maximum(l1, l2)
    e1 = jnp.exp(l1 - mx)
    e2 = jnp.exp(l2 - mx)
    den = e1 + e2
    gates_ref[...] = jnp.where(lane == i1, e1 / den, 0.0) + jnp.where(lane == i2, e2 / den, 0.0)


def router(x, g, mods, rw, rb, shift_idx, scale_idx):
    tm = 512
    rw_p = jnp.zeros((D, LANES), F32).at[:, :N_EXPERTS].set(rw)
    rb_p = jnp.zeros((1, LANES), F32).at[0, :N_EXPERTS].set(rb)
    return pl.pallas_call(
        functools.partial(_router_kernel, shift_idx=shift_idx, scale_idx=scale_idx),
        grid=(ROWS // tm,),
        in_specs=[pl.BlockSpec((tm, D), lambda i: (i, 0)),
                  pl.BlockSpec((1, D), lambda i: (0, 0)),
                  pl.BlockSpec((1, 6, D), lambda i: (_mod_index(i * tm), 0, 0)),
                  pl.BlockSpec((D, LANES), lambda i: (0, 0)),
                  pl.BlockSpec((1, LANES), lambda i: (0, 0))],
        out_specs=pl.BlockSpec((tm, LANES), lambda i: (i, 0)),
        out_shape=jax.ShapeDtypeStruct((ROWS, LANES), F32),
        compiler_params=_cparams(("arbitrary",)),
        name="router",
    )(x, g, mods, rw_p, rb_p)


def _moe_kernel(h_ref, gates_ref, wgu_ref, wdn_ref, yin_ref, yo_ref, wgu_s, wdn_s):
    e = pl.program_id(0)

    @pl.when(pl.program_id(1) == 0)
    def _():
        wgu_s[...] = wgu_ref[0].astype(BF16)
        wdn_s[...] = wdn_ref[0].astype(BF16)

    gu = jnp.dot(h_ref[...], wgu_s[...], preferred_element_type=F32)
    a = (_silu(gu[:, :D_EXPERT]) * gu[:, D_EXPERT:]).astype(BF16)
    y = jnp.dot(a, wdn_s[...], preferred_element_type=F32)
    gt = gates_ref[...]
    lane = lax.broadcasted_iota(jnp.int32, gt.shape, 1)
    gate = jnp.sum(jnp.where(lane == e, gt, 0.0), axis=-1, keepdims=True)
    yo_ref[...] = yin_ref[...] + gate * y


def moe_dense(h, gates, w_gu, w_dn):
    tm = 256
    y0 = jnp.zeros((ROWS, D), F32)
    return pl.pallas_call(
        _moe_kernel,
        grid=(N_EXPERTS, ROWS // tm),
        in_specs=[pl.BlockSpec((tm, D), lambda e, i: (i, 0)),
                  pl.BlockSpec((tm, LANES), lambda e, i: (i, 0)),
                  pl.BlockSpec((1, D, 2 * D_EXPERT), lambda e, i: (e, 0, 0), pipeline_mode=pl.Buffered(1)),
                  pl.BlockSpec((1, D_EXPERT, D), lambda e, i: (e, 0, 0), pipeline_mode=pl.Buffered(1)),
                  pl.BlockSpec((tm, D), lambda e, i: (i, 0))],
        out_specs=pl.BlockSpec((tm, D), lambda e, i: (i, 0)),
        out_shape=jax.ShapeDtypeStruct((ROWS, D), F32),
        scratch_shapes=[pltpu.VMEM((D, 2 * D_EXPERT), BF16),
                        pltpu.VMEM((D_EXPERT, D), BF16)],
        input_output_aliases={4: 0},
        compiler_params=_cparams(("arbitrary", "arbitrary")),
        name="moe",
    )(h, gates, w_gu, w_dn, y0)


def _resid_kernel(x_ref, y_ref, mod_ref, nmod_ref, g_ref, *out_refs, gate_idx, shift_idx, scale_idx, final):
    if final:
        xo_ref, ho_ref = None, out_refs[0]
    else:
        xo_ref, ho_ref = out_refs
    _residual_epilogue(x_ref[...], y_ref[...], mod_ref, nmod_ref, g_ref, xo_ref, ho_ref,
                       gate_idx, shift_idx, scale_idx, final)


def resid(x, y, mods, nmods, g, gate_idx, shift_idx, scale_idx, final):
    tm = 512
    modspec = pl.BlockSpec((1, 6, D), lambda i: (_mod_index(i * tm), 0, 0))
    rowspec = pl.BlockSpec((tm, D), lambda i: (i, 0))
    if final:
        out_specs = [rowspec]
        out_shape = [jax.ShapeDtypeStruct((ROWS, D), F32)]
    else:
        out_specs = [rowspec, rowspec]
        out_shape = [jax.ShapeDtypeStruct((ROWS, D), F32), jax.ShapeDtypeStruct((ROWS, D), BF16)]
    return pl.pallas_call(
        functools.partial(_resid_kernel, gate_idx=gate_idx, shift_idx=shift_idx, scale_idx=scale_idx,
                          final=final),
        grid=(ROWS // tm,),
        in_specs=[rowspec, rowspec, modspec, modspec, pl.BlockSpec((1, D), lambda i: (0, 0))],
        out_specs=out_specs,
        out_shape=out_shape,
        compiler_params=_cparams(("arbitrary",)),
        name="resid",
    )(x, y, mods, nmods, g)


def _softmax_pv(scores, values):
    m = None
    for s in scores:
        mi = jnp.max(s, axis=-1, keepdims=True)
        m = mi if m is None else jnp.maximum(m, mi)
    num, den = None, None
    for s, v in zip(scores, values):
        p = jnp.exp(s - m)
        li = jnp.sum(p, axis=-1, keepdims=True)
        oi = jnp.dot(p.astype(BF16), v, preferred_element_type=F32)
        num = oi if num is None else num + oi
        den = li if den is None else den + li
    return num / den


def _qk(q, k):
    return lax.dot_general(q, k, (((1,), (1,)), ((), ())), preferred_element_type=F32)


def _head_rmsnorm(x, gain, n_heads):
    parts = []
    for h in range(n_heads):
        xh = x[:, h * HD:(h + 1) * HD]
        ms = jnp.mean(xh * xh, axis=-1, keepdims=True)
        parts.append(xh * lax.rsqrt(ms + EPS) * gain)
    return parts


def _rope128(x, cos, sin_signed):
    lane = lax.broadcasted_iota(jnp.int32, x.shape, 1)
    up = pltpu.roll(x, LANES - 16, 1)
    dn = pltpu.roll(x, 16, 1)
    partner = jnp.where((lane % 32) < 16, up, dn)
    return x * cos + partner * sin_signed


def _rope_tables():
    t = np.arange(T_LAT)
    n_f = HD // 4
    inv = ROPE_THETA ** (-np.arange(n_f, dtype=np.float32) / n_f)
    cos = np.zeros((T_LAT, HD), np.float32)
    sin = np.zeros((T_LAT, HD), np.float32)
    for half, pos in ((0, t // GRID_W), (1, t % GRID_W)):
        ang = pos[:, None].astype(np.float32) * inv[None, :]
        c, s = np.cos(ang), np.sin(ang)
        base = half * (HD // 2)
        cos[:, base:base + n_f] = c
        cos[:, base + n_f:base + 2 * n_f] = c
        sin[:, base:base + n_f] = -s
        sin[:, base + n_f:base + 2 * n_f] = s
    return np.tile(cos, (1, 2)), np.tile(sin, (1, 2))


def _attn_a_ctx_kernel(q_ref, kv_ref, qg_ref, kg_ref, o_ref, ko_ref):
    q = q_ref[...]
    kv = kv_ref[...]
    k = kv[:, :A_KV_W]
    v = kv[:, A_KV_W:]
    kn = _head_rmsnorm(k, kg_ref[...], A_KV_HEADS)
    ko_ref[...] = jnp.concatenate(kn, axis=-1)
    qn = _head_rmsnorm(q, qg_ref[...], A_HEADS)
    group = A_HEADS // A_KV_HEADS
    outs = []
    for h in range(A_HEADS):
        g = h // group
        kh = kn[g].astype(BF16)
        vh = v[:, g * HD:(g + 1) * HD].astype(BF16)
        qh = (qn[h] * ATT_SCALE).astype(BF16)
        outs.append(_softmax_pv([_qk(qh, kh)], [vh]))
    o_ref[...] = jnp.concatenate(outs, axis=-1).astype(BF16)


def attn_a_ctx(z_a, qg, kg):
    t = T_CTX
    return pl.pallas_call(
        _attn_a_ctx_kernel,
        grid=(N_CTX,),
        in_specs=[pl.BlockSpec((t, A_W), lambda b: (b, 0)),
                  pl.BlockSpec((t, 2 * A_KV_W), lambda b: (b, A_W // (2 * A_KV_W))),
                  pl.BlockSpec((1, HD), lambda b: (0, 0)),
                  pl.BlockSpec((1, HD), lambda b: (0, 0))],
        out_specs=[pl.BlockSpec((t, A_W), lambda b: (b, 0)),
                   pl.BlockSpec((t, A_KV_W), lambda b: (b, 0))],
        out_shape=[jax.ShapeDtypeStruct((ROWS_CTX, A_W), BF16),
                   jax.ShapeDtypeStruct((ROWS_CTX, A_KV_W), F32)],
        compiler_params=_cparams(("arbitrary",)),
        name="attn_a_ctx",
    )(z_a, z_a, qg, kg)


A_LAT_TQ = 256


def _attn_a_lat_kernel(q_ref, kv_ref, ck_ref, cv_ref, qg_ref, kg_ref, cq_ref, sq_ref, ckk_ref, skk_ref,
                       o_ref, k_s, v_s):
    @pl.when(pl.program_id(1) == 0)
    def _():
        kv = kv_ref[...]
        kn = jnp.concatenate(_head_rmsnorm(kv[:, :A_KV_W], kg_ref[...], A_KV_HEADS), axis=-1)
        k_s[...] = _rope128(kn, ckk_ref[...], skk_ref[...]).astype(BF16)
        v_s[...] = kv[:, A_KV_W:].astype(BF16)

    qn = _head_rmsnorm(q_ref[...], qg_ref[...], A_HEADS)
    cos, sin = cq_ref[...], sq_ref[...]
    qr = []
    for c in range(A_HEADS // 2):
        slab = _rope128(jnp.concatenate(qn[2 * c:2 * c + 2], axis=-1), cos, sin)
        qr.append(slab[:, :HD])
        qr.append(slab[:, HD:])
    ck = ck_ref[0].astype(BF16)
    cv = cv_ref[0].astype(BF16)
    kk = k_s[...]
    vv = v_s[...]
    group = A_HEADS // A_KV_HEADS
    outs = []
    for h in range(A_HEADS):
        g = h // group
        sl = slice(g * HD, (g + 1) * HD)
        qh = (qr[h] * ATT_SCALE).astype(BF16)
        outs.append(_softmax_pv([_qk(qh, kk[:, sl]), _qk(qh, ck[:, sl])], [vv[:, sl], cv[:, sl]]))
    o_ref[...] = jnp.concatenate(outs, axis=-1).astype(BF16)


def attn_a_lat(z_a, ck, cv, qg, kg):
    tq = A_LAT_TQ
    nq = T_LAT // tq
    cos, sin = _rope_tables()
    cos, sin = jnp.asarray(cos), jnp.asarray(sin)
    row0 = ROWS_CTX // tq
    seq0 = ROWS_CTX // T_LAT
    return pl.pallas_call(
        _attn_a_lat_kernel,
        grid=(N_LAT, nq),
        in_specs=[pl.BlockSpec((tq, A_W), lambda b, i: (row0 + b * nq + i, 0)),
                  pl.BlockSpec((T_LAT, 2 * A_KV_W), lambda b, i: (seq0 + b, A_W // (2 * A_KV_W))),
                  pl.BlockSpec((1, PAST, A_KV_W), lambda b, i: (b, 0, 0)),
                  pl.BlockSpec((1, PAST, A_KV_W), lambda b, i: (b, 0, 0)),
                  pl.BlockSpec((1, HD), lambda b, i: (0, 0)),
                  pl.BlockSpec((1, HD), lambda b, i: (0, 0)),
                  pl.BlockSpec((tq, LANES), lambda b, i: (i, 0)),
                  pl.BlockSpec((tq, LANES), lambda b, i: (i, 0)),
                  pl.BlockSpec((T_LAT, LANES), lambda b, i: (0, 0)),
                  pl.BlockSpec((T_LAT, LANES), lambda b, i: (0, 0))],
        out_specs=pl.BlockSpec((tq, A_W), lambda b, i: (b * nq + i, 0)),
        out_shape=jax.ShapeDtypeStruct((ROWS_LAT, A_W), BF16),
        scratch_shapes=[pltpu.VMEM((T_LAT, A_KV_W), BF16),
                        pltpu.VMEM((T_LAT, A_KV_W), BF16)],
        compiler_params=_cparams(("arbitrary", "arbitrary")),
        name="attn_a_lat",
    )(z_a, z_a, ck, cv, qg, kg, cos, sin, cos, sin)


def _attn_c_ctx_kernel(q_ref, k_ref, v_ref, o_ref):
    q = q_ref[...]
    k = k_ref[...]
    v = v_ref[...]
    outs = []
    for h in range(LANES // HD):
        sl = slice(h * HD, (h + 1) * HD)
        qh = (q[:, sl] * ATT_SCALE).astype(BF16)
        outs.append(_softmax_pv([_qk(qh, k[:, sl].astype(BF16))], [v[:, sl].astype(BF16)]))
    o_ref[...] = jnp.concatenate(outs, axis=-1).astype(BF16)


def attn_c_ctx(z):
    t = T_CTX
    nhp = C_W // LANES
    return pl.pallas_call(
        _attn_c_ctx_kernel,
        grid=(N_CTX, nhp),
        in_specs=[pl.BlockSpec((t, LANES), lambda b, p: (b, p)),
                  pl.BlockSpec((t, LANES), lambda b, p: (b, nhp + p)),
                  pl.BlockSpec((t, LANES), lambda b, p: (b, 2 * nhp + p))],
        out_specs=pl.BlockSpec((t, LANES), lambda b, p: (b, p)),
        out_shape=jax.ShapeDtypeStruct((ROWS_CTX, C_W), BF16),
        compiler_params=_cparams(("arbitrary", "arbitrary")),
        name="attn_c_ctx",
    )(z, z, z)


NA_GRID_ROWS = T_LAT // GRID_W
NA_WIN = NA_ROWS * GRID_W
NA_DR = 2 * NA_ROWS - 1
NA_DC = 2 * NA_COLS - 1


def _na_row_start(r):
    return min(max(r - NA_ROWS // 2, 0), NA_GRID_ROWS - NA_ROWS)


def _na_bias_table(rpb):
    col = np.arange(GRID_W)
    cs = np.clip(col - NA_COLS // 2, 0, GRID_W - NA_COLS)
    col_in = (col[None, :] >= cs[:, None]) & (col[None, :] < cs[:, None] + NA_COLS)
    period = GRID_W + 1
    seq = jnp.concatenate([rpb, jnp.zeros((C_HEADS, NA_DR, period - NA_DC), F32)], axis=-1)
    seq = jnp.roll(seq, -(NA_COLS - 1), axis=-1)
    tile = jnp.tile(seq, (1, 1, GRID_W))[..., :GRID_W * GRID_W].reshape(C_HEADS, NA_DR, GRID_W, GRID_W)
    tile = jnp.where(jnp.asarray(col_in)[None, None], tile, NEG_INF)
    return jnp.concatenate([tile[:, :-1], tile[:, 1:]], axis=-1)


def _attn_na_kernel(q_ref, k_ref, v_ref, ck_ref, cv_ref, bias_ref, o_ref):
    heads = []
    for h in range(LANES // HD):
        sl = slice(h * HD, (h + 1) * HD)
        q = (q_ref[:, sl] * ATT_SCALE).astype(BF16)
        k = k_ref[:, sl].astype(BF16)
        v = v_ref[:, sl].astype(BF16)
        ck = ck_ref[0][:, sl].astype(BF16)
        cv = cv_ref[0][:, sl].astype(BF16)
        rows = []
        for r in range(NA_GRID_ROWS):
            qr = q[r * GRID_W:(r + 1) * GRID_W]
            rs = _na_row_start(r)
            kw = k[rs * GRID_W:rs * GRID_W + NA_WIN]
            vw = v[rs * GRID_W:rs * GRID_W + NA_WIN]
            dr0 = rs - r + (NA_ROWS - 1)
            bias = jnp.concatenate([bias_ref[h, dr0 + 2 * i] for i in range(NA_ROWS // 2)], axis=-1)
            s_nb = _qk(qr, kw) + bias
            s_cx = _qk(qr, ck)
            rows.append(_softmax_pv([s_nb, s_cx], [vw, cv]))
        heads.append(jnp.concatenate(rows, axis=0))
    o_ref[...] = jnp.concatenate(heads, axis=-1).astype(BF16)


def attn_na(z, ck, cv, bias):
    nhp = C_W // LANES
    hpb = LANES // HD
    seq0 = ROWS_CTX // T_LAT
    return pl.pallas_call(
        _attn_na_kernel,
        grid=(nhp, N_LAT),
        in_specs=[pl.BlockSpec((T_LAT, LANES), lambda p, b: (seq0 + b, p)),
                  pl.BlockSpec((T_LAT, LANES), lambda p, b: (seq0 + b, nhp + p)),
                  pl.BlockSpec((T_LAT, LANES), lambda p, b: (seq0 + b, 2 * nhp + p)),
                  pl.BlockSpec((1, PAST, LANES), lambda p, b: (b, 0, p)),
                  pl.BlockSpec((1, PAST, LANES), lambda p, b: (b, 0, p)),
                  pl.BlockSpec((hpb, NA_DR - 1, GRID_W, LANES), lambda p, b: (p, 0, 0, 0))],
        out_specs=pl.BlockSpec((T_LAT, LANES), lambda p, b: (b, p)),
        out_shape=jax.ShapeDtypeStruct((ROWS_LAT, C_W), BF16),
        compiler_params=_cparams(("arbitrary", "arbitrary")),
        name="attn_na",
    )(z, z, z, ck, cv, bias)


def _seg_sum(x, n_heads):
    parts = []
    for h in range(n_heads):
        s = jnp.sum(x[:, h * HD:(h + 1) * HD], axis=-1, keepdims=True)
        parts.append(jnp.broadcast_to(s, (x.shape[0], HD)))
    return jnp.concatenate(parts, axis=-1)


PREP_TM = 256


def _seq_len_at(row_start):
    return jnp.where(row_start < ROWS_CTX, T_CTX, T_LAT)


def _rwkv_prep_kernel(zb_ref, zprev_ref, znext_ref, mu_ref, kkw_ref, w0_ref, w2_ref, a0_ref, a2_ref, ka_ref,
                      rk_ref, g2_ref, r_ref, v_ref, kk_ref, w_ref, kd_ref, bb_ref, g_ref, bonus_ref):
    z = zb_ref[...]
    t = z.shape[0]
    start = pl.program_id(0) * t
    seq_len = _seq_len_at(start)
    pos = (start - jnp.where(start < ROWS_CTX, 0, ROWS_CTX)) % seq_len
    halo_prev = jnp.where(pos == 0, 0.0, zprev_ref[SUBLANES - 1:SUBLANES, :])
    halo_next = jnp.where(pos + t == seq_len, 0.0, znext_ref[0:1, :])
    row = lax.broadcasted_iota(jnp.int32, (t, 1), 0)
    prev = jnp.where(row == 0, halo_prev, pltpu.roll(z, 1, 0))
    nxt = jnp.where(row == t - 1, halo_next, pltpu.roll(z, t - 1, 0))
    m = z + mu_ref[...] * (0.5 * (prev + nxt) - z)
    r = m[:, :B_W]
    k = m[:, B_W:2 * B_W]
    v = m[:, 2 * B_W:3 * B_W]
    o = 3 * B_W
    wd = m[:, o:o + 2 * LORA_W]
    ad = m[:, o + 2 * LORA_W:o + 2 * LORA_W + 2 * LORA_A]
    gd = m[:, o + 2 * LORA_W + 2 * LORA_A:]

    kkr = k * kkw_ref[...]
    kk = kkr * lax.rsqrt(_seg_sum(kkr * kkr, B_HEADS) + 1e-12)

    wl = w0_ref[...] + jnp.dot(jnp.tanh(wd), w2_ref[...], precision=HIGHEST, preferred_element_type=F32)
    decay = jnp.exp(-float(np.exp(-0.5)) * jax.nn.sigmoid(wl))
    a = jax.nn.sigmoid(a0_ref[...] + jnp.dot(ad, a2_ref[...], precision=HIGHEST, preferred_element_type=F32))
    k2 = jnp.concatenate([k, k], axis=-1)
    ka2 = jnp.concatenate([ka_ref[...], ka_ref[...]], axis=-1)
    kk2 = jnp.concatenate([kk, kk], axis=-1)

    r_ref[...] = r
    v_ref[...] = v
    kk_ref[...] = kk
    w_ref[...] = decay
    kd_ref[...] = k2 * (1.0 + (a - 1.0) * ka2)
    bb_ref[...] = kk2 * a
    g_ref[...] = jnp.dot(jax.nn.sigmoid(gd).astype(BF16), g2_ref[...].astype(BF16), preferred_element_type=F32)
    bonus_ref[...] = _seg_sum(r * k * rk_ref[...], B_HEADS) * v


def rwkv_prep(z_b, params):
    tm = PREP_TM
    full = lambda a: pl.BlockSpec(a.shape, lambda s: (0,) * a.ndim)
    widths = (B_W, B_W, B_W, 2 * B_W, 2 * B_W, 2 * B_W, B_W, B_W)
    per = tm // SUBLANES
    last = ROWS // SUBLANES - 1
    return pl.pallas_call(
        _rwkv_prep_kernel,
        grid=(ROWS // tm,),
        in_specs=[pl.BlockSpec((tm, B_IN), lambda s: (s, 0)),
                  pl.BlockSpec((SUBLANES, B_IN), lambda s: (jnp.maximum(s * per - 1, 0), 0)),
                  pl.BlockSpec((SUBLANES, B_IN), lambda s: (jnp.minimum((s + 1) * per, last), 0))]
                 + [full(p) for p in params],
        out_specs=[pl.BlockSpec((tm, w), lambda s: (s, 0)) for w in widths],
        out_shape=[jax.ShapeDtypeStruct((ROWS, w), F32) for w in widths],
        compiler_params=_cparams(("arbitrary",)),
        name="rwkv_prep",
    )(z_b, z_b, z_b, *params)


SCAN_CHAINS = 8


def _tree_sum(xs):
    while len(xs) > 1:
        xs = [xs[i] + xs[i + 1] for i in range(0, len(xs) - 1, 2)] + ([xs[-1]] if len(xs) % 2 else [])
    return xs[0]


def _rwkv_scan_kernel(r_ref, kk_ref, v_ref, w_ref, k_ref, b_ref, s0_ref, y_ref, st_ref, s_ref, *, ni, tc):
    d = pl.program_id(0)
    c = pl.program_id(1)
    n_acc = max(1, SCAN_CHAINS // ni)

    @pl.when(c == 0)
    def _():
        s_ref[...] = s0_ref[...]

    def bcast(ref, t, j):
        return jnp.broadcast_to(ref[t, pl.ds(j, 1), :], (SUBLANES, LANES))

    def step(i, carry):
        t = i + d * (tc - 1 - 2 * i)
        sa = [[None] * n_acc for _ in range(ni)]
        for j in range(HD):
            kkj = bcast(kk_ref, t, j)
            for g in range(ni):
                p = s_ref[j, pl.ds(g * SUBLANES, SUBLANES), :] * kkj
                a = j % n_acc
                sa[g][a] = p if sa[g][a] is None else sa[g][a] + p
        sa = [-_tree_sum(x) for x in sa]
        vv = [v_ref[t, pl.ds(g * SUBLANES, SUBLANES), :] for g in range(ni)]
        ya = [[None] * n_acc for _ in range(ni)]
        for j in range(HD):
            wj = bcast(w_ref, t, j)
            bj = bcast(b_ref, t, j)
            kj = bcast(k_ref, t, j)
            rj = bcast(r_ref, t, j)
            for g in range(ni):
                sl = pl.ds(g * SUBLANES, SUBLANES)
                s_new = s_ref[j, sl, :] * wj + sa[g] * bj + vv[g] * kj
                s_ref[j, sl, :] = s_new
                p = s_new * rj
                a = j % n_acc
                ya[g][a] = p if ya[g][a] is None else ya[g][a] + p
        for g in range(ni):
            y_ref[t, pl.ds(g * SUBLANES, SUBLANES), :] = _tree_sum(ya[g])
        return carry

    lax.fori_loop(0, tc, step, 0)

    @pl.when(c == pl.num_programs(1) - 1)
    def _():
        st_ref[...] = s_ref[...]


def rwkv_scan(r, kk, v, w, k, bb, s0, tc):
    t = r.shape[0]
    ni8 = v.shape[1]
    ni = ni8 // SUBLANES
    nc = t // tc

    def chunk(d, c):
        return c + d * (nc - 1 - 2 * c)

    vec = pl.BlockSpec((tc, HD, LANES), lambda d, c: (chunk(d, c), 0, 0))
    val = pl.BlockSpec((tc, ni8, LANES), lambda d, c: (chunk(d, c), 0, 0))
    dvec = pl.BlockSpec((None, tc, HD, LANES), lambda d, c: (d, chunk(d, c), 0, 0))
    dval = pl.BlockSpec((None, tc, ni8, LANES), lambda d, c: (d, chunk(d, c), 0, 0))
    st = pl.BlockSpec((None, HD, ni8, LANES), lambda d, c: (d, 0, 0, 0))
    return pl.pallas_call(
        functools.partial(_rwkv_scan_kernel, ni=ni, tc=tc),
        grid=(2, nc),
        in_specs=[vec, vec, val, dvec, dvec, dvec, st],
        out_specs=[dval, st],
        out_shape=[jax.ShapeDtypeStruct((2, t, ni8, LANES), F32),
                   jax.ShapeDtypeStruct((2, HD, ni8, LANES), F32)],
        scratch_shapes=[pltpu.VMEM((HD, ni8, LANES), F32)],
        compiler_params=_cparams(("arbitrary", "arbitrary")),
        name="rwkv_scan",
    )(r, kk, v, w, k, bb, s0)


def _rwkv_post_kernel(y_ref, bonus_ref, g_ref, lng_ref, lnb_ref, o_ref):
    y = y_ref[0] + y_ref[1]
    mu = _seg_sum(y, B_HEADS) * (1.0 / HD)
    yc = y - mu
    var = _seg_sum(yc * yc, B_HEADS) * (1.0 / HD)
    yn = yc * lax.rsqrt(var + GN_EPS) * lng_ref[...] + lnb_ref[...]
    o_ref[...] = ((yn + bonus_ref[...]) * g_ref[...]).astype(BF16)


def rwkv_post(y, bonus, g, lng, lnb):
    tm = 512
    rows = y.shape[1]
    rowspec = pl.BlockSpec((tm, B_W), lambda i: (i, 0))
    vecspec = pl.BlockSpec((1, B_W), lambda i: (0, 0))
    return pl.pallas_call(
        _rwkv_post_kernel,
        grid=(rows // tm,),
        in_specs=[pl.BlockSpec((2, tm, B_W), lambda i: (0, i, 0)), rowspec, rowspec, vecspec, vecspec],
        out_specs=rowspec,
        out_shape=jax.ShapeDtypeStruct((rows, B_W), BF16),
        compiler_params=_cparams(("arbitrary",)),
        name="rwkv_post",
    )(y, bonus, g, lng, lnb)


def _scan_vec_layout(x, n_seq, t, n_dir, rep):
    x = x.reshape(n_seq, t, n_dir, B_HEADS, HD).transpose(2, 1, 4, 0, 3)
    x = jnp.broadcast_to(x[:, :, :, None], (n_dir, t, HD, rep, n_seq, B_HEADS))
    return x.reshape(n_dir, t, HD, LANES)


def _scan_val_layout(x, n_seq, t, rep):
    x = x.reshape(n_seq, t, B_HEADS, rep, HD // rep).transpose(1, 4, 3, 0, 2)
    return x.reshape(t, HD // rep, LANES)


def _scan_val_unlayout(y, n_seq, t, rep):
    y = y.reshape(2, t, HD // rep, rep, n_seq, B_HEADS).transpose(0, 4, 1, 5, 3, 2)
    return y.reshape(2, n_seq * t, B_W)


def rwkv_prep_all(z_b, prm, e):
    w2 = prm['b_w2'][e]
    a2 = prm['b_a2'][e]
    zero = jnp.zeros((LORA_W, B_W), F32)
    w2bd = jnp.concatenate([jnp.concatenate([w2[0], zero], 1), jnp.concatenate([zero, w2[1]], 1)], 0)
    a2bd = jnp.concatenate([jnp.concatenate([a2[0], zero], 1), jnp.concatenate([zero, a2[1]], 1)], 0)
    params = (prm['b_mu'][e][None], prm['b_kk'][e][None], prm['b_w0'][e].reshape(1, 2 * B_W), w2bd,
              prm['b_a0'][e].reshape(1, 2 * B_W), a2bd, prm['b_ka'][e][None],
              prm['b_rk'][e].reshape(1, B_W), prm['b_g2'][e])
    return rwkv_prep(z_b, params)


def rwkv_scan_group(prep, row0, n_seq, t, s0f, s0b, tc):
    r, v, kk, w, kd, bb = [a[row0:row0 + n_seq * t] for a in prep[:6]]
    rep = LANES // (n_seq * B_HEADS)
    vec = functools.partial(_scan_vec_layout, n_seq=n_seq, t=t, rep=rep)
    s0 = jnp.stack([s0f, s0b]).reshape(2, n_seq, B_HEADS, rep, HD // rep, HD)
    s0 = s0.transpose(0, 5, 4, 3, 1, 2).reshape(2, HD, HD // rep, LANES)

    y, s_fin = rwkv_scan(vec(r, n_dir=1)[0], vec(kk, n_dir=1)[0], _scan_val_layout(v, n_seq, t, rep),
                         vec(w, n_dir=2), vec(kd, n_dir=2), vec(bb, n_dir=2), s0, tc)

    s_fin = s_fin.reshape(2, HD, HD // rep, rep, n_seq, B_HEADS).transpose(0, 4, 5, 3, 2, 1)
    s_fin = s_fin.reshape(2, n_seq, B_HEADS, HD, HD)
    return _scan_val_unlayout(y, n_seq, t, rep), s_fin[0], s_fin[1]


def kernel(x_prompt, x_sample, cache_a_k, cache_a_v, state_b_fwd, state_b_bwd, cache_c_k, cache_c_v, c, c_ctx,
           ada_w, ada_b, norm1_g, norm2_g, final_norm_g, w_in_e, w_out_e, a_q_gain, a_k_gain, b_mu, b_w0, b_w2,
           b_a0, b_a2, b_g2, b_kk, b_ka, b_rk, b_ln_g, b_ln_b, ffn_w_gu, ffn_w_dn, w_in_o, w_out_o, c_rpb,
           router_w, router_b, moe_w_gu, moe_w_dn):
    prm = dict(b_mu=b_mu, b_w0=b_w0, b_w2=b_w2, b_a0=b_a0, b_a2=b_a2, b_g2=b_g2, b_kk=b_kk, b_ka=b_ka,
               b_rk=b_rk, b_ln_g=b_ln_g, b_ln_b=b_ln_b)
    x = jnp.concatenate([x_prompt.reshape(ROWS_CTX, D), x_sample.reshape(ROWS_LAT, D)], axis=0)
    cond = jnp.zeros((N_MODS, D), F32).at[0].set(c_ctx).at[1:1 + N_LAT].set(c)
    mods_all = ada_all(cond, ada_w, ada_b)
    mods = [mods_all[l].reshape(N_MODS, 6, D) for l in range(DEPTH)]

    h = modnorm(x, norm1_g[0][None], mods[0], 0, 1)

    new_a_k, new_a_v, new_sf, new_sb, new_c_k, new_c_v = [], [], [], [], [], []
    y_final = None
    for l in range(DEPTH):
        if l % 2 == 0:
            e = l // 2
            z_a = mm_in(h, w_in_e[e][:, :A_IN], tn=A_IN)
            z_b = mm_in(h, w_in_e[e][:, A_IN:], tn=B_IN // 3)
            qg, kg = a_q_gain[e][None], a_k_gain[e][None]
            ya_ctx, k_new = attn_a_ctx(z_a, qg, kg)
            ya_lat = attn_a_lat(z_a, cache_a_k[:, e].reshape(N_LAT, PAST, A_KV_W),
                                cache_a_v[:, e].reshape(N_LAT, PAST, A_KV_W), qg, kg)
            new_a_k.append(k_new.reshape(N_CTX, T_CTX, A_KV_HEADS, HD))
            new_a_v.append(z_a[:ROWS_CTX, A_W + A_KV_W:].reshape(N_CTX, T_CTX, A_KV_HEADS, HD))
            zeros = jnp.zeros((N_CTX, B_HEADS, HD, HD), F32)
            prep = rwkv_prep_all(z_b, prm, e)
            y_c, sf, sb = rwkv_scan_group(prep, 0, N_CTX, T_CTX, zeros, zeros, tc=32)
            y_l, _, _ = rwkv_scan_group(prep, ROWS_CTX, N_LAT, T_LAT, state_b_fwd[:, e],
                                        state_b_bwd[:, e], tc=64)
            new_sf.append(sf)
            new_sb.append(sb)
            y_b = rwkv_post(jnp.concatenate([y_c, y_l], axis=1), prep[7], prep[6],
                            b_ln_g[e][None], b_ln_b[e][None])
            y = jnp.concatenate([jnp.concatenate([ya_ctx, ya_lat], axis=0), y_b], axis=1)
            x, h = mm_out(y, w_out_e[e], x, mods[l], mods[l], norm2_g[l][None], 2, 3, 4)
            x, h = ffn(h, ffn_w_gu[e], ffn_w_dn[e], x, mods[l], mods[l + 1], norm1_g[l + 1][None], 5, 0, 1)
        else:
            o = l // 2
            z = mm_in(h, w_in_o[o], tn=C_W)
            y_ctx = attn_c_ctx(z)
            bias = _na_bias_table(c_rpb[o])
            y_lat = attn_na(z, cache_c_k[:, o].reshape(N_LAT, PAST, C_W),
                            cache_c_v[:, o].reshape(N_LAT, PAST, C_W), bias)
            new_c_k.append(z[:ROWS_CTX, C_W:2 * C_W].reshape(N_CTX, T_CTX, C_HEADS, HD))
            new_c_v.append(z[:ROWS_CTX, 2 * C_W:].reshape(N_CTX, T_CTX, C_HEADS, HD))
            y = jnp.concatenate([y_ctx, y_lat], axis=0)
            x, h = mm_out(y, w_out_o[o], x, mods[l], mods[l], norm2_g[l][None], 2, 3, 4)
            gates = router(x, norm2_g[l][None], mods[l], router_w[o], router_b[o], 3, 4)
            y_moe = moe_dense(h, gates, moe_w_gu[o], moe_w_dn[o])
            if l + 1 < DEPTH:
                x, h = resid(x, y_moe, mods[l], mods[l + 1], norm1_g[l + 1][None], 5, 0, 1, False)
            else:
                (y_final,) = resid(x, y_moe, mods[l], mods[l], final_norm_g[None], 5, 0, 1, True)

    y_prompt = y_final[:ROWS_CTX].reshape(N_CTX, T_CTX, D)
    y_sample = y_final[ROWS_CTX:].reshape(N_LAT, T_LAT, D)
    return (y_prompt, y_sample, jnp.stack(new_a_k, axis=1), jnp.stack(new_a_v, axis=1),
            jnp.stack(new_sf, axis=1), jnp.stack(new_sb, axis=1),
            jnp.stack(new_c_k, axis=1), jnp.stack(new_c_v, axis=1))
```

```python
import functools

import numpy as np
import jax
import jax.numpy as jnp
from jax import lax
from jax.experimental import pallas as pl
from jax.experimental.pallas import tpu as pltpu

F32 = jnp.float32
BF16 = jnp.bfloat16
HIGHEST = lax.Precision.HIGHEST

D = 1024
N_CTX, T_CTX = 16, 256
N_LAT, T_LAT = 2, 1024
ROWS_CTX = N_CTX * T_CTX
ROWS_LAT = N_LAT * T_LAT
ROWS = ROWS_CTX + ROWS_LAT
DEPTH = 4
GRID_W = 64
HD = 64
A_HEADS, A_KV_HEADS, B_HEADS, C_HEADS = 8, 2, 8, 16
A_W, A_KV_W, B_W, C_W = A_HEADS * HD, A_KV_HEADS * HD, B_HEADS * HD, C_HEADS * HD
A_IN = A_W + 2 * A_KV_W
LORA_W, LORA_A, LORA_G = 64, 64, 128
B_IN = 3 * B_W + 2 * LORA_W + 2 * LORA_A + LORA_G
EVEN_IN = A_IN + B_IN
PAST = 512
NA_ROWS, NA_COLS = 8, 16
D_FF = 2816
N_EXPERTS = 8
D_EXPERT = 1408
ROPE_THETA = 10000.0
EPS = 1e-6
GN_EPS = 64e-5
NEG_INF = -1e30
ATT_SCALE = HD ** -0.5

LANES = 128
SUBLANES = 8
VMEM_LIMIT = 56 * 1024 * 1024

N_MODS = 8

ROW_SUB = T_LAT // T_CTX
ROW_GROUPS = ROWS // (ROW_SUB * T_CTX)
CTX_GROUPS = N_CTX // ROW_SUB


def _cparams(sem):
    return pltpu.CompilerParams(dimension_semantics=sem, vmem_limit_bytes=VMEM_LIMIT)


def _untouched():
    return pl.BlockSpec(memory_space=pl.ANY)


def _mod_index(row_start):
    return jnp.where(row_start < ROWS_CTX, 0, 1 + (row_start - ROWS_CTX) // T_LAT)


def _modspec(layer, tm, nargs=1):
    if nargs == 1:
        return pl.BlockSpec((None, 1, 6, D), lambda i: (layer, _mod_index(i * tm), 0, 0))
    return pl.BlockSpec((None, 1, 6, D), lambda i, j: (layer, _mod_index(i * tm), 0, 0))


def _gainspec(layer, nargs=1):
    if nargs == 1:
        return pl.BlockSpec((None, 1, D), lambda i: (layer, 0, 0))
    return pl.BlockSpec((None, 1, D), lambda i, j: (layer, 0, 0))


def _modnorm(x, g, shift, scale):
    ms = jnp.mean(x * x, axis=-1, keepdims=True)
    return (x * lax.rsqrt(ms + EPS) * g) * (1.0 + scale) + shift


def _silu(x):
    return x * jax.nn.sigmoid(x)


def _ada_kernel(c_ref, w_ref, b_ref, o_ref):
    s = _silu(c_ref[...]).astype(BF16)
    o_ref[0] = jnp.dot(s, w_ref[0].astype(BF16), preferred_element_type=F32) + b_ref[0]


def ada_all(cond, ada_w, ada_b):
    tn = 1536
    n = 6 * D
    return pl.pallas_call(
        _ada_kernel,
        grid=(DEPTH, n // tn),
        in_specs=[pl.BlockSpec((N_MODS, D), lambda l, j: (0, 0)),
                  pl.BlockSpec((1, D, tn), lambda l, j: (l, 0, j)),
                  pl.BlockSpec((1, 1, tn), lambda l, j: (l, 0, j))],
        out_specs=pl.BlockSpec((1, N_MODS, tn), lambda l, j: (l, 0, j)),
        out_shape=jax.ShapeDtypeStruct((DEPTH, N_MODS, n), F32),
        compiler_params=_cparams(("arbitrary", "arbitrary")),
        name="ada",
    )(cond, ada_w, ada_b.reshape(DEPTH, 1, n))


def _first_norm_kernel(xp_ref, xs_ref, g_ref, mod_ref, x_ref, h_ref, *, n_ctx_blocks):
    i = pl.program_id(0)

    def emit(src):
        x = src[...]
        x_ref[...] = x
        h_ref[...] = _modnorm(x, g_ref[...], mod_ref[0, 0:1, :], mod_ref[0, 1:2, :]).astype(BF16)

    @pl.when(i < n_ctx_blocks)
    def _():
        emit(xp_ref)

    @pl.when(i >= n_ctx_blocks)
    def _():
        emit(xs_ref)


def first_norm(x_prompt, x_sample, gains, mods):
    tm = 512
    nc = ROWS_CTX // tm
    rowspec = pl.BlockSpec((tm, D), lambda i: (i, 0))
    return pl.pallas_call(
        functools.partial(_first_norm_kernel, n_ctx_blocks=nc),
        grid=(ROWS // tm,),
        in_specs=[pl.BlockSpec((tm, D), lambda i: (jnp.minimum(i, nc - 1), 0)),
                  pl.BlockSpec((tm, D), lambda i: (jnp.maximum(i - nc, 0), 0)),
                  _gainspec(0), _modspec(0, tm)],
        out_specs=[rowspec, rowspec],
        out_shape=[jax.ShapeDtypeStruct((ROWS, D), F32), jax.ShapeDtypeStruct((ROWS, D), BF16)],
        compiler_params=_cparams(("arbitrary",)),
        name="first_norm",
    )(x_prompt.reshape(ROWS_CTX, D), x_sample.reshape(ROWS_LAT, D), gains, mods)


def _mm_in_kernel(h_ref, w_ref, o_ref, wb_ref):
    @pl.when(pl.program_id(1) == 0)
    def _():
        wb_ref[...] = w_ref[...].astype(BF16)

    o_ref[...] = jnp.dot(h_ref[...], wb_ref[...], preferred_element_type=F32)


def mm_in(h, w, layer, tn, tm=512):
    _, k, n = w.shape
    return pl.pallas_call(
        _mm_in_kernel,
        grid=(n // tn, ROWS // tm),
        in_specs=[pl.BlockSpec((tm, k), lambda j, i: (i, 0)),
                  pl.BlockSpec((None, k, tn), lambda j, i: (layer, 0, j))],
        out_specs=pl.BlockSpec((tm, tn), lambda j, i: (i, j)),
        out_shape=jax.ShapeDtypeStruct((ROWS, n), F32),
        scratch_shapes=[pltpu.VMEM((k, tn), BF16)],
        compiler_params=_cparams(("arbitrary", "arbitrary")),
        name="mm_in",
    )(h, w)


def _residual_epilogue(x, acc, mod_ref, nmod_ref, g_ref, xo_ref, ho_ref, gate_idx, shift_idx, scale_idx, final):
    gate = mod_ref[0, gate_idx:gate_idx + 1, :]
    xn = x + gate * acc
    if final:
        ms = jnp.mean(xn * xn, axis=-1, keepdims=True)
        ho_ref[...] = xn * lax.rsqrt(ms + EPS) * g_ref[...]
    else:
        xo_ref[...] = xn
        shift = nmod_ref[0, shift_idx:shift_idx + 1, :]
        scale = nmod_ref[0, scale_idx:scale_idx + 1, :]
        ho_ref[...] = _modnorm(xn, g_ref[...], shift, scale).astype(BF16)


def _mm_out_kernel(*refs, n_parts, gate_idx, shift_idx, scale_idx):
    y_refs = refs[:n_parts]
    w_refs = refs[n_parts:2 * n_parts]
    x_ref, mod_ref, nmod_ref, g_ref, xo_ref, ho_ref = refs[2 * n_parts:2 * n_parts + 6]
    wb_refs = refs[2 * n_parts + 6:]

    @pl.when(pl.program_id(0) == 0)
    def _():
        for w_ref, wb_ref in zip(w_refs, wb_refs):
            wb_ref[...] = w_ref[...].astype(BF16)

    acc = None
    for y_ref, wb_ref in zip(y_refs, wb_refs):
        p = jnp.dot(y_ref[...], wb_ref[...], preferred_element_type=F32)
        acc = p if acc is None else acc + p
    _residual_epilogue(x_ref[...], acc, mod_ref, nmod_ref, g_ref, xo_ref, ho_ref,
                       gate_idx, shift_idx, scale_idx, False)


def mm_out(parts, w, layer_w, x, mods, layer, gains, gain_layer):
    tm = 512
    kp = parts[0].shape[1]
    n_parts = len(parts)
    rowspec = pl.BlockSpec((tm, D), lambda i: (i, 0))
    return pl.pallas_call(
        functools.partial(_mm_out_kernel, n_parts=n_parts, gate_idx=2, shift_idx=3, scale_idx=4),
        grid=(ROWS // tm,),
        in_specs=[pl.BlockSpec((tm, kp), lambda i: (i, 0)) for _ in parts]
                 + [pl.BlockSpec((None, kp, D), lambda i, p=p: (layer_w, p, 0)) for p in range(n_parts)]
                 + [rowspec, _modspec(layer, tm), _modspec(layer, tm), _gainspec(gain_layer)],
        out_specs=[rowspec, rowspec],
        out_shape=[jax.ShapeDtypeStruct((ROWS, D), F32), jax.ShapeDtypeStruct((ROWS, D), BF16)],
        scratch_shapes=[pltpu.VMEM((kp, D), BF16) for _ in parts],
        compiler_params=_cparams(("arbitrary",)),
        name="mm_out",
    )(*parts, *([w] * n_parts), x, mods, mods, gains)


FFN_TF = 256
FFN_NF = D_FF // FFN_TF


def _ffn_kernel(h_ref, wg_ref, wu_ref, wd_ref, x_ref, mod_ref, nmod_ref, g_ref, xo_ref, ho_ref,
                wg_s, wu_s, wd_s, acc_ref):
    i = pl.program_id(0)
    f = pl.program_id(1)

    @pl.when(i == 0)
    def _():
        wg_s[f] = wg_ref[...].astype(BF16)
        wu_s[f] = wu_ref[...].astype(BF16)
        wd_s[f] = wd_ref[...].astype(BF16)

    @pl.when(f == 0)
    def _():
        acc_ref[...] = jnp.zeros_like(acc_ref)

    h = h_ref[...]
    gp = jnp.dot(h, wg_s[f], preferred_element_type=F32)
    up = jnp.dot(h, wu_s[f], preferred_element_type=F32)
    a = (_silu(gp) * up).astype(BF16)
    acc_ref[...] += jnp.dot(a, wd_s[f], preferred_element_type=F32)

    @pl.when(f == FFN_NF - 1)
    def _():
        _residual_epilogue(x_ref[...], acc_ref[...], mod_ref, nmod_ref, g_ref, xo_ref, ho_ref, 5, 0, 1, False)


def ffn(h, w_gu, w_dn, e, x, mods, layer, gains):
    tm = 512
    tf, nf = FFN_TF, FFN_NF

    def once(i, f):
        return jnp.where(i == 0, f, nf - 1)

    rowspec = pl.BlockSpec((tm, D), lambda i, f: (i, 0))
    return pl.pallas_call(
        _ffn_kernel,
        grid=(ROWS // tm, nf),
        in_specs=[rowspec,
                  pl.BlockSpec((None, D, tf), lambda i, f: (e, 0, once(i, f))),
                  pl.BlockSpec((None, D, tf), lambda i, f: (e, 0, nf + once(i, f))),
                  pl.BlockSpec((None, tf, D), lambda i, f: (e, once(i, f), 0)),
                  rowspec, _modspec(layer, tm, 2), _modspec(layer + 1, tm, 2), _gainspec(layer + 1, 2)],
        out_specs=[rowspec, rowspec],
        out_shape=[jax.ShapeDtypeStruct((ROWS, D), F32), jax.ShapeDtypeStruct((ROWS, D), BF16)],
        scratch_shapes=[pltpu.VMEM((nf, D, tf), BF16),
                        pltpu.VMEM((nf, D, tf), BF16),
                        pltpu.VMEM((nf, tf, D), BF16),
                        pltpu.VMEM((tm, D), F32)],
        compiler_params=_cparams(("arbitrary", "arbitrary")),
        name="ffn",
    )(h, w_gu, w_gu, w_dn, x, mods, mods, gains)


def _router_kernel(x_ref, g_ref, mod_ref, rw_ref, rb_ref, gates_ref):
    h = _modnorm(x_ref[...], g_ref[...], mod_ref[0, 3:4, :], mod_ref[0, 4:5, :])
    logits = jnp.dot(h, rw_ref[...], precision=HIGHEST, preferred_element_type=F32)
    lane = lax.broadcasted_iota(jnp.int32, logits.shape, 1)
    sel = jnp.where(lane < N_EXPERTS, logits + rb_ref[...], -jnp.inf)
    m1 = jnp.max(sel, axis=-1, keepdims=True)
    i1 = jnp.min(jnp.where(sel == m1, lane, LANES), axis=-1, keepdims=True)
    sel2 = jnp.where(lane == i1, -jnp.inf, sel)
    m2 = jnp.max(sel2, axis=-1, keepdims=True)
    i2 = jnp.min(jnp.where(sel2 == m2, lane, LANES), axis=-1, keepdims=True)
    l1 = jnp.sum(jnp.where(lane == i1, logits, 0.0), axis=-1, keepdims=True)
    l2 = jnp.sum(jnp.where(lane == i2, logits, 0.0), axis=-1, keepdims=True)
    mx = jnp.maximum(l1, l2)
    e1 = jnp.exp(l1 - mx)
    e2 = jnp.exp(l2 - mx)
    den = e1 + e2
    gates_ref[...] = jnp.where(lane == i1, e1 / den, 0.0) + jnp.where(lane == i2, e2 / den, 0.0)


def router(x, gains, mods, layer, rw_p, rb_p, o):
    tm = 512
    return pl.pallas_call(
        _router_kernel,
        grid=(ROWS // tm,),
        in_specs=[pl.BlockSpec((tm, D), lambda i: (i, 0)),
                  _gainspec(layer), _modspec(layer, tm),
                  pl.BlockSpec((None, D, LANES), lambda i: (o, 0, 0)),
                  pl.BlockSpec((None, 1, LANES), lambda i: (o, 0, 0))],
        out_specs=pl.BlockSpec((tm, LANES), lambda i: (i, 0)),
        out_shape=jax.ShapeDtypeStruct((ROWS, LANES), F32),
        compiler_params=_cparams(("arbitrary",)),
        name="router",
    )(x, gains, mods, rw_p, rb_p)


def _moe_kernel(h_ref, gates_ref, wgu_ref, wdn_ref, yin_ref, yo_ref, wgu_s, wdn_s):
    e = pl.program_id(0)

    @pl.when(pl.program_id(1) == 0)
    def _():
        wgu_s[...] = wgu_ref[...].astype(BF16)
        wdn_s[...] = wdn_ref[...].astype(BF16)

    gu = jnp.dot(h_ref[...], wgu_s[...], preferred_element_type=F32)
    a = (_silu(gu[:, :D_EXPERT]) * gu[:, D_EXPERT:]).astype(BF16)
    y = jnp.dot(a, wdn_s[...], preferred_element_type=F32)
    gt = gates_ref[...]
    lane = lax.broadcasted_iota(jnp.int32, gt.shape, 1)
    gate = jnp.sum(jnp.where(lane == e, gt, 0.0), axis=-1, keepdims=True)
    yo_ref[...] = yin_ref[...] + gate * y


def moe_dense(h, gates, w_gu, w_dn, o):
    tm = 256
    y0 = jnp.zeros((ROWS, D), F32)
    return pl.pallas_call(
        _moe_kernel,
        grid=(N_EXPERTS, ROWS // tm),
        in_specs=[pl.BlockSpec((tm, D), lambda e, i: (i, 0)),
                  pl.BlockSpec((tm, LANES), lambda e, i: (i, 0)),
                  pl.BlockSpec((None, None, D, 2 * D_EXPERT), lambda e, i: (o, e, 0, 0),
                               pipeline_mode=pl.Buffered(1)),
                  pl.BlockSpec((None, None, D_EXPERT, D), lambda e, i: (o, e, 0, 0),
                               pipeline_mode=pl.Buffered(1)),
                  pl.BlockSpec((tm, D), lambda e, i: (i, 0))],
        out_specs=pl.BlockSpec((tm, D), lambda e, i: (i, 0)),
        out_shape=jax.ShapeDtypeStruct((ROWS, D), F32),
        scratch_shapes=[pltpu.VMEM((D, 2 * D_EXPERT), BF16),
                        pltpu.VMEM((D_EXPERT, D), BF16)],
        input_output_aliases={4: 0},
        compiler_params=_cparams(("arbitrary", "arbitrary")),
        name="moe",
    )(h, gates, w_gu, w_dn, y0)


def _resid_kernel(x_ref, y_ref, mod_ref, nmod_ref, g_ref, *out_refs, final):
    if final:
        xo_ref, ho_ref = None, out_refs[0]
    else:
        xo_ref, ho_ref = out_refs
    _residual_epilogue(x_ref[...], y_ref[...], mod_ref, nmod_ref, g_ref, xo_ref, ho_ref, 5, 0, 1, final)


def resid(x, y, mods, layer, gains, final):
    tm = 512
    rowspec = pl.BlockSpec((tm, D), lambda i: (i, 0))
    if final:
        out_specs = [rowspec]
        out_shape = [jax.ShapeDtypeStruct((ROWS, D), F32)]
        nmod, gain = _modspec(layer, tm), pl.BlockSpec((1, D), lambda i: (0, 0))
    else:
        out_specs = [rowspec, rowspec]
        out_shape = [jax.ShapeDtypeStruct((ROWS, D), F32), jax.ShapeDtypeStruct((ROWS, D), BF16)]
        nmod, gain = _modspec(layer + 1, tm), _gainspec(layer + 1)
    return pl.pallas_call(
        functools.partial(_resid_kernel, final=final),
        grid=(ROWS // tm,),
        in_specs=[rowspec, rowspec, _modspec(layer, tm), nmod, gain],
        out_specs=out_specs,
        out_shape=out_shape,
        compiler_params=_cparams(("arbitrary",)),
        name="resid",
    )(x, y, mods, mods, gains)


def _softmax_pv(scores, values):
    m = None
    for s in scores:
        mi = jnp.max(s, axis=-1, keepdims=True)
        m = mi if m is None else jnp.maximum(m, mi)
    num, den = None, None
    for s, v in zip(scores, values):
        p = jnp.exp(s - m)
        li = jnp.sum(p, axis=-1, keepdims=True)
        oi = jnp.dot(p.astype(BF16), v, preferred_element_type=F32)
        num = oi if num is None else num + oi
        den = li if den is None else den + li
    return num / den


def _qk(q, k):
    return lax.dot_general(q, k, (((1,), (1,)), ((), ())), preferred_element_type=F32)


def _head_rmsnorm(x, gain, n_heads):
    parts = []
    for h in range(n_heads):
        xh = x[:, h * HD:(h + 1) * HD]
        ms = jnp.mean(xh * xh, axis=-1, keepdims=True)
        parts.append(xh * lax.rsqrt(ms + EPS) * gain)
    return parts


def _rope128(x, cos, sin_signed):
    lane = lax.broadcasted_iota(jnp.int32, x.shape, 1)
    up = pltpu.roll(x, LANES - 16, 1)
    dn = pltpu.roll(x, 16, 1)
    partner = jnp.where((lane % 32) < 16, up, dn)
    return x * cos + partner * sin_signed


def _rope_tables():
    t = np.arange(T_LAT)
    n_f = HD // 4
    inv = ROPE_THETA ** (-np.arange(n_f, dtype=np.float32) / n_f)
    cos = np.zeros((T_LAT, HD), np.float32)
    sin = np.zeros((T_LAT, HD), np.float32)
    for half, pos in ((0, t // GRID_W), (1, t % GRID_W)):
        ang = pos[:, None].astype(np.float32) * inv[None, :]
        c, s = np.cos(ang), np.sin(ang)
        base = half * (HD // 2)
        cos[:, base:base + n_f] = c
        cos[:, base + n_f:base + 2 * n_f] = c
        sin[:, base:base + n_f] = -s
        sin[:, base + n_f:base + 2 * n_f] = s
    return np.tile(cos, (1, 2)), np.tile(sin, (1, 2))


def _attn_a_ctx_kernel(q_ref, kv_ref, qg_ref, kg_ref, *rest):
    o_ref, ko_ref, vo_ref = rest[-3:]
    q = q_ref[...]
    kv = kv_ref[...]
    k = kv[:, :A_KV_W]
    v = kv[:, A_KV_W:]
    kn = _head_rmsnorm(k, kg_ref[...], A_KV_HEADS)
    ko_ref[...] = jnp.concatenate(kn, axis=-1)
    vo_ref[...] = v
    qn = _head_rmsnorm(q, qg_ref[...], A_HEADS)
    group = A_HEADS // A_KV_HEADS
    outs = []
    for h in range(A_HEADS):
        g = h // group
        kh = kn[g].astype(BF16)
        vh = v[:, g * HD:(g + 1) * HD].astype(BF16)
        qh = (qn[h] * ATT_SCALE).astype(BF16)
        outs.append(_softmax_pv([_qk(qh, kh)], [vh]))
    o_ref[...] = jnp.concatenate(outs, axis=-1).astype(BF16)


def attn_a_ctx(z, gq, gk, e, prev_caches):
    t = T_CTX
    n_even = gq.shape[0]
    cache_spec = pl.BlockSpec((None, None, t, A_KV_W), lambda b: (b, e, 0, 0))
    cache_shape = jax.ShapeDtypeStruct((N_CTX, n_even, t, A_KV_W), F32)
    extra = [] if prev_caches is None else list(prev_caches)
    aliases = {} if prev_caches is None else {4: 1, 5: 2}
    return pl.pallas_call(
        _attn_a_ctx_kernel,
        grid=(N_CTX,),
        in_specs=[pl.BlockSpec((t, A_W), lambda b: (b, 0)),
                  pl.BlockSpec((t, 2 * A_KV_W), lambda b: (b, A_W // (2 * A_KV_W))),
                  pl.BlockSpec((None, 1, HD), lambda b: (e, 0, 0)),
                  pl.BlockSpec((None, 1, HD), lambda b: (e, 0, 0))] + [_untouched() for _ in extra],
        out_specs=[pl.BlockSpec((t, A_W), lambda b: (b, 0)), cache_spec, cache_spec],
        out_shape=[jax.ShapeDtypeStruct((ROWS, A_W), BF16), cache_shape, cache_shape],
        input_output_aliases=aliases,
        compiler_params=_cparams(("arbitrary",)),
        name="attn_a_ctx",
    )(z, z, gq, gk, *extra)


A_LAT_TQ = 256


def _attn_a_lat_kernel(q_ref, kv_ref, ck_ref, cv_ref, qg_ref, kg_ref, cq_ref, sq_ref, ckk_ref, skk_ref,
                       prev_ref, o_ref, k_s, v_s):
    del prev_ref

    @pl.when(pl.program_id(1) == 0)
    def _():
        kv = kv_ref[...]
        kn = jnp.concatenate(_head_rmsnorm(kv[:, :A_KV_W], kg_ref[...], A_KV_HEADS), axis=-1)
        k_s[...] = _rope128(kn, ckk_ref[...], skk_ref[...]).astype(BF16)
        v_s[...] = kv[:, A_KV_W:].astype(BF16)

    qn = _head_rmsnorm(q_ref[...], qg_ref[...], A_HEADS)
    cos, sin = cq_ref[...], sq_ref[...]
    qr = []
    for c in range(A_HEADS // 2):
        slab = _rope128(jnp.concatenate(qn[2 * c:2 * c + 2], axis=-1), cos, sin)
        qr.append(slab[:, :HD])
        qr.append(slab[:, HD:])
    ck = ck_ref[...].astype(BF16)
    cv = cv_ref[...].astype(BF16)
    kk = k_s[...]
    vv = v_s[...]
    group = A_HEADS // A_KV_HEADS
    outs = []
    for h in range(A_HEADS):
        g = h // group
        sl = slice(g * HD, (g + 1) * HD)
        qh = (qr[h] * ATT_SCALE).astype(BF16)
        outs.append(_softmax_pv([_qk(qh, kk[:, sl]), _qk(qh, ck[:, sl])], [vv[:, sl], cv[:, sl]]))
    o_ref[...] = jnp.concatenate(outs, axis=-1).astype(BF16)


def attn_a_lat(z, ck, cv, gq, gk, e, y_prev):
    tq = A_LAT_TQ
    nq = T_LAT // tq
    cos, sin = _rope_tables()
    cos, sin = jnp.asarray(cos), jnp.asarray(sin)
    row0 = ROWS_CTX // tq
    seq0 = ROWS_CTX // T_LAT
    cache_spec = pl.BlockSpec((None, None, PAST, A_KV_W), lambda b, i: (b, e, 0, 0))
    return pl.pallas_call(
        _attn_a_lat_kernel,
        grid=(N_LAT, nq),
        in_specs=[pl.BlockSpec((tq, A_W), lambda b, i: (row0 + b * nq + i, 0)),
                  pl.BlockSpec((T_LAT, 2 * A_KV_W), lambda b, i: (seq0 + b, A_W // (2 * A_KV_W))),
                  cache_spec, cache_spec,
                  pl.BlockSpec((None, 1, HD), lambda b, i: (e, 0, 0)),
                  pl.BlockSpec((None, 1, HD), lambda b, i: (e, 0, 0)),
                  pl.BlockSpec((tq, LANES), lambda b, i: (i, 0)),
                  pl.BlockSpec((tq, LANES), lambda b, i: (i, 0)),
                  pl.BlockSpec((T_LAT, LANES), lambda b, i: (0, 0)),
                  pl.BlockSpec((T_LAT, LANES), lambda b, i: (0, 0)),
                  _untouched()],
        out_specs=pl.BlockSpec((tq, A_W), lambda b, i: (row0 + b * nq + i, 0)),
        out_shape=jax.ShapeDtypeStruct((ROWS, A_W), BF16),
        scratch_shapes=[pltpu.VMEM((T_LAT, A_KV_W), BF16),
                        pltpu.VMEM((T_LAT, A_KV_W), BF16)],
        input_output_aliases={10: 0},
        compiler_params=_cparams(("arbitrary", "arbitrary")),
        name="attn_a_lat",
    )(z, z, ck, cv, gq, gk, cos, sin, cos, sin, y_prev)


def _attn_c_ctx_kernel(q_ref, k_ref, v_ref, *rest):
    o_ref, ko_ref, vo_ref = rest[-3:]
    q = q_ref[...]
    k = k_ref[...]
    v = v_ref[...]
    ko_ref[...] = k
    vo_ref[...] = v
    outs = []
    for h in range(LANES // HD):
        sl = slice(h * HD, (h + 1) * HD)
        qh = (q[:, sl] * ATT_SCALE).astype(BF16)
        outs.append(_softmax_pv([_qk(qh, k[:, sl].astype(BF16))], [v[:, sl].astype(BF16)]))
    o_ref[...] = jnp.concatenate(outs, axis=-1).astype(BF16)


def attn_c_ctx(z, o, n_odd, prev_caches):
    t = T_CTX
    nhp = C_W // LANES
    cache_spec = pl.BlockSpec((None, None, t, LANES), lambda b, p: (b, o, 0, p))
    cache_shape = jax.ShapeDtypeStruct((N_CTX, n_odd, t, C_W), F32)
    extra = [] if prev_caches is None else list(prev_caches)
    aliases = {} if prev_caches is None else {3: 1, 4: 2}
    return pl.pallas_call(
        _attn_c_ctx_kernel,
        grid=(N_CTX, nhp),
        in_specs=[pl.BlockSpec((t, LANES), lambda b, p: (b, p)),
                  pl.BlockSpec((t, LANES), lambda b, p: (b, nhp + p)),
                  pl.BlockSpec((t, LANES), lambda b, p: (b, 2 * nhp + p))] + [_untouched() for _ in extra],
        out_specs=[pl.BlockSpec((t, LANES), lambda b, p: (b, p)), cache_spec, cache_spec],
        out_shape=[jax.ShapeDtypeStruct((ROWS, C_W), BF16), cache_shape, cache_shape],
        input_output_aliases=aliases,
        compiler_params=_cparams(("arbitrary", "arbitrary")),
        name="attn_c_ctx",
    )(z, z, z, *extra)


NA_GRID_ROWS = T_LAT // GRID_W
NA_WIN = NA_ROWS * GRID_W
NA_DR = 2 * NA_ROWS - 1
NA_DC = 2 * NA_COLS - 1


def _na_row_start(r):
    return min(max(r - NA_ROWS // 2, 0), NA_GRID_ROWS - NA_ROWS)


def _na_bias_table(rpb):
    n_l = rpb.shape[0]
    col = np.arange(GRID_W)
    cs = np.clip(col - NA_COLS // 2, 0, GRID_W - NA_COLS)
    col_in = (col[None, :] >= cs[:, None]) & (col[None, :] < cs[:, None] + NA_COLS)
    period = GRID_W + 1
    seq = jnp.concatenate([rpb, jnp.zeros((n_l, C_HEADS, NA_DR, period - NA_DC), F32)], axis=-1)
    seq = jnp.roll(seq, -(NA_COLS - 1), axis=-1)
    tile = jnp.tile(seq, (1, 1, 1, GRID_W))[..., :GRID_W * GRID_W].reshape(n_l, C_HEADS, NA_DR, GRID_W, GRID_W)
    tile = jnp.where(jnp.asarray(col_in), tile, NEG_INF)
    return jnp.concatenate([tile[:, :, :-1], tile[:, :, 1:]], axis=-1)


def _attn_na_kernel(q_ref, k_ref, v_ref, ck_ref, cv_ref, bias_ref, prev_ref, o_ref):
    del prev_ref
    heads = []
    for h in range(LANES // HD):
        sl = slice(h * HD, (h + 1) * HD)
        q = (q_ref[:, sl] * ATT_SCALE).astype(BF16)
        k = k_ref[:, sl].astype(BF16)
        v = v_ref[:, sl].astype(BF16)
        ck = ck_ref[:, sl].astype(BF16)
        cv = cv_ref[:, sl].astype(BF16)
        rows = []
        for r in range(NA_GRID_ROWS):
            qr = q[r * GRID_W:(r + 1) * GRID_W]
            rs = _na_row_start(r)
            kw = k[rs * GRID_W:rs * GRID_W + NA_WIN]
            vw = v[rs * GRID_W:rs * GRID_W + NA_WIN]
            dr0 = rs - r + (NA_ROWS - 1)
            bias = jnp.concatenate([bias_ref[h, dr0 + 2 * i] for i in range(NA_ROWS // 2)], axis=-1)
            s_nb = _qk(qr, kw) + bias
            s_cx = _qk(qr, ck)
            rows.append(_softmax_pv([s_nb, s_cx], [vw, cv]))
        heads.append(jnp.concatenate(rows, axis=0))
    o_ref[...] = jnp.concatenate(heads, axis=-1).astype(BF16)


def attn_na(z, ck, cv, bias, o, y_prev):
    nhp = C_W // LANES
    hpb = LANES // HD
    seq0 = ROWS_CTX // T_LAT
    cache_spec = pl.BlockSpec((None, None, PAST, LANES), lambda p, b: (b, o, 0, p))
    return pl.pallas_call(
        _attn_na_kernel,
        grid=(nhp, N_LAT),
        in_specs=[pl.BlockSpec((T_LAT, LANES), lambda p, b: (seq0 + b, p)),
                  pl.BlockSpec((T_LAT, LANES), lambda p, b: (seq0 + b, nhp + p)),
                  pl.BlockSpec((T_LAT, LANES), lambda p, b: (seq0 + b, 2 * nhp + p)),
                  cache_spec, cache_spec,
                  pl.BlockSpec((None, hpb, NA_DR - 1, GRID_W, LANES), lambda p, b: (o, p, 0, 0, 0)),
                  _untouched()],
        out_specs=pl.BlockSpec((T_LAT, LANES), lambda p, b: (seq0 + b, p)),
        out_shape=jax.ShapeDtypeStruct((ROWS, C_W), BF16),
        input_output_aliases={6: 0},
        compiler_params=_cparams(("arbitrary", "arbitrary")),
        name="attn_na",
    )(z, z, z, ck, cv, bias, y_prev)


def _seg_sum(x, n_heads):
    parts = []
    for h in range(n_heads):
        s = jnp.sum(x[:, h * HD:(h + 1) * HD], axis=-1, keepdims=True)
        parts.append(jnp.broadcast_to(s, (x.shape[0], HD)))
    return jnp.concatenate(parts, axis=-1)


PREP_TM = 256


def _seq_len_at(row_start):
    return jnp.where(row_start < ROWS_CTX, T_CTX, T_LAT)


def _rwkv_prep_kernel(z_ref, zprev_ref, znext_ref, mu_ref, kkw_ref, w0_ref, w2_ref, a0_ref, a2_ref, ka_ref,
                      rk_ref, g2_ref, r_ref, v_ref, kk_ref, w_ref, kd_ref, bb_ref, g_ref, bonus_ref):
    z = z_ref[:, A_IN:]
    t = z.shape[0]
    start = pl.program_id(0) * t
    seq_len = _seq_len_at(start)
    pos = (start - jnp.where(start < ROWS_CTX, 0, ROWS_CTX)) % seq_len
    halo_prev = jnp.where(pos == 0, 0.0, zprev_ref[SUBLANES - 1:SUBLANES, A_IN:])
    halo_next = jnp.where(pos + t == seq_len, 0.0, znext_ref[0:1, A_IN:])
    row = lax.broadcasted_iota(jnp.int32, (t, 1), 0)
    prev = jnp.where(row == 0, halo_prev, pltpu.roll(z, 1, 0))
    nxt = jnp.where(row == t - 1, halo_next, pltpu.roll(z, t - 1, 0))
    m = z + mu_ref[...] * (0.5 * (prev + nxt) - z)
    r = m[:, :B_W]
    k = m[:, B_W:2 * B_W]
    v = m[:, 2 * B_W:3 * B_W]
    o = 3 * B_W
    wd = m[:, o:o + 2 * LORA_W]
    ad = m[:, o + 2 * LORA_W:o + 2 * LORA_W + 2 * LORA_A]
    gd = m[:, o + 2 * LORA_W + 2 * LORA_A:]

    kkr = k * kkw_ref[...]
    kk = kkr * lax.rsqrt(_seg_sum(kkr * kkr, B_HEADS) + 1e-12)

    wl = w0_ref[...] + jnp.dot(jnp.tanh(wd), w2_ref[...], precision=HIGHEST, preferred_element_type=F32)
    decay = jnp.exp(-float(np.exp(-0.5)) * jax.nn.sigmoid(wl))
    a = jax.nn.sigmoid(a0_ref[...] + jnp.dot(ad, a2_ref[...], precision=HIGHEST, preferred_element_type=F32))
    k2 = jnp.concatenate([k, k], axis=-1)
    ka2 = jnp.concatenate([ka_ref[...], ka_ref[...]], axis=-1)
    kk2 = jnp.concatenate([kk, kk], axis=-1)

    r_ref[...] = r
    v_ref[...] = v
    kk_ref[...] = kk
    w_ref[...] = decay
    kd_ref[...] = k2 * (1.0 + (a - 1.0) * ka2)
    bb_ref[...] = kk2 * a
    g_ref[...] = jnp.dot(jax.nn.sigmoid(gd).astype(BF16), g2_ref[...].astype(BF16), preferred_element_type=F32)
    bonus_ref[...] = _seg_sum(r * k * rk_ref[...], B_HEADS) * v


def rwkv_prep(z, params):
    tm = PREP_TM
    full = lambda a: pl.BlockSpec(a.shape, lambda s: (0,) * a.ndim)
    widths = (B_W, B_W, B_W, 2 * B_W, 2 * B_W, 2 * B_W, B_W, B_W)
    per = tm // SUBLANES
    last = ROWS // SUBLANES - 1
    return pl.pallas_call(
        _rwkv_prep_kernel,
        grid=(ROWS // tm,),
        in_specs=[pl.BlockSpec((tm, EVEN_IN), lambda s: (s, 0)),
                  pl.BlockSpec((SUBLANES, EVEN_IN), lambda s: (jnp.maximum(s * per - 1, 0), 0)),
                  pl.BlockSpec((SUBLANES, EVEN_IN), lambda s: (jnp.minimum((s + 1) * per, last), 0))]
                 + [full(p) for p in params],
        out_specs=[pl.BlockSpec((tm, w), lambda s: (s, 0)) for w in widths],
        out_shape=[jax.ShapeDtypeStruct((ROWS, w), F32) for w in widths],
        compiler_params=_cparams(("arbitrary",)),
        name="rwkv_prep",
    )(z, z, z, *params)


def rwkv_prep_all(z, prm, e):
    w2 = prm['b_w2'][e]
    a2 = prm['b_a2'][e]
    zero = jnp.zeros((LORA_W, B_W), F32)
    w2bd = jnp.concatenate([jnp.concatenate([w2[0], zero], 1), jnp.concatenate([zero, w2[1]], 1)], 0)
    a2bd = jnp.concatenate([jnp.concatenate([a2[0], zero], 1), jnp.concatenate([zero, a2[1]], 1)], 0)
    params = (prm['b_mu'][e][None], prm['b_kk'][e][None], prm['b_w0'][e].reshape(1, 2 * B_W), w2bd,
              prm['b_a0'][e].reshape(1, 2 * B_W), a2bd, prm['b_ka'][e][None],
              prm['b_rk'][e].reshape(1, B_W), prm['b_g2'][e])
    return rwkv_prep(z, params)


SCAN_CHAINS = 8
N_SCAN_VECS = 5


def _tree_sum(xs):
    while len(xs) > 1:
        xs = [xs[i] + xs[i + 1] for i in range(0, len(xs) - 1, 2)] + ([xs[-1]] if len(xs) % 2 else [])
    return xs[0]


def _scan_steps(d, r_s, kk_s, w_s, k_s, b_s, v_s, y_s, s_ref, ni, tc):
    n_acc = max(1, SCAN_CHAINS // ni)

    def bcast(ref, t, j):
        return jnp.broadcast_to(ref[t, pl.ds(j, 1), :], (SUBLANES, LANES))

    def step(i, carry):
        t = i + d * (tc - 1 - 2 * i)
        sa = [[None] * n_acc for _ in range(ni)]
        for j in range(HD):
            kkj = bcast(kk_s, t, j)
            for g in range(ni):
                p = s_ref[j, pl.ds(g * SUBLANES, SUBLANES), :] * kkj
                a = j % n_acc
                sa[g][a] = p if sa[g][a] is None else sa[g][a] + p
        sa = [-_tree_sum(x) for x in sa]
        vv = [v_s[t, pl.ds(g * SUBLANES, SUBLANES), :] for g in range(ni)]
        ya = [[None] * n_acc for _ in range(ni)]
        for j in range(HD):
            wj = bcast(w_s, t, j)
            bj = bcast(b_s, t, j)
            kj = bcast(k_s, t, j)
            rj = bcast(r_s, t, j)
            for g in range(ni):
                sl = pl.ds(g * SUBLANES, SUBLANES)
                s_new = s_ref[j, sl, :] * wj + sa[g] * bj + vv[g] * kj
                s_ref[j, sl, :] = s_new
                p = s_new * rj
                a = j % n_acc
                ya[g][a] = p if ya[g][a] is None else ya[g][a] + p
        for g in range(ni):
            y_s[t, pl.ds(g * SUBLANES, SUBLANES), :] = _tree_sum(ya[g])
        return carry

    lax.fori_loop(0, tc, step, 0)


def _scan_ctx_kernel(r_ref, kk_ref, v_ref, w_ref, k_ref, b_ref, y_ref, st_ref,
                     r_s, kk_s, v_s, w_s, k_s, b_s, y_s, s_ref, *, tc):
    d = pl.program_id(0)
    c = pl.program_id(1)

    @pl.when(c == 0)
    def _():
        s_ref[...] = jnp.zeros_like(s_ref)

    def load(t, carry):
        for src, dst in ((r_ref, r_s), (kk_ref, kk_s), (v_ref, v_s), (w_ref, w_s), (k_ref, k_s), (b_ref, b_s)):
            dst[t] = src[:, :, t].reshape(LANES, HD).T
        return carry

    lax.fori_loop(0, tc, load, 0)
    _scan_steps(d, r_s, kk_s, w_s, k_s, b_s, v_s, y_s, s_ref, HD // SUBLANES, tc)

    def store(t, carry):
        y_ref[:, :, t] = y_s[t].T.reshape(CTX_GROUPS, ROW_SUB, B_HEADS, HD)
        return carry

    lax.fori_loop(0, tc, store, 0)

    @pl.when(c == pl.num_programs(1) - 1)
    def _():
        st_ref[...] = s_ref[...]


def _rows5(x, heads):
    return x.reshape(ROW_GROUPS, ROW_SUB, T_CTX, heads, HD)


def rwkv_scan_ctx(prep, tc):
    r, v, kk, w, kd, bb = prep[:6]
    nc = T_CTX // tc

    def chunk(d, c):
        return c + d * (nc - 1 - 2 * c)

    blk = (CTX_GROUPS, ROW_SUB, tc, B_HEADS, HD)
    shared = pl.BlockSpec(blk, lambda d, c: (0, 0, chunk(d, c), 0, 0))
    perdir = pl.BlockSpec(blk, lambda d, c: (0, 0, chunk(d, c), d, 0))
    vec_s = pltpu.VMEM((tc, HD, LANES), F32)
    y, st = pl.pallas_call(
        functools.partial(_scan_ctx_kernel, tc=tc),
        grid=(2, nc),
        in_specs=[shared, shared, shared, perdir, perdir, perdir],
        out_specs=[perdir, pl.BlockSpec((None, HD, HD, LANES), lambda d, c: (d, 0, 0, 0))],
        out_shape=[jax.ShapeDtypeStruct((ROW_GROUPS, ROW_SUB, T_CTX, 2 * B_HEADS, HD), F32),
                   jax.ShapeDtypeStruct((2, HD, HD, LANES), F32)],
        scratch_shapes=[vec_s] * 7 + [pltpu.VMEM((HD, HD, LANES), F32)],
        compiler_params=_cparams(("arbitrary", "arbitrary")),
        name="rwkv_scan_ctx",
    )(_rows5(r, B_HEADS), _rows5(kk, B_HEADS), _rows5(v, B_HEADS),
      _rows5(w, 2 * B_HEADS), _rows5(kd, 2 * B_HEADS), _rows5(bb, 2 * B_HEADS))
    return y, st


LAT_REP = LANES // (N_LAT * B_HEADS)


def _scan_lat_kernel(r_ref, kk_ref, v_ref, w_ref, k_ref, b_ref, s0_ref, yprev_ref, y_ref,
                     r_s, kk_s, v_s, w_s, k_s, b_s, y_s, s_ref, *, tc):
    del yprev_ref
    d = pl.program_id(0)
    c = pl.program_id(1)
    n = N_LAT * B_HEADS

    @pl.when(c == 0)
    def _():
        s_ref[...] = s0_ref[...]

    def load(t, carry):
        for src, dst in ((r_ref, r_s), (kk_ref, kk_s), (w_ref, w_s), (k_ref, k_s), (b_ref, b_s)):
            m = src[:, 0, t].reshape(n, HD)
            dst[t] = jnp.concatenate([m] * LAT_REP, axis=0).T
        m = v_ref[:, 0, t].reshape(n, HD)
        vt = jnp.concatenate([m, jnp.zeros((LANES - n, HD), F32)], axis=0).T
        acc = vt[0:SUBLANES]
        for g in range(1, LAT_REP):
            acc = acc + pltpu.roll(vt[g * SUBLANES:(g + 1) * SUBLANES], g * n, 1)
        v_s[t] = acc
        return carry

    lax.fori_loop(0, tc, load, 0)
    _scan_steps(d, r_s, kk_s, w_s, k_s, b_s, v_s, y_s, s_ref, 1, tc)

    def store(t, carry):
        y = y_s[t]
        rows = [y] + [pltpu.roll(y, LANES - g * n, 1) for g in range(1, LAT_REP)]
        yt = jnp.concatenate(rows, axis=0).T
        y_ref[:, 0, t] = yt[:n].reshape(N_LAT, B_HEADS, HD)
        return carry

    lax.fori_loop(0, tc, store, 0)


def rwkv_scan_lat(prep, s0f, s0b, y_prev, tc):
    r, v, kk, w, kd, bb = prep[:6]
    nc = T_LAT // tc
    per_sub = T_CTX // tc

    def pos(d, c):
        cc = c + d * (nc - 1 - 2 * c)
        return cc // per_sub, cc % per_sub

    blk = (N_LAT, 1, tc, B_HEADS, HD)
    g0 = CTX_GROUPS // N_LAT

    def shared_map(d, c):
        q, off = pos(d, c)
        return (g0, q, off, 0, 0)

    def perdir_map(d, c):
        q, off = pos(d, c)
        return (g0, q, off, d, 0)

    shared = pl.BlockSpec(blk, shared_map)
    perdir = pl.BlockSpec(blk, perdir_map)
    s0 = jnp.stack([s0f, s0b]).reshape(2, N_LAT, B_HEADS, LAT_REP, SUBLANES, HD)
    s0 = s0.transpose(0, 5, 4, 3, 1, 2).reshape(2, HD, SUBLANES, LANES)
    vec_s = pltpu.VMEM((tc, HD, LANES), F32)
    val_s = pltpu.VMEM((tc, SUBLANES, LANES), F32)
    return pl.pallas_call(
        functools.partial(_scan_lat_kernel, tc=tc),
        grid=(2, nc),
        in_specs=[shared, shared, shared, perdir, perdir, perdir,
                  pl.BlockSpec((None, HD, SUBLANES, LANES), lambda d, c: (d, 0, 0, 0)),
                  _untouched()],
        out_specs=perdir,
        out_shape=jax.ShapeDtypeStruct((ROW_GROUPS, ROW_SUB, T_CTX, 2 * B_HEADS, HD), F32),
        scratch_shapes=[vec_s, vec_s, val_s, vec_s, vec_s, vec_s, val_s,
                        pltpu.VMEM((HD, SUBLANES, LANES), F32)],
        input_output_aliases={7: 0},
        compiler_params=_cparams(("arbitrary", "arbitrary")),
        name="rwkv_scan_lat",
    )(_rows5(r, B_HEADS), _rows5(kk, B_HEADS), _rows5(v, B_HEADS),
      _rows5(w, 2 * B_HEADS), _rows5(kd, 2 * B_HEADS), _rows5(bb, 2 * B_HEADS), s0, y_prev)


def _rwkv_post_kernel(y_ref, bonus_ref, g_ref, lng_ref, lnb_ref, o_ref):
    y = y_ref[:, :B_W] + y_ref[:, B_W:]
    mu = _seg_sum(y, B_HEADS) * (1.0 / HD)
    yc = y - mu
    var = _seg_sum(yc * yc, B_HEADS) * (1.0 / HD)
    yn = yc * lax.rsqrt(var + GN_EPS) * lng_ref[...] + lnb_ref[...]
    o_ref[...] = ((yn + bonus_ref[...]) * g_ref[...]).astype(BF16)


def rwkv_post(y, bonus, g, lng, lnb, e):
    tm = 512
    rowspec = pl.BlockSpec((tm, B_W), lambda i: (i, 0))
    vecspec = pl.BlockSpec((None, 1, B_W), lambda i: (e, 0, 0))
    return pl.pallas_call(
        _rwkv_post_kernel,
        grid=(ROWS // tm,),
        in_specs=[pl.BlockSpec((tm, 2 * B_W), lambda i: (i, 0)), rowspec, rowspec, vecspec, vecspec],
        out_specs=rowspec,
        out_shape=jax.ShapeDtypeStruct((ROWS, B_W), BF16),
        compiler_params=_cparams(("arbitrary",)),
        name="rwkv_post",
    )(y, bonus, g, lng, lnb)


def kernel(x_prompt, x_sample, cache_a_k, cache_a_v, state_b_fwd, state_b_bwd, cache_c_k, cache_c_v, c, c_ctx,
           ada_w, ada_b, norm1_g, norm2_g, final_norm_g, w_in_e, w_out_e, a_q_gain, a_k_gain, b_mu, b_w0, b_w2,
           b_a0, b_a2, b_g2, b_kk, b_ka, b_rk, b_ln_g, b_ln_b, ffn_w_gu, ffn_w_dn, w_in_o, w_out_o, c_rpb,
           router_w, router_b, moe_w_gu, moe_w_dn):
    n_even, n_odd = w_in_e.shape[0], w_in_o.shape[0]
    prm = dict(b_mu=b_mu, b_w0=b_w0, b_w2=b_w2, b_a0=b_a0, b_a2=b_a2, b_g2=b_g2, b_kk=b_kk, b_ka=b_ka,
               b_rk=b_rk)
    cond = jnp.zeros((N_MODS, D), F32).at[0].set(c_ctx).at[1:1 + N_LAT].set(c)
    mods = ada_all(cond, ada_w, ada_b).reshape(DEPTH, N_MODS, 6, D)
    g1 = norm1_g.reshape(DEPTH, 1, D)
    g2 = norm2_g.reshape(DEPTH, 1, D)
    gq = a_q_gain.reshape(n_even, 1, HD)
    gk = a_k_gain.reshape(n_even, 1, HD)
    lng = b_ln_g.reshape(n_even, 1, B_W)
    lnb = b_ln_b.reshape(n_even, 1, B_W)
    ck_a = cache_a_k.reshape(N_LAT, n_even, PAST, A_KV_W)
    cv_a = cache_a_v.reshape(N_LAT, n_even, PAST, A_KV_W)
    ck_c = cache_c_k.reshape(N_LAT, n_odd, PAST, C_W)
    cv_c = cache_c_v.reshape(N_LAT, n_odd, PAST, C_W)
    na_bias = _na_bias_table(c_rpb)
    rw_p = jnp.zeros((n_odd, D, LANES), F32).at[:, :, :N_EXPERTS].set(router_w)
    rb_p = jnp.zeros((n_odd, 1, LANES), F32).at[:, 0, :N_EXPERTS].set(router_b)

    x, h = first_norm(x_prompt, x_sample, g1, mods)

    a_caches, c_caches = None, None
    new_sf, new_sb = [], []
    y_final = None
    for l in range(DEPTH):
        if l % 2 == 0:
            e = l // 2
            z = mm_in(h, w_in_e, e, tn=EVEN_IN // 3)
            y_a, k_new, v_new = attn_a_ctx(z, gq, gk, e, a_caches)
            a_caches = (k_new, v_new)
            y_a = attn_a_lat(z, ck_a, cv_a, gq, gk, e, y_a)
            prep = rwkv_prep_all(z, prm, e)
            y_s, st = rwkv_scan_ctx(prep, tc=16)
            y_s = rwkv_scan_lat(prep, state_b_fwd[:, e], state_b_bwd[:, e], y_s, tc=64)
            st = st.reshape(2, HD, HD, N_CTX, B_HEADS).transpose(0, 3, 4, 2, 1)
            new_sf.append(st[0])
            new_sb.append(st[1])
            y_b = rwkv_post(y_s.reshape(ROWS, 2 * B_W), prep[7], prep[6], lng, lnb, e)
            x, h = mm_out([y_a, y_b], w_out_e, e, x, mods, l, g2, l)
            x, h = ffn(h, ffn_w_gu, ffn_w_dn, e, x, mods, l, g1)
        else:
            o = l // 2
            z = mm_in(h, w_in_o, o, tn=C_W)
            y, k_new, v_new = attn_c_ctx(z, o, n_odd, c_caches)
            c_caches = (k_new, v_new)
            y = attn_na(z, ck_c, cv_c, na_bias, o, y)
            x, h = mm_out([y], w_out_o, o, x, mods, l, g2, l)
            gates = router(x, g2, mods, l, rw_p, rb_p, o)
            y_moe = moe_dense(h, gates, moe_w_gu, moe_w_dn, o)
            if l + 1 < DEPTH:
                x, h = resid(x, y_moe, mods, l, g1, False)
            else:
                (y_final,) = resid(x, y_moe, mods, l, final_norm_g[None], True)

    y_prompt = y_final[:ROWS_CTX].reshape(N_CTX, T_CTX, D)
    y_sample = y_final[ROWS_CTX:].reshape(N_LAT, T_LAT, D)
    return (y_prompt, y_sample,
            a_caches[0].reshape(N_CTX, n_even, T_CTX, A_KV_HEADS, HD),
            a_caches[1].reshape(N_CTX, n_even, T_CTX, A_KV_HEADS, HD),
            jnp.stack(new_sf, axis=1), jnp.stack(new_sb, axis=1),
            c_caches[0].reshape(N_CTX, n_odd, T_CTX, C_HEADS, HD),
            c_caches[1].reshape(N_CTX, n_odd, T_CTX, C_HEADS, HD))
```

```python
import functools

import numpy as np
import jax
import jax.numpy as jnp
from jax import lax
from jax.experimental import pallas as pl
from jax.experimental.pallas import tpu as pltpu

F32 = jnp.float32
BF16 = jnp.bfloat16
HIGHEST = lax.Precision.HIGHEST

D = 1024
N_CTX, T_CTX = 16, 256
N_LAT, T_LAT = 2, 1024
ROWS_CTX = N_CTX * T_CTX
ROWS_LAT = N_LAT * T_LAT
ROWS = ROWS_CTX + ROWS_LAT
DEPTH = 4
GRID_W = 64
HD = 64
A_HEADS, A_KV_HEADS, B_HEADS, C_HEADS = 8, 2, 8, 16
A_W, A_KV_W, B_W, C_W = A_HEADS * HD, A_KV_HEADS * HD, B_HEADS * HD, C_HEADS * HD
A_IN = A_W + 2 * A_KV_W
LORA_W, LORA_A, LORA_G = 64, 64, 128
B_IN = 3 * B_W + 2 * LORA_W + 2 * LORA_A + LORA_G
EVEN_IN = A_IN + B_IN
PAST = 512
NA_ROWS, NA_COLS = 8, 16
D_FF = 2816
N_EXPERTS = 8
D_EXPERT = 1408
ROPE_THETA = 10000.0
EPS = 1e-6
GN_EPS = 64e-5
NEG_INF = -1e30
ATT_SCALE = HD ** -0.5

LANES = 128
SUBLANES = 8
VMEM_LIMIT = 56 * 1024 * 1024

N_MODS = 8

ROW_SUB = T_LAT // T_CTX
ROW_GROUPS = ROWS // (ROW_SUB * T_CTX)
CTX_GROUPS = N_CTX // ROW_SUB


def _cparams(sem):
    return pltpu.CompilerParams(dimension_semantics=sem, vmem_limit_bytes=VMEM_LIMIT)


def _untouched():
    return pl.BlockSpec(memory_space=pl.ANY)


def _mod_index(row_start):
    return jnp.where(row_start < ROWS_CTX, 0, 1 + (row_start - ROWS_CTX) // T_LAT)


def _modspec(layer, tm, nargs=1):
    if nargs == 1:
        return pl.BlockSpec((None, 1, 6, D), lambda i: (layer, _mod_index(i * tm), 0, 0))
    return pl.BlockSpec((None, 1, 6, D), lambda i, j: (layer, _mod_index(i * tm), 0, 0))


def _gainspec(layer, nargs=1):
    if nargs == 1:
        return pl.BlockSpec((None, 1, D), lambda i: (layer, 0, 0))
    return pl.BlockSpec((None, 1, D), lambda i, j: (layer, 0, 0))


def _modnorm(x, g, shift, scale):
    ms = jnp.mean(x * x, axis=-1, keepdims=True)
    return (x * lax.rsqrt(ms + EPS) * g) * (1.0 + scale) + shift


def _silu(x):
    return x * jax.nn.sigmoid(x)


def _ada_kernel(c_ref, w_ref, b_ref, o_ref):
    s = _silu(c_ref[...]).astype(BF16)
    o_ref[0] = jnp.dot(s, w_ref[0].astype(BF16), preferred_element_type=F32) + b_ref[0]


def ada_all(cond, ada_w, ada_b):
    tn = 1536
    n = 6 * D
    return pl.pallas_call(
        _ada_kernel,
        grid=(DEPTH, n // tn),
        in_specs=[pl.BlockSpec((N_MODS, D), lambda l, j: (0, 0)),
                  pl.BlockSpec((1, D, tn), lambda l, j: (l, 0, j)),
                  pl.BlockSpec((1, 1, tn), lambda l, j: (l, 0, j))],
        out_specs=pl.BlockSpec((1, N_MODS, tn), lambda l, j: (l, 0, j)),
        out_shape=jax.ShapeDtypeStruct((DEPTH, N_MODS, n), F32),
        compiler_params=_cparams(("arbitrary", "arbitrary")),
        name="ada",
    )(cond, ada_w, ada_b.reshape(DEPTH, 1, n))


def _first_norm_kernel(xp_ref, xs_ref, g_ref, mod_ref, x_ref, h_ref, *, n_ctx_blocks):
    i = pl.program_id(0)

    def emit(src):
        x = src[...]
        x_ref[...] = x
        h_ref[...] = _modnorm(x, g_ref[...], mod_ref[0, 0:1, :], mod_ref[0, 1:2, :]).astype(BF16)

    @pl.when(i < n_ctx_blocks)
    def _():
        emit(xp_ref)

    @pl.when(i >= n_ctx_blocks)
    def _():
        emit(xs_ref)


def first_norm(x_prompt, x_sample, gains, mods):
    tm = 512
    nc = ROWS_CTX // tm
    rowspec = pl.BlockSpec((tm, D), lambda i: (i, 0))
    return pl.pallas_call(
        functools.partial(_first_norm_kernel, n_ctx_blocks=nc),
        grid=(ROWS // tm,),
        in_specs=[pl.BlockSpec((tm, D), lambda i: (jnp.minimum(i, nc - 1), 0)),
                  pl.BlockSpec((tm, D), lambda i: (jnp.maximum(i - nc, 0), 0)),
                  _gainspec(0), _modspec(0, tm)],
        out_specs=[rowspec, rowspec],
        out_shape=[jax.ShapeDtypeStruct((ROWS, D), F32), jax.ShapeDtypeStruct((ROWS, D), BF16)],
        compiler_params=_cparams(("arbitrary",)),
        name="first_norm",
    )(x_prompt.reshape(ROWS_CTX, D), x_sample.reshape(ROWS_LAT, D), gains, mods)


def _mm_in_kernel(h_ref, w_ref, o_ref, wb_ref):
    @pl.when(pl.program_id(1) == 0)
    def _():
        wb_ref[...] = w_ref[...].astype(BF16)

    o_ref[...] = jnp.dot(h_ref[...], wb_ref[...], preferred_element_type=F32)


def mm_in(h, w, layer, tn, tm=512):
    _, k, n = w.shape
    return pl.pallas_call(
        _mm_in_kernel,
        grid=(n // tn, ROWS // tm),
        in_specs=[pl.BlockSpec((tm, k), lambda j, i: (i, 0)),
                  pl.BlockSpec((None, k, tn), lambda j, i: (layer, 0, j))],
        out_specs=pl.BlockSpec((tm, tn), lambda j, i: (i, j)),
        out_shape=jax.ShapeDtypeStruct((ROWS, n), F32),
        scratch_shapes=[pltpu.VMEM((k, tn), BF16)],
        compiler_params=_cparams(("arbitrary", "arbitrary")),
        name="mm_in",
    )(h, w)


def _residual_epilogue(x, acc, mod_ref, nmod_ref, g_ref, xo_ref, ho_ref, gate_idx, shift_idx, scale_idx, final):
    gate = mod_ref[0, gate_idx:gate_idx + 1, :]
    xn = x + gate * acc
    if final:
        ms = jnp.mean(xn * xn, axis=-1, keepdims=True)
        ho_ref[...] = xn * lax.rsqrt(ms + EPS) * g_ref[...]
    else:
        xo_ref[...] = xn
        shift = nmod_ref[0, shift_idx:shift_idx + 1, :]
        scale = nmod_ref[0, scale_idx:scale_idx + 1, :]
        ho_ref[...] = _modnorm(xn, g_ref[...], shift, scale).astype(BF16)


def _mm_out_kernel(*refs, n_parts, gate_idx, shift_idx, scale_idx):
    y_refs = refs[:n_parts]
    w_refs = refs[n_parts:2 * n_parts]
    x_ref, mod_ref, nmod_ref, g_ref, xo_ref, ho_ref = refs[2 * n_parts:2 * n_parts + 6]
    wb_refs = refs[2 * n_parts + 6:]

    @pl.when(pl.program_id(0) == 0)
    def _():
        for w_ref, wb_ref in zip(w_refs, wb_refs):
            wb_ref[...] = w_ref[...].astype(BF16)

    acc = None
    for y_ref, wb_ref in zip(y_refs, wb_refs):
        p = jnp.dot(y_ref[...], wb_ref[...], preferred_element_type=F32)
        acc = p if acc is None else acc + p
    _residual_epilogue(x_ref[...], acc, mod_ref, nmod_ref, g_ref, xo_ref, ho_ref,
                       gate_idx, shift_idx, scale_idx, False)


def mm_out(parts, w, layer_w, x, mods, layer, gains, gain_layer):
    tm = 512
    kp = parts[0].shape[1]
    n_parts = len(parts)
    rowspec = pl.BlockSpec((tm, D), lambda i: (i, 0))
    return pl.pallas_call(
        functools.partial(_mm_out_kernel, n_parts=n_parts, gate_idx=2, shift_idx=3, scale_idx=4),
        grid=(ROWS // tm,),
        in_specs=[pl.BlockSpec((tm, kp), lambda i: (i, 0)) for _ in parts]
                 + [pl.BlockSpec((None, kp, D), lambda i, p=p: (layer_w, p, 0)) for p in range(n_parts)]
                 + [rowspec, _modspec(layer, tm), _modspec(layer, tm), _gainspec(gain_layer)],
        out_specs=[rowspec, rowspec],
        out_shape=[jax.ShapeDtypeStruct((ROWS, D), F32), jax.ShapeDtypeStruct((ROWS, D), BF16)],
        scratch_shapes=[pltpu.VMEM((kp, D), BF16) for _ in parts],
        compiler_params=_cparams(("arbitrary",)),
        name="mm_out",
    )(*parts, *([w] * n_parts), x, mods, mods, gains)


FFN_TF = 256
FFN_NF = D_FF // FFN_TF


def _ffn_kernel(h_ref, wg_ref, wu_ref, wd_ref, x_ref, mod_ref, nmod_ref, g_ref, xo_ref, ho_ref,
                wg_s, wu_s, wd_s, acc_ref):
    i = pl.program_id(0)
    f = pl.program_id(1)

    @pl.when(i == 0)
    def _():
        wg_s[f] = wg_ref[...].astype(BF16)
        wu_s[f] = wu_ref[...].astype(BF16)
        wd_s[f] = wd_ref[...].astype(BF16)

    @pl.when(f == 0)
    def _():
        acc_ref[...] = jnp.zeros_like(acc_ref)

    h = h_ref[...]
    gp = jnp.dot(h, wg_s[f], preferred_element_type=F32)
    up = jnp.dot(h, wu_s[f], preferred_element_type=F32)
    a = (_silu(gp) * up).astype(BF16)
    acc_ref[...] += jnp.dot(a, wd_s[f], preferred_element_type=F32)

    @pl.when(f == FFN_NF - 1)
    def _():
        _residual_epilogue(x_ref[...], acc_ref[...], mod_ref, nmod_ref, g_ref, xo_ref, ho_ref, 5, 0, 1, False)


def ffn(h, w_gu, w_dn, e, x, mods, layer, gains):
    tm = 512
    tf, nf = FFN_TF, FFN_NF

    def once(i, f):
        return jnp.where(i == 0, f, nf - 1)

    rowspec = pl.BlockSpec((tm, D), lambda i, f: (i, 0))
    return pl.pallas_call(
        _ffn_kernel,
        grid=(ROWS // tm, nf),
        in_specs=[rowspec,
                  pl.BlockSpec((None, D, tf), lambda i, f: (e, 0, once(i, f))),
                  pl.BlockSpec((None, D, tf), lambda i, f: (e, 0, nf + once(i, f))),
                  pl.BlockSpec((None, tf, D), lambda i, f: (e, once(i, f), 0)),
                  rowspec, _modspec(layer, tm, 2), _modspec(layer + 1, tm, 2), _gainspec(layer + 1, 2)],
        out_specs=[rowspec, rowspec],
        out_shape=[jax.ShapeDtypeStruct((ROWS, D), F32), jax.ShapeDtypeStruct((ROWS, D), BF16)],
        scratch_shapes=[pltpu.VMEM((nf, D, tf), BF16),
                        pltpu.VMEM((nf, D, tf), BF16),
                        pltpu.VMEM((nf, tf, D), BF16),
                        pltpu.VMEM((tm, D), F32)],
        compiler_params=_cparams(("arbitrary", "arbitrary")),
        name="ffn",
    )(h, w_gu, w_gu, w_dn, x, mods, mods, gains)


def _router_kernel(x_ref, g_ref, mod_ref, rw_ref, rb_ref, gates_ref):
    h = _modnorm(x_ref[...], g_ref[...], mod_ref[0, 3:4, :], mod_ref[0, 4:5, :])
    logits = jnp.dot(h, rw_ref[...], precision=HIGHEST, preferred_element_type=F32)
    lane = lax.broadcasted_iota(jnp.int32, logits.shape, 1)
    sel = jnp.where(lane < N_EXPERTS, logits + rb_ref[...], -jnp.inf)
    m1 = jnp.max(sel, axis=-1, keepdims=True)
    i1 = jnp.min(jnp.where(sel == m1, lane, LANES), axis=-1, keepdims=True)
    sel2 = jnp.where(lane == i1, -jnp.inf, sel)
    m2 = jnp.max(sel2, axis=-1, keepdims=True)
    i2 = jnp.min(jnp.where(sel2 == m2, lane, LANES), axis=-1, keepdims=True)
    l1 = jnp.sum(jnp.where(lane == i1, logits, 0.0), axis=-1, keepdims=True)
    l2 = jnp.sum(jnp.where(lane == i2, logits, 0.0), axis=-1, keepdims=True)
    mx = jnp.maximum(l1, l2)
    e1 = jnp.exp(l1 - mx)
    e2 = jnp.exp(l2 - mx)
    den = e1 + e2
    gates_ref[...] = jnp.where(lane == i1, e1 / den, 0.0) + jnp.where(lane == i2, e2 / den, 0.0)


def router(x, gains, mods, layer, rw_p, rb_p, o):
    tm = 512
    return pl.pallas_call(
        _router_kernel,
        grid=(ROWS // tm,),
        in_specs=[pl.BlockSpec((tm, D), lambda i: (i, 0)),
                  _gainspec(layer), _modspec(layer, tm),
                  pl.BlockSpec((None, D, LANES), lambda i: (o, 0, 0)),
                  pl.BlockSpec((None, 1, LANES), lambda i: (o, 0, 0))],
        out_specs=pl.BlockSpec((tm, LANES), lambda i: (i, 0)),
        out_shape=jax.ShapeDtypeStruct((ROWS, LANES), F32),
        compiler_params=_cparams(("arbitrary",)),
        name="router",
    )(x, gains, mods, rw_p, rb_p)


def _moe_kernel(h_ref, gates_ref, wgu_ref, wdn_ref, yin_ref, yo_ref, wgu_s, wdn_s):
    e = pl.program_id(0)

    @pl.when(pl.program_id(1) == 0)
    def _():
        wgu_s[...] = wgu_ref[...].astype(BF16)
        wdn_s[...] = wdn_ref[...].astype(BF16)

    gu = jnp.dot(h_ref[...], wgu_s[...], preferred_element_type=F32)
    a = (_silu(gu[:, :D_EXPERT]) * gu[:, D_EXPERT:]).astype(BF16)
    y = jnp.dot(a, wdn_s[...], preferred_element_type=F32)
    gt = gates_ref[...]
    lane = lax.broadcasted_iota(jnp.int32, gt.shape, 1)
    gate = jnp.sum(jnp.where(lane == e, gt, 0.0), axis=-1, keepdims=True)
    yo_ref[...] = yin_ref[...] + gate * y


def moe_dense(h, gates, w_gu, w_dn, o):
    tm = 256
    y0 = jnp.zeros((ROWS, D), F32)
    return pl.pallas_call(
        _moe_kernel,
        grid=(N_EXPERTS, ROWS // tm),
        in_specs=[pl.BlockSpec((tm, D), lambda e, i: (i, 0)),
                  pl.BlockSpec((tm, LANES), lambda e, i: (i, 0)),
                  pl.BlockSpec((None, None, D, 2 * D_EXPERT), lambda e, i: (o, e, 0, 0),
                               pipeline_mode=pl.Buffered(1)),
                  pl.BlockSpec((None, None, D_EXPERT, D), lambda e, i: (o, e, 0, 0),
                               pipeline_mode=pl.Buffered(1)),
                  pl.BlockSpec((tm, D), lambda e, i: (i, 0))],
        out_specs=pl.BlockSpec((tm, D), lambda e, i: (i, 0)),
        out_shape=jax.ShapeDtypeStruct((ROWS, D), F32),
        scratch_shapes=[pltpu.VMEM((D, 2 * D_EXPERT), BF16),
                        pltpu.VMEM((D_EXPERT, D), BF16)],
        input_output_aliases={4: 0},
        compiler_params=_cparams(("arbitrary", "arbitrary")),
        name="moe",
    )(h, gates, w_gu, w_dn, y0)


def _resid_kernel(x_ref, y_ref, mod_ref, nmod_ref, g_ref, *out_refs, final):
    if final:
        xo_ref, ho_ref = None, out_refs[0]
    else:
        xo_ref, ho_ref = out_refs
    _residual_epilogue(x_ref[...], y_ref[...], mod_ref, nmod_ref, g_ref, xo_ref, ho_ref, 5, 0, 1, final)


def resid(x, y, mods, layer, gains, final):
    tm = 512
    rowspec = pl.BlockSpec((tm, D), lambda i: (i, 0))
    if final:
        out_specs = [rowspec]
        out_shape = [jax.ShapeDtypeStruct((ROWS, D), F32)]
        nmod, gain = _modspec(layer, tm), pl.BlockSpec((1, D), lambda i: (0, 0))
    else:
        out_specs = [rowspec, rowspec]
        out_shape = [jax.ShapeDtypeStruct((ROWS, D), F32), jax.ShapeDtypeStruct((ROWS, D), BF16)]
        nmod, gain = _modspec(layer + 1, tm), _gainspec(layer + 1)
    return pl.pallas_call(
        functools.partial(_resid_kernel, final=final),
        grid=(ROWS // tm,),
        in_specs=[rowspec, rowspec, _modspec(layer, tm), nmod, gain],
        out_specs=out_specs,
        out_shape=out_shape,
        compiler_params=_cparams(("arbitrary",)),
        name="resid",
    )(x, y, mods, mods, gains)


def _softmax_pv(scores, values):
    m = None
    for s in scores:
        mi = jnp.max(s, axis=-1, keepdims=True)
        m = mi if m is None else jnp.maximum(m, mi)
    num, den = None, None
    for s, v in zip(scores, values):
        p = jnp.exp(s - m)
        li = jnp.sum(p, axis=-1, keepdims=True)
        oi = jnp.dot(p.astype(BF16), v, preferred_element_type=F32)
        num = oi if num is None else num + oi
        den = li if den is None else den + li
    return num / den


def _qk(q, k):
    return lax.dot_general(q, k, (((1,), (1,)), ((), ())), preferred_element_type=F32)


def _head_rmsnorm(x, gain, n_heads):
    parts = []
    for h in range(n_heads):
        xh = x[:, h * HD:(h + 1) * HD]
        ms = jnp.mean(xh * xh, axis=-1, keepdims=True)
        parts.append(xh * lax.rsqrt(ms + EPS) * gain)
    return parts


def _rope128(x, cos, sin_signed):
    lane = lax.broadcasted_iota(jnp.int32, x.shape, 1)
    up = pltpu.roll(x, LANES - 16, 1)
    dn = pltpu.roll(x, 16, 1)
    partner = jnp.where((lane % 32) < 16, up, dn)
    return x * cos + partner * sin_signed


def _rope_tables():
    t = np.arange(T_LAT)
    n_f = HD // 4
    inv = ROPE_THETA ** (-np.arange(n_f, dtype=np.float32) / n_f)
    cos = np.zeros((T_LAT, HD), np.float32)
    sin = np.zeros((T_LAT, HD), np.float32)
    for half, pos in ((0, t // GRID_W), (1, t % GRID_W)):
        ang = pos[:, None].astype(np.float32) * inv[None, :]
        c, s = np.cos(ang), np.sin(ang)
        base = half * (HD // 2)
        cos[:, base:base + n_f] = c
        cos[:, base + n_f:base + 2 * n_f] = c
        sin[:, base:base + n_f] = -s
        sin[:, base + n_f:base + 2 * n_f] = s
    return np.tile(cos, (1, 2)), np.tile(sin, (1, 2))


def _attn_a_ctx_kernel(q_ref, kv_ref, qg_ref, kg_ref, *rest):
    o_ref, ko_ref, vo_ref = rest[-3:]
    q = q_ref[...]
    kv = kv_ref[...]
    k = kv[:, :A_KV_W]
    v = kv[:, A_KV_W:]
    kn = _head_rmsnorm(k, kg_ref[...], A_KV_HEADS)
    ko_ref[...] = jnp.concatenate(kn, axis=-1)
    vo_ref[...] = v
    qn = _head_rmsnorm(q, qg_ref[...], A_HEADS)
    group = A_HEADS // A_KV_HEADS
    outs = []
    for h in range(A_HEADS):
        g = h // group
        kh = kn[g].astype(BF16)
        vh = v[:, g * HD:(g + 1) * HD].astype(BF16)
        qh = (qn[h] * ATT_SCALE).astype(BF16)
        outs.append(_softmax_pv([_qk(qh, kh)], [vh]))
    o_ref[...] = jnp.concatenate(outs, axis=-1).astype(BF16)


def attn_a_ctx(z, gq, gk, e, prev_caches):
    t = T_CTX
    n_even = gq.shape[0]
    cache_spec = pl.BlockSpec((None, None, t, A_KV_W), lambda b: (b, e, 0, 0))
    cache_shape = jax.ShapeDtypeStruct((N_CTX, n_even, t, A_KV_W), F32)
    extra = [] if prev_caches is None else list(prev_caches)
    aliases = {} if prev_caches is None else {4: 1, 5: 2}
    return pl.pallas_call(
        _attn_a_ctx_kernel,
        grid=(N_CTX,),
        in_specs=[pl.BlockSpec((t, A_W), lambda b: (b, 0)),
                  pl.BlockSpec((t, 2 * A_KV_W), lambda b: (b, A_W // (2 * A_KV_W))),
                  pl.BlockSpec((None, 1, HD), lambda b: (e, 0, 0)),
                  pl.BlockSpec((None, 1, HD), lambda b: (e, 0, 0))] + [_untouched() for _ in extra],
        out_specs=[pl.BlockSpec((t, A_W), lambda b: (b, 0)), cache_spec, cache_spec],
        out_shape=[jax.ShapeDtypeStruct((ROWS, A_W), BF16), cache_shape, cache_shape],
        input_output_aliases=aliases,
        compiler_params=_cparams(("arbitrary",)),
        name="attn_a_ctx",
    )(z, z, gq, gk, *extra)


A_LAT_TQ = 256


def _attn_a_lat_kernel(q_ref, kv_ref, ck_ref, cv_ref, qg_ref, kg_ref, cq_ref, sq_ref, ckk_ref, skk_ref,
                       prev_ref, o_ref, k_s, v_s):
    del prev_ref

    @pl.when(pl.program_id(1) == 0)
    def _():
        kv = kv_ref[...]
        kn = jnp.concatenate(_head_rmsnorm(kv[:, :A_KV_W], kg_ref[...], A_KV_HEADS), axis=-1)
        k_s[...] = _rope128(kn, ckk_ref[...], skk_ref[...]).astype(BF16)
        v_s[...] = kv[:, A_KV_W:].astype(BF16)

    qn = _head_rmsnorm(q_ref[...], qg_ref[...], A_HEADS)
    cos, sin = cq_ref[...], sq_ref[...]
    qr = []
    for c in range(A_HEADS // 2):
        slab = _rope128(jnp.concatenate(qn[2 * c:2 * c + 2], axis=-1), cos, sin)
        qr.append(slab[:, :HD])
        qr.append(slab[:, HD:])
    ck = ck_ref[...].astype(BF16)
    cv = cv_ref[...].astype(BF16)
    kk = k_s[...]
    vv = v_s[...]
    group = A_HEADS // A_KV_HEADS
    outs = []
    for h in range(A_HEADS):
        g = h // group
        sl = slice(g * HD, (g + 1) * HD)
        qh = (qr[h] * ATT_SCALE).astype(BF16)
        outs.append(_softmax_pv([_qk(qh, kk[:, sl]), _qk(qh, ck[:, sl])], [vv[:, sl], cv[:, sl]]))
    o_ref[...] = jnp.concatenate(outs, axis=-1).astype(BF16)


def attn_a_lat(z, ck, cv, gq, gk, e, y_prev):
    tq = A_LAT_TQ
    nq = T_LAT // tq
    cos, sin = _rope_tables()
    cos, sin = jnp.asarray(cos), jnp.asarray(sin)
    row0 = ROWS_CTX // tq
    seq0 = ROWS_CTX // T_LAT
    cache_spec = pl.BlockSpec((None, None, PAST, A_KV_W), lambda b, i: (b, e, 0, 0))
    return pl.pallas_call(
        _attn_a_lat_kernel,
        grid=(N_LAT, nq),
        in_specs=[pl.BlockSpec((tq, A_W), lambda b, i: (row0 + b * nq + i, 0)),
                  pl.BlockSpec((T_LAT, 2 * A_KV_W), lambda b, i: (seq0 + b, A_W // (2 * A_KV_W))),
                  cache_spec, cache_spec,
                  pl.BlockSpec((None, 1, HD), lambda b, i: (e, 0, 0)),
                  pl.BlockSpec((None, 1, HD), lambda b, i: (e, 0, 0)),
                  pl.BlockSpec((tq, LANES), lambda b, i: (i, 0)),
                  pl.BlockSpec((tq, LANES), lambda b, i: (i, 0)),
                  pl.BlockSpec((T_LAT, LANES), lambda b, i: (0, 0)),
                  pl.BlockSpec((T_LAT, LANES), lambda b, i: (0, 0)),
                  _untouched()],
        out_specs=pl.BlockSpec((tq, A_W), lambda b, i: (row0 + b * nq + i, 0)),
        out_shape=jax.ShapeDtypeStruct((ROWS, A_W), BF16),
        scratch_shapes=[pltpu.VMEM((T_LAT, A_KV_W), BF16),
                        pltpu.VMEM((T_LAT, A_KV_W), BF16)],
        input_output_aliases={10: 0},
        compiler_params=_cparams(("arbitrary", "arbitrary")),
        name="attn_a_lat",
    )(z, z, ck, cv, gq, gk, cos, sin, cos, sin, y_prev)


def _attn_c_ctx_kernel(q_ref, k_ref, v_ref, *rest):
    o_ref, ko_ref, vo_ref = rest[-3:]
    q = q_ref[...]
    k = k_ref[...]
    v = v_ref[...]
    ko_ref[...] = k
    vo_ref[...] = v
    outs = []
    for h in range(LANES // HD):
        sl = slice(h * HD, (h + 1) * HD)
        qh = (q[:, sl] * ATT_SCALE).astype(BF16)
        outs.append(_softmax_pv([_qk(qh, k[:, sl].astype(BF16))], [v[:, sl].astype(BF16)]))
    o_ref[...] = jnp.concatenate(outs, axis=-1).astype(BF16)


def attn_c_ctx(z, o, n_odd, prev_caches):
    t = T_CTX
    nhp = C_W // LANES
    cache_spec = pl.BlockSpec((None, None, t, LANES), lambda b, p: (b, o, 0, p))
    cache_shape = jax.ShapeDtypeStruct((N_CTX, n_odd, t, C_W), F32)
    extra = [] if prev_caches is None else list(prev_caches)
    aliases = {} if prev_caches is None else {3: 1, 4: 2}
    return pl.pallas_call(
        _attn_c_ctx_kernel,
        grid=(N_CTX, nhp),
        in_specs=[pl.BlockSpec((t, LANES), lambda b, p: (b, p)),
                  pl.BlockSpec((t, LANES), lambda b, p: (b, nhp + p)),
                  pl.BlockSpec((t, LANES), lambda b, p: (b, 2 * nhp + p))] + [_untouched() for _ in extra],
        out_specs=[pl.BlockSpec((t, LANES), lambda b, p: (b, p)), cache_spec, cache_spec],
        out_shape=[jax.ShapeDtypeStruct((ROWS, C_W), BF16), cache_shape, cache_shape],
        input_output_aliases=aliases,
        compiler_params=_cparams(("arbitrary", "arbitrary")),
        name="attn_c_ctx",
    )(z, z, z, *extra)


NA_GRID_ROWS = T_LAT // GRID_W
NA_WIN = NA_ROWS * GRID_W
NA_DR = 2 * NA_ROWS - 1
NA_DC = 2 * NA_COLS - 1


def _na_row_start(r):
    return min(max(r - NA_ROWS // 2, 0), NA_GRID_ROWS - NA_ROWS)


def _na_bias_table(rpb):
    n_l = rpb.shape[0]
    col = np.arange(GRID_W)
    cs = np.clip(col - NA_COLS // 2, 0, GRID_W - NA_COLS)
    col_in = (col[None, :] >= cs[:, None]) & (col[None, :] < cs[:, None] + NA_COLS)
    period = GRID_W + 1
    seq = jnp.concatenate([rpb, jnp.zeros((n_l, C_HEADS, NA_DR, period - NA_DC), F32)], axis=-1)
    seq = jnp.roll(seq, -(NA_COLS - 1), axis=-1)
    tile = jnp.tile(seq, (1, 1, 1, GRID_W))[..., :GRID_W * GRID_W].reshape(n_l, C_HEADS, NA_DR, GRID_W, GRID_W)
    tile = jnp.where(jnp.asarray(col_in), tile, NEG_INF)
    return jnp.concatenate([tile[:, :, :-1], tile[:, :, 1:]], axis=-1)


def _attn_na_kernel(q_ref, k_ref, v_ref, ck_ref, cv_ref, bias_ref, prev_ref, o_ref):
    del prev_ref
    heads = []
    for h in range(LANES // HD):
        sl = slice(h * HD, (h + 1) * HD)
        q = (q_ref[:, sl] * ATT_SCALE).astype(BF16)
        k = k_ref[:, sl].astype(BF16)
        v = v_ref[:, sl].astype(BF16)
        ck = ck_ref[:, sl].astype(BF16)
        cv = cv_ref[:, sl].astype(BF16)
        rows = []
        for r in range(NA_GRID_ROWS):
            qr = q[r * GRID_W:(r + 1) * GRID_W]
            rs = _na_row_start(r)
            kw = k[rs * GRID_W:rs * GRID_W + NA_WIN]
            vw = v[rs * GRID_W:rs * GRID_W + NA_WIN]
            dr0 = rs - r + (NA_ROWS - 1)
            bias = jnp.concatenate([bias_ref[h, dr0 + 2 * i] for i in range(NA_ROWS // 2)], axis=-1)
            s_nb = _qk(qr, kw) + bias
            s_cx = _qk(qr, ck)
            rows.append(_softmax_pv([s_nb, s_cx], [vw, cv]))
        heads.append(jnp.concatenate(rows, axis=0))
    o_ref[...] = jnp.concatenate(heads, axis=-1).astype(BF16)


def attn_na(z, ck, cv, bias, o, y_prev):
    nhp = C_W // LANES
    hpb = LANES // HD
    seq0 = ROWS_CTX // T_LAT
    cache_spec = pl.BlockSpec((None, None, PAST, LANES), lambda p, b: (b, o, 0, p))
    return pl.pallas_call(
        _attn_na_kernel,
        grid=(nhp, N_LAT),
        in_specs=[pl.BlockSpec((T_LAT, LANES), lambda p, b: (seq0 + b, p)),
                  pl.BlockSpec((T_LAT, LANES), lambda p, b: (seq0 + b, nhp + p)),
                  pl.BlockSpec((T_LAT, LANES), lambda p, b: (seq0 + b, 2 * nhp + p)),
                  cache_spec, cache_spec,
                  pl.BlockSpec((None, hpb, NA_DR - 1, GRID_W, LANES), lambda p, b: (o, p, 0, 0, 0)),
                  _untouched()],
        out_specs=pl.BlockSpec((T_LAT, LANES), lambda p, b: (seq0 + b, p)),
        out_shape=jax.ShapeDtypeStruct((ROWS, C_W), BF16),
        input_output_aliases={6: 0},
        compiler_params=_cparams(("arbitrary", "arbitrary")),
        name="attn_na",
    )(z, z, z, ck, cv, bias, y_prev)


def _seg_sum(x, n_heads):
    parts = []
    for h in range(n_heads):
        s = jnp.sum(x[:, h * HD:(h + 1) * HD], axis=-1, keepdims=True)
        parts.append(jnp.broadcast_to(s, (x.shape[0], HD)))
    return jnp.concatenate(parts, axis=-1)


PREP_TM = 256


def _seq_len_at(row_start):
    return jnp.where(row_start < ROWS_CTX, T_CTX, T_LAT)


def _rwkv_prep_kernel(z_ref, zprev_ref, znext_ref, mu_ref, kkw_ref, w0_ref, w2_ref, a0_ref, a2_ref, ka_ref,
                      rk_ref, g2_ref, o_rk_ref, o_wk_ref, o_bv_ref, g_ref, bonus_ref):
    z = z_ref[:, A_IN:]
    t = z.shape[0]
    start = pl.program_id(0) * t
    seq_len = _seq_len_at(start)
    pos = (start - jnp.where(start < ROWS_CTX, 0, ROWS_CTX)) % seq_len
    halo_prev = jnp.where(pos == 0, 0.0, zprev_ref[SUBLANES - 1:SUBLANES, A_IN:])
    halo_next = jnp.where(pos + t == seq_len, 0.0, znext_ref[0:1, A_IN:])
    row = lax.broadcasted_iota(jnp.int32, (t, 1), 0)
    prev = jnp.where(row == 0, halo_prev, pltpu.roll(z, 1, 0))
    nxt = jnp.where(row == t - 1, halo_next, pltpu.roll(z, t - 1, 0))
    m = z + mu_ref[...] * (0.5 * (prev + nxt) - z)
    r = m[:, :B_W]
    k = m[:, B_W:2 * B_W]
    v = m[:, 2 * B_W:3 * B_W]
    o = 3 * B_W
    wd = m[:, o:o + 2 * LORA_W]
    ad = m[:, o + 2 * LORA_W:o + 2 * LORA_W + 2 * LORA_A]
    gd = m[:, o + 2 * LORA_W + 2 * LORA_A:]

    kkr = k * kkw_ref[...]
    kk = kkr * lax.rsqrt(_seg_sum(kkr * kkr, B_HEADS) + 1e-12)

    wl = w0_ref[...] + jnp.dot(jnp.tanh(wd), w2_ref[...], precision=HIGHEST, preferred_element_type=F32)
    decay = jnp.exp(-float(np.exp(-0.5)) * jax.nn.sigmoid(wl))
    a = jax.nn.sigmoid(a0_ref[...] + jnp.dot(ad, a2_ref[...], precision=HIGHEST, preferred_element_type=F32))
    k2 = jnp.concatenate([k, k], axis=-1)
    ka2 = jnp.concatenate([ka_ref[...], ka_ref[...]], axis=-1)
    kk2 = jnp.concatenate([kk, kk], axis=-1)

    kd = k2 * (1.0 + (a - 1.0) * ka2)
    bb = kk2 * a
    for h in range(B_HEADS):
        sl = slice(h * HD, (h + 1) * HD)
        o_rk_ref[pl.ds(h, t, stride=B_HEADS), :] = jnp.concatenate([r[:, sl], kk[:, sl]], axis=-1)
        for dr in range(2):
            sd = slice(dr * B_W + h * HD, dr * B_W + (h + 1) * HD)
            row = pl.ds(dr * B_HEADS + h, t, stride=2 * B_HEADS)
            o_wk_ref[row, :] = jnp.concatenate([decay[:, sd], kd[:, sd]], axis=-1)
            o_bv_ref[row, :] = jnp.concatenate([bb[:, sd], v[:, sl]], axis=-1)
    g_ref[...] = jnp.dot(jax.nn.sigmoid(gd).astype(BF16), g2_ref[...].astype(BF16), preferred_element_type=F32)
    bonus_ref[...] = _seg_sum(r * k * rk_ref[...], B_HEADS) * v


def rwkv_prep(z, params):
    tm = PREP_TM
    full = lambda a: pl.BlockSpec(a.shape, lambda s: (0,) * a.ndim)
    out_rows = (B_HEADS, 2 * B_HEADS, 2 * B_HEADS)
    per = tm // SUBLANES
    last = ROWS // SUBLANES - 1
    rowspec = pl.BlockSpec((tm, B_W), lambda s: (s, 0))
    return pl.pallas_call(
        _rwkv_prep_kernel,
        grid=(ROWS // tm,),
        in_specs=[pl.BlockSpec((tm, EVEN_IN), lambda s: (s, 0)),
                  pl.BlockSpec((SUBLANES, EVEN_IN), lambda s: (jnp.maximum(s * per - 1, 0), 0)),
                  pl.BlockSpec((SUBLANES, EVEN_IN), lambda s: (jnp.minimum((s + 1) * per, last), 0))]
                 + [full(p) for p in params],
        out_specs=[pl.BlockSpec((tm * n, LANES), lambda s: (s, 0)) for n in out_rows] + [rowspec, rowspec],
        out_shape=[jax.ShapeDtypeStruct((ROWS * n, LANES), F32) for n in out_rows]
                  + [jax.ShapeDtypeStruct((ROWS, B_W), F32)] * 2,
        compiler_params=_cparams(("arbitrary",)),
        name="rwkv_prep",
    )(z, z, z, *params)


def rwkv_prep_all(z, prm, e):
    w2 = prm['b_w2'][e]
    a2 = prm['b_a2'][e]
    zero = jnp.zeros((LORA_W, B_W), F32)
    w2bd = jnp.concatenate([jnp.concatenate([w2[0], zero], 1), jnp.concatenate([zero, w2[1]], 1)], 0)
    a2bd = jnp.concatenate([jnp.concatenate([a2[0], zero], 1), jnp.concatenate([zero, a2[1]], 1)], 0)
    params = (prm['b_mu'][e][None], prm['b_kk'][e][None], prm['b_w0'][e].reshape(1, 2 * B_W), w2bd,
              prm['b_a0'][e].reshape(1, 2 * B_W), a2bd, prm['b_ka'][e][None],
              prm['b_rk'][e].reshape(1, B_W), prm['b_g2'][e])
    return rwkv_prep(z, params)


SCAN_CHAINS = 8


def _tree_sum(xs):
    while len(xs) > 1:
        xs = [xs[i] + xs[i + 1] for i in range(0, len(xs) - 1, 2)] + ([xs[-1]] if len(xs) % 2 else [])
    return xs[0]


def _scan_step(rk_b, wk_b, bv_b, vv, s_ref, ni):
    n_acc = max(1, SCAN_CHAINS // ni)

    def bcast(ref, row):
        return jnp.broadcast_to(ref[pl.ds(row, 1), :], (SUBLANES, LANES))

    sa = [[None] * n_acc for _ in range(ni)]
    for j in range(HD):
        kkj = bcast(rk_b, HD + j)
        for g in range(ni):
            p = s_ref[j, pl.ds(g * SUBLANES, SUBLANES), :] * kkj
            a = j % n_acc
            sa[g][a] = p if sa[g][a] is None else sa[g][a] + p
    sa = [-_tree_sum(x) for x in sa]
    ya = [[None] * n_acc for _ in range(ni)]
    for j in range(HD):
        wj = bcast(wk_b, j)
        kj = bcast(wk_b, HD + j)
        bj = bcast(bv_b, j)
        rj = bcast(rk_b, j)
        for g in range(ni):
            sl = pl.ds(g * SUBLANES, SUBLANES)
            s_new = s_ref[j, sl, :] * wj + sa[g] * bj + vv[g] * kj
            s_ref[j, sl, :] = s_new
            p = s_new * rj
            a = j % n_acc
            ya[g][a] = p if ya[g][a] is None else ya[g][a] + p
    return [_tree_sum(ya[g]) for g in range(ni)]


def _scan_pairs(tc, relayout, run):
    relayout(0, 0)

    def pair(k, carry):
        i0 = 2 * k
        relayout(i0 + 1, 1)
        run(i0, 0)
        relayout(jnp.minimum(i0 + 2, tc - 1), 0)
        run(i0 + 1, 1)
        return carry

    lax.fori_loop(0, tc // 2, pair, 0)


def _scan_ctx_kernel(rk_ref, wk_ref, bv_ref, y_ref, st_ref, rk0, wk0, bv0, rk1, wk1, bv1, s_ref, *, tc):
    d = pl.program_id(0)
    c = pl.program_id(1)
    ni = HD // SUBLANES
    bufs = ((rk0, wk0, bv0), (rk1, wk1, bv1))

    @pl.when(c == 0)
    def _():
        s_ref[...] = jnp.zeros_like(s_ref)

    def t_of(i):
        return i + d * (tc - 1 - 2 * i)

    def relayout(i, slot):
        t = t_of(i)
        for src, dst in zip((rk_ref, wk_ref, bv_ref), bufs[slot]):
            dst[...] = src[:, :, t].reshape(LANES, LANES).T

    def run(i, slot):
        rk_b, wk_b, bv_b = bufs[slot]
        vv = [bv_b[pl.ds(HD + g * SUBLANES, SUBLANES), :] for g in range(ni)]
        y = jnp.concatenate(_scan_step(rk_b, wk_b, bv_b, vv, s_ref, ni), axis=0)
        y_ref[:, :, t_of(i)] = y.T.reshape(CTX_GROUPS, ROW_SUB, B_HEADS, HD)

    _scan_pairs(tc, relayout, run)

    @pl.when(c == pl.num_programs(1) - 1)
    def _():
        st_ref[...] = s_ref[...]


def _rows5(x, heads):
    return x.reshape(ROW_GROUPS, ROW_SUB, T_CTX, heads, x.shape[-1])


def _operand_bufs():
    return [pltpu.VMEM((LANES, LANES), F32)] * 6


def rwkv_scan_ctx(prep, tc):
    rk, wk, bv = prep[:3]
    nc = T_CTX // tc

    def chunk(d, c):
        return c + d * (nc - 1 - 2 * c)

    blk = (CTX_GROUPS, ROW_SUB, tc, B_HEADS, LANES)
    shared = pl.BlockSpec(blk, lambda d, c: (0, 0, chunk(d, c), 0, 0))
    perdir = pl.BlockSpec(blk, lambda d, c: (0, 0, chunk(d, c), d, 0))
    yspec = pl.BlockSpec((CTX_GROUPS, ROW_SUB, tc, B_HEADS, HD), lambda d, c: (0, 0, chunk(d, c), d, 0))
    return pl.pallas_call(
        functools.partial(_scan_ctx_kernel, tc=tc),
        grid=(2, nc),
        in_specs=[shared, perdir, perdir],
        out_specs=[yspec, pl.BlockSpec((None, HD, HD, LANES), lambda d, c: (d, 0, 0, 0))],
        out_shape=[jax.ShapeDtypeStruct((ROW_GROUPS, ROW_SUB, T_CTX, 2 * B_HEADS, HD), F32),
                   jax.ShapeDtypeStruct((2, HD, HD, LANES), F32)],
        scratch_shapes=_operand_bufs() + [pltpu.VMEM((HD, HD, LANES), F32)],
        compiler_params=_cparams(("arbitrary", "arbitrary")),
        name="rwkv_scan_ctx",
    )(_rows5(rk, B_HEADS), _rows5(wk, 2 * B_HEADS), _rows5(bv, 2 * B_HEADS))


LAT_REP = LANES // (N_LAT * B_HEADS)


def _scan_lat_kernel(rk_ref, wk_ref, bv_ref, s0_ref, yprev_ref, y_ref, rk0, wk0, bv0, rk1, wk1, bv1, s_ref,
                     y_s, *, tc):
    del yprev_ref
    d = pl.program_id(0)
    c = pl.program_id(1)
    n = N_LAT * B_HEADS
    bufs = ((rk0, wk0, bv0), (rk1, wk1, bv1))

    @pl.when(c == 0)
    def _():
        s_ref[...] = s0_ref[...]

    def t_of(i):
        return i + d * (tc - 1 - 2 * i)

    def relayout(i, slot):
        t = t_of(i)
        for src, dst in zip((rk_ref, wk_ref, bv_ref), bufs[slot]):
            m = src[:, 0, t].reshape(n, LANES)
            dst[...] = jnp.concatenate([m] * LAT_REP, axis=0).T

    def run(i, slot):
        rk_b, wk_b, bv_b = bufs[slot]
        group = lax.broadcasted_iota(jnp.int32, (SUBLANES, LANES), 1) // n
        v = bv_b[pl.ds(HD, SUBLANES), :]
        for g in range(1, LAT_REP):
            v = jnp.where(group == g, bv_b[pl.ds(HD + g * SUBLANES, SUBLANES), :], v)
        (y_s[t_of(i)],) = _scan_step(rk_b, wk_b, bv_b, [v], s_ref, 1)

    _scan_pairs(tc, relayout, run)

    def store(t, carry):
        y = y_s[t]
        rows = [y] + [pltpu.roll(y, LANES - g * n, 1) for g in range(1, LAT_REP)]
        yt = jnp.concatenate(rows, axis=0).T
        y_ref[:, 0, t] = yt[:n].reshape(N_LAT, B_HEADS, HD)
        return carry

    lax.fori_loop(0, tc, store, 0, unroll=8)


def rwkv_scan_lat(prep, s0f, s0b, y_prev, tc):
    rk, wk, bv = prep[:3]
    nc = T_LAT // tc
    per_sub = T_CTX // tc
    g0 = CTX_GROUPS // N_LAT

    def pos(d, c):
        cc = c + d * (nc - 1 - 2 * c)
        return cc // per_sub, cc % per_sub

    def shared_map(d, c):
        q, off = pos(d, c)
        return (g0, q, off, 0, 0)

    def perdir_map(d, c):
        q, off = pos(d, c)
        return (g0, q, off, d, 0)

    blk = (N_LAT, 1, tc, B_HEADS, LANES)
    s0 = jnp.stack([s0f, s0b]).reshape(2, N_LAT, B_HEADS, LAT_REP, SUBLANES, HD)
    s0 = s0.transpose(0, 5, 4, 3, 1, 2).reshape(2, HD, SUBLANES, LANES)
    return pl.pallas_call(
        functools.partial(_scan_lat_kernel, tc=tc),
        grid=(2, nc),
        in_specs=[pl.BlockSpec(blk, shared_map), pl.BlockSpec(blk, perdir_map), pl.BlockSpec(blk, perdir_map),
                  pl.BlockSpec((None, HD, SUBLANES, LANES), lambda d, c: (d, 0, 0, 0)),
                  _untouched()],
        out_specs=pl.BlockSpec((N_LAT, 1, tc, B_HEADS, HD), perdir_map),
        out_shape=jax.ShapeDtypeStruct((ROW_GROUPS, ROW_SUB, T_CTX, 2 * B_HEADS, HD), F32),
        scratch_shapes=_operand_bufs() + [pltpu.VMEM((HD, SUBLANES, LANES), F32),
                                          pltpu.VMEM((tc, SUBLANES, LANES), F32)],
        input_output_aliases={4: 0},
        compiler_params=_cparams(("arbitrary", "arbitrary")),
        name="rwkv_scan_lat",
    )(_rows5(rk, B_HEADS), _rows5(wk, 2 * B_HEADS), _rows5(bv, 2 * B_HEADS), s0, y_prev)


def _rwkv_post_kernel(y_ref, bonus_ref, g_ref, lng_ref, lnb_ref, o_ref):
    tm = o_ref.shape[0]
    outs = []
    for h in range(B_HEADS):
        sl = slice(h * HD, (h + 1) * HD)
        y = (y_ref[pl.ds(h, tm, stride=2 * B_HEADS), :]
             + y_ref[pl.ds(B_HEADS + h, tm, stride=2 * B_HEADS), :])
        yc = y - jnp.mean(y, axis=-1, keepdims=True)
        var = jnp.mean(yc * yc, axis=-1, keepdims=True)
        yn = yc * lax.rsqrt(var + GN_EPS) * lng_ref[:, sl] + lnb_ref[:, sl]
        outs.append((yn + bonus_ref[:, sl]) * g_ref[:, sl])
    o_ref[...] = jnp.concatenate(outs, axis=-1).astype(BF16)


def rwkv_post(y, bonus, g, lng, lnb, e):
    tm = 256
    rowspec = pl.BlockSpec((tm, B_W), lambda i: (i, 0))
    vecspec = pl.BlockSpec((None, 1, B_W), lambda i: (e, 0, 0))
    return pl.pallas_call(
        _rwkv_post_kernel,
        grid=(ROWS // tm,),
        in_specs=[pl.BlockSpec((tm * 2 * B_HEADS, HD), lambda i: (i, 0)), rowspec, rowspec, vecspec, vecspec],
        out_specs=rowspec,
        out_shape=jax.ShapeDtypeStruct((ROWS, B_W), BF16),
        compiler_params=_cparams(("arbitrary",)),
        name="rwkv_post",
    )(y, bonus, g, lng, lnb)


def kernel(x_prompt, x_sample, cache_a_k, cache_a_v, state_b_fwd, state_b_bwd, cache_c_k, cache_c_v, c, c_ctx,
           ada_w, ada_b, norm1_g, norm2_g, final_norm_g, w_in_e, w_out_e, a_q_gain, a_k_gain, b_mu, b_w0, b_w2,
           b_a0, b_a2, b_g2, b_kk, b_ka, b_rk, b_ln_g, b_ln_b, ffn_w_gu, ffn_w_dn, w_in_o, w_out_o, c_rpb,
           router_w, router_b, moe_w_gu, moe_w_dn):
    n_even, n_odd = w_in_e.shape[0], w_in_o.shape[0]
    prm = dict(b_mu=b_mu, b_w0=b_w0, b_w2=b_w2, b_a0=b_a0, b_a2=b_a2, b_g2=b_g2, b_kk=b_kk, b_ka=b_ka,
               b_rk=b_rk)
    cond = jnp.zeros((N_MODS, D), F32).at[0].set(c_ctx).at[1:1 + N_LAT].set(c)
    mods = ada_all(cond, ada_w, ada_b).reshape(DEPTH, N_MODS, 6, D)
    g1 = norm1_g.reshape(DEPTH, 1, D)
    g2 = norm2_g.reshape(DEPTH, 1, D)
    gq = a_q_gain.reshape(n_even, 1, HD)
    gk = a_k_gain.reshape(n_even, 1, HD)
    lng = b_ln_g.reshape(n_even, 1, B_W)
    lnb = b_ln_b.reshape(n_even, 1, B_W)
    ck_a = cache_a_k.reshape(N_LAT, n_even, PAST, A_KV_W)
    cv_a = cache_a_v.reshape(N_LAT, n_even, PAST, A_KV_W)
    ck_c = cache_c_k.reshape(N_LAT, n_odd, PAST, C_W)
    cv_c = cache_c_v.reshape(N_LAT, n_odd, PAST, C_W)
    na_bias = _na_bias_table(c_rpb)
    rw_p = jnp.zeros((n_odd, D, LANES), F32).at[:, :, :N_EXPERTS].set(router_w)
    rb_p = jnp.zeros((n_odd, 1, LANES), F32).at[:, 0, :N_EXPERTS].set(router_b)

    x, h = first_norm(x_prompt, x_sample, g1, mods)

    a_caches, c_caches = None, None
    new_sf, new_sb = [], []
    y_final = None
    for l in range(DEPTH):
        if l % 2 == 0:
            e = l // 2
            z = mm_in(h, w_in_e, e, tn=EVEN_IN // 3)
            y_a, k_new, v_new = attn_a_ctx(z, gq, gk, e, a_caches)
            a_caches = (k_new, v_new)
            y_a = attn_a_lat(z, ck_a, cv_a, gq, gk, e, y_a)
            prep = rwkv_prep_all(z, prm, e)
            y_s, st = rwkv_scan_ctx(prep, tc=32)
            y_s = rwkv_scan_lat(prep, state_b_fwd[:, e], state_b_bwd[:, e], y_s, tc=64)
            st = st.reshape(2, HD, HD, N_CTX, B_HEADS).transpose(0, 3, 4, 2, 1)
            new_sf.append(st[0])
            new_sb.append(st[1])
            y_b = rwkv_post(y_s.reshape(ROWS * 2 * B_HEADS, HD), prep[4], prep[3], lng, lnb, e)
            x, h = mm_out([y_a, y_b], w_out_e, e, x, mods, l, g2, l)
            x, h = ffn(h, ffn_w_gu, ffn_w_dn, e, x, mods, l, g1)
        else:
            o = l // 2
            z = mm_in(h, w_in_o, o, tn=C_W)
            y, k_new, v_new = attn_c_ctx(z, o, n_odd, c_caches)
            c_caches = (k_new, v_new)
            y = attn_na(z, ck_c, cv_c, na_bias, o, y)
            x, h = mm_out([y], w_out_o, o, x, mods, l, g2, l)
            gates = router(x, g2, mods, l, rw_p, rb_p, o)
            y_moe = moe_dense(h, gates, moe_w_gu, moe_w_dn, o)
            if l + 1 < DEPTH:
                x, h = resid(x, y_moe, mods, l, g1, False)
            else:
                (y_final,) = resid(x, y_moe, mods, l, final_norm_g[None], True)

    y_prompt = y_final[:ROWS_CTX].reshape(N_CTX, T_CTX, D)
    y_sample = y_final[ROWS_CTX:].reshape(N_LAT, T_LAT, D)
    return (y_prompt, y_sample,
            a_caches[0].reshape(N_CTX, n_even, T_CTX, A_KV_HEADS, HD),
            a_caches[1].reshape(N_CTX, n_even, T_CTX, A_KV_HEADS, HD),
            jnp.stack(new_sf, axis=1), jnp.stack(new_sb, axis=1),
            c_caches[0].reshape(N_CTX, n_odd, T_CTX, C_HEADS, HD),
            c_caches[1].reshape(N_CTX, n_odd, T_CTX, C_HEADS, HD))
```

```python
import functools

import numpy as np
import jax
import jax.numpy as jnp
from jax import lax
from jax.experimental import pallas as pl
from jax.experimental.pallas import tpu as pltpu

F32 = jnp.float32
BF16 = jnp.bfloat16
HIGHEST = lax.Precision.HIGHEST

D = 1024
N_CTX, T_CTX = 16, 256
N_LAT, T_LAT = 2, 1024
ROWS_CTX = N_CTX * T_CTX
ROWS_LAT = N_LAT * T_LAT
ROWS = ROWS_CTX + ROWS_LAT
DEPTH = 4
GRID_W = 64
HD = 64
A_HEADS, A_KV_HEADS, B_HEADS, C_HEADS = 8, 2, 8, 16
A_W, A_KV_W, B_W, C_W = A_HEADS * HD, A_KV_HEADS * HD, B_HEADS * HD, C_HEADS * HD
A_IN = A_W + 2 * A_KV_W
LORA_W, LORA_A, LORA_G = 64, 64, 128
B_IN = 3 * B_W + 2 * LORA_W + 2 * LORA_A + LORA_G
EVEN_IN = A_IN + B_IN
PAST = 512
NA_ROWS, NA_COLS = 8, 16
D_FF = 2816
N_EXPERTS = 8
D_EXPERT = 1408
ROPE_THETA = 10000.0
EPS = 1e-6
GN_EPS = 64e-5
NEG_INF = -1e30
ATT_SCALE = HD ** -0.5

LANES = 128
SUBLANES = 8
VMEM_LIMIT = 56 * 1024 * 1024

N_MODS = 8

ROW_SUB = T_LAT // T_CTX
ROW_GROUPS = ROWS // (ROW_SUB * T_CTX)
CTX_GROUPS = N_CTX // ROW_SUB


def _cparams(sem):
    return pltpu.CompilerParams(dimension_semantics=sem, vmem_limit_bytes=VMEM_LIMIT)


def _untouched():
    return pl.BlockSpec(memory_space=pl.ANY)


def _mod_index(row_start):
    return jnp.where(row_start < ROWS_CTX, 0, 1 + (row_start - ROWS_CTX) // T_LAT)


def _modspec(layer, tm, nargs=1):
    if nargs == 1:
        return pl.BlockSpec((None, 1, 6, D), lambda i: (layer, _mod_index(i * tm), 0, 0))
    return pl.BlockSpec((None, 1, 6, D), lambda i, j: (layer, _mod_index(i * tm), 0, 0))


def _gainspec(layer, nargs=1):
    if nargs == 1:
        return pl.BlockSpec((None, 1, D), lambda i: (layer, 0, 0))
    return pl.BlockSpec((None, 1, D), lambda i, j: (layer, 0, 0))


def _modnorm(x, g, shift, scale):
    ms = jnp.mean(x * x, axis=-1, keepdims=True)
    return (x * lax.rsqrt(ms + EPS) * g) * (1.0 + scale) + shift


def _silu(x):
    return x * jax.nn.sigmoid(x)


def _ada_kernel(c_ref, w_ref, b_ref, o_ref):
    s = _silu(c_ref[...]).astype(BF16)
    o_ref[0] = jnp.dot(s, w_ref[0].astype(BF16), preferred_element_type=F32) + b_ref[0]


def ada_all(cond, ada_w, ada_b):
    tn = 1536
    n = 6 * D
    return pl.pallas_call(
        _ada_kernel,
        grid=(DEPTH, n // tn),
        in_specs=[pl.BlockSpec((N_MODS, D), lambda l, j: (0, 0)),
                  pl.BlockSpec((1, D, tn), lambda l, j: (l, 0, j)),
                  pl.BlockSpec((1, 1, tn), lambda l, j: (l, 0, j))],
        out_specs=pl.BlockSpec((1, N_MODS, tn), lambda l, j: (l, 0, j)),
        out_shape=jax.ShapeDtypeStruct((DEPTH, N_MODS, n), F32),
        compiler_params=_cparams(("arbitrary", "arbitrary")),
        name="ada",
    )(cond, ada_w, ada_b.reshape(DEPTH, 1, n))


def _first_norm_kernel(xp_ref, xs_ref, g_ref, mod_ref, x_ref, h_ref, *, n_ctx_blocks):
    i = pl.program_id(0)

    def emit(src):
        x = src[...]
        x_ref[...] = x
        h_ref[...] = _modnorm(x, g_ref[...], mod_ref[0, 0:1, :], mod_ref[0, 1:2, :]).astype(BF16)

    @pl.when(i < n_ctx_blocks)
    def _():
        emit(xp_ref)

    @pl.when(i >= n_ctx_blocks)
    def _():
        emit(xs_ref)


def first_norm(x_prompt, x_sample, gains, mods):
    tm = 512
    nc = ROWS_CTX // tm
    rowspec = pl.BlockSpec((tm, D), lambda i: (i, 0))
    return pl.pallas_call(
        functools.partial(_first_norm_kernel, n_ctx_blocks=nc),
        grid=(ROWS // tm,),
        in_specs=[pl.BlockSpec((tm, D), lambda i: (jnp.minimum(i, nc - 1), 0)),
                  pl.BlockSpec((tm, D), lambda i: (jnp.maximum(i - nc, 0), 0)),
                  _gainspec(0), _modspec(0, tm)],
        out_specs=[rowspec, rowspec],
        out_shape=[jax.ShapeDtypeStruct((ROWS, D), F32), jax.ShapeDtypeStruct((ROWS, D), BF16)],
        compiler_params=_cparams(("arbitrary",)),
        name="first_norm",
    )(x_prompt.reshape(ROWS_CTX, D), x_sample.reshape(ROWS_LAT, D), gains, mods)


def _mm_in_kernel(h_ref, w_ref, o_ref, wb_ref):
    @pl.when(pl.program_id(1) == 0)
    def _():
        wb_ref[...] = w_ref[...].astype(BF16)

    o_ref[...] = jnp.dot(h_ref[...], wb_ref[...], preferred_element_type=F32)


def mm_in(h, w, layer, tn, tm=1024):
    _, k, n = w.shape
    return pl.pallas_call(
        _mm_in_kernel,
        grid=(n // tn, ROWS // tm),
        in_specs=[pl.BlockSpec((tm, k), lambda j, i: (i, 0)),
                  pl.BlockSpec((None, k, tn), lambda j, i: (layer, 0, j))],
        out_specs=pl.BlockSpec((tm, tn), lambda j, i: (i, j)),
        out_shape=jax.ShapeDtypeStruct((ROWS, n), F32),
        scratch_shapes=[pltpu.VMEM((k, tn), BF16)],
        compiler_params=_cparams(("arbitrary", "arbitrary")),
        name="mm_in",
    )(h, w)


def _residual_epilogue(x, acc, mod_ref, nmod_ref, g_ref, xo_ref, ho_ref, gate_idx, shift_idx, scale_idx, final):
    gate = mod_ref[0, gate_idx:gate_idx + 1, :]
    xn = x + gate * acc
    if final:
        ms = jnp.mean(xn * xn, axis=-1, keepdims=True)
        ho_ref[...] = xn * lax.rsqrt(ms + EPS) * g_ref[...]
    else:
        xo_ref[...] = xn
        shift = nmod_ref[0, shift_idx:shift_idx + 1, :]
        scale = nmod_ref[0, scale_idx:scale_idx + 1, :]
        ho_ref[...] = _modnorm(xn, g_ref[...], shift, scale).astype(BF16)


def _mm_out_kernel(*refs, n_parts, gate_idx, shift_idx, scale_idx):
    y_refs = refs[:n_parts]
    w_refs = refs[n_parts:2 * n_parts]
    x_ref, mod_ref, nmod_ref, g_ref, xo_ref, ho_ref = refs[2 * n_parts:2 * n_parts + 6]
    wb_refs = refs[2 * n_parts + 6:]

    @pl.when(pl.program_id(0) == 0)
    def _():
        for w_ref, wb_ref in zip(w_refs, wb_refs):
            wb_ref[...] = w_ref[...].astype(BF16)

    acc = None
    for y_ref, wb_ref in zip(y_refs, wb_refs):
        p = jnp.dot(y_ref[...], wb_ref[...], preferred_element_type=F32)
        acc = p if acc is None else acc + p
    _residual_epilogue(x_ref[...], acc, mod_ref, nmod_ref, g_ref, xo_ref, ho_ref,
                       gate_idx, shift_idx, scale_idx, False)


def mm_out(parts, w, layer_w, x, mods, layer, gains, gain_layer):
    tm = 512
    kp = parts[0].shape[1]
    n_parts = len(parts)
    rowspec = pl.BlockSpec((tm, D), lambda i: (i, 0))
    return pl.pallas_call(
        functools.partial(_mm_out_kernel, n_parts=n_parts, gate_idx=2, shift_idx=3, scale_idx=4),
        grid=(ROWS // tm,),
        in_specs=[pl.BlockSpec((tm, kp), lambda i: (i, 0)) for _ in parts]
                 + [pl.BlockSpec((None, kp, D), lambda i, p=p: (layer_w, p, 0)) for p in range(n_parts)]
                 + [rowspec, _modspec(layer, tm), _modspec(layer, tm), _gainspec(gain_layer)],
        out_specs=[rowspec, rowspec],
        out_shape=[jax.ShapeDtypeStruct((ROWS, D), F32), jax.ShapeDtypeStruct((ROWS, D), BF16)],
        scratch_shapes=[pltpu.VMEM((kp, D), BF16) for _ in parts],
        compiler_params=_cparams(("arbitrary",)),
        name="mm_out",
    )(*parts, *([w] * n_parts), x, mods, mods, gains)


FFN_TF = 256
FFN_NF = D_FF // FFN_TF


def _ffn_kernel(h_ref, wg_ref, wu_ref, wd_ref, x_ref, mod_ref, nmod_ref, g_ref, xo_ref, ho_ref,
                wg_s, wu_s, wd_s, acc_ref):
    i = pl.program_id(0)
    f = pl.program_id(1)

    @pl.when(i == 0)
    def _():
        wg_s[f] = wg_ref[...].astype(BF16)
        wu_s[f] = wu_ref[...].astype(BF16)
        wd_s[f] = wd_ref[...].astype(BF16)

    @pl.when(f == 0)
    def _():
        acc_ref[...] = jnp.zeros_like(acc_ref)

    h = h_ref[...]
    gp = jnp.dot(h, wg_s[f], preferred_element_type=F32)
    up = jnp.dot(h, wu_s[f], preferred_element_type=F32)
    a = (_silu(gp) * up).astype(BF16)
    acc_ref[...] += jnp.dot(a, wd_s[f], preferred_element_type=F32)

    @pl.when(f == FFN_NF - 1)
    def _():
        _residual_epilogue(x_ref[...], acc_ref[...], mod_ref, nmod_ref, g_ref, xo_ref, ho_ref, 5, 0, 1, False)


def ffn(h, w_gu, w_dn, e, x, mods, layer, gains):
    tm = 512
    tf, nf = FFN_TF, FFN_NF

    def once(i, f):
        return jnp.where(i == 0, f, nf - 1)

    rowspec = pl.BlockSpec((tm, D), lambda i, f: (i, 0))
    return pl.pallas_call(
        _ffn_kernel,
        grid=(ROWS // tm, nf),
        in_specs=[rowspec,
                  pl.BlockSpec((None, D, tf), lambda i, f: (e, 0, once(i, f))),
                  pl.BlockSpec((None, D, tf), lambda i, f: (e, 0, nf + once(i, f))),
                  pl.BlockSpec((None, tf, D), lambda i, f: (e, once(i, f), 0)),
                  rowspec, _modspec(layer, tm, 2), _modspec(layer + 1, tm, 2), _gainspec(layer + 1, 2)],
        out_specs=[rowspec, rowspec],
        out_shape=[jax.ShapeDtypeStruct((ROWS, D), F32), jax.ShapeDtypeStruct((ROWS, D), BF16)],
        scratch_shapes=[pltpu.VMEM((nf, D, tf), BF16),
                        pltpu.VMEM((nf, D, tf), BF16),
                        pltpu.VMEM((nf, tf, D), BF16),
                        pltpu.VMEM((tm, D), F32)],
        compiler_params=_cparams(("arbitrary", "arbitrary")),
        name="ffn",
    )(h, w_gu, w_gu, w_dn, x, mods, mods, gains)


def _router_kernel(x_ref, g_ref, mod_ref, rw_ref, rb_ref, gates_ref):
    h = _modnorm(x_ref[...], g_ref[...], mod_ref[0, 3:4, :], mod_ref[0, 4:5, :])
    logits = jnp.dot(h, rw_ref[...], precision=HIGHEST, preferred_element_type=F32)
    lane = lax.broadcasted_iota(jnp.int32, logits.shape, 1)
    sel = jnp.where(lane < N_EXPERTS, logits + rb_ref[...], -jnp.inf)
    m1 = jnp.max(sel, axis=-1, keepdims=True)
    i1 = jnp.min(jnp.where(sel == m1, lane, LANES), axis=-1, keepdims=True)
    sel2 = jnp.where(lane == i1, -jnp.inf, sel)
    m2 = jnp.max(sel2, axis=-1, keepdims=True)
    i2 = jnp.min(jnp.where(sel2 == m2, lane, LANES), axis=-1, keepdims=True)
    l1 = jnp.sum(jnp.where(lane == i1, logits, 0.0), axis=-1, keepdims=True)
    l2 = jnp.sum(jnp.where(lane == i2, logits, 0.0), axis=-1, keepdims=True)
    mx = jnp.maximum(l1, l2)
    e1 = jnp.exp(l1 - mx)
    e2 = jnp.exp(l2 - mx)
    den = e1 + e2
    gates_ref[...] = jnp.where(lane == i1, e1 / den, 0.0) + jnp.where(lane == i2, e2 / den, 0.0)


def router(x, gains, mods, layer, rw_p, rb_p, o):
    tm = 512
    return pl.pallas_call(
        _router_kernel,
        grid=(ROWS // tm,),
        in_specs=[pl.BlockSpec((tm, D), lambda i: (i, 0)),
                  _gainspec(layer), _modspec(layer, tm),
                  pl.BlockSpec((None, D, LANES), lambda i: (o, 0, 0)),
                  pl.BlockSpec((None, 1, LANES), lambda i: (o, 0, 0))],
        out_specs=pl.BlockSpec((tm, LANES), lambda i: (i, 0)),
        out_shape=jax.ShapeDtypeStruct((ROWS, LANES), F32),
        compiler_params=_cparams(("arbitrary",)),
        name="router",
    )(x, gains, mods, rw_p, rb_p)


def _moe_kernel(h_ref, gates_ref, wgu_ref, wdn_ref, yin_ref, yo_ref, wgu_s, wdn_s):
    e = pl.program_id(0)

    @pl.when(pl.program_id(1) == 0)
    def _():
        wgu_s[...] = wgu_ref[...].astype(BF16)
        wdn_s[...] = wdn_ref[...].astype(BF16)

    gu = jnp.dot(h_ref[...], wgu_s[...], preferred_element_type=F32)
    a = (_silu(gu[:, :D_EXPERT]) * gu[:, D_EXPERT:]).astype(BF16)
    y = jnp.dot(a, wdn_s[...], preferred_element_type=F32)
    gt = gates_ref[...]
    lane = lax.broadcasted_iota(jnp.int32, gt.shape, 1)
    gate = jnp.sum(jnp.where(lane == e, gt, 0.0), axis=-1, keepdims=True)
    yo_ref[...] = yin_ref[...] + gate * y


def moe_dense(h, gates, w_gu, w_dn, o):
    tm = 512
    y0 = jnp.zeros((ROWS, D), F32)
    return pl.pallas_call(
        _moe_kernel,
        grid=(N_EXPERTS, ROWS // tm),
        in_specs=[pl.BlockSpec((tm, D), lambda e, i: (i, 0)),
                  pl.BlockSpec((tm, LANES), lambda e, i: (i, 0)),
                  pl.BlockSpec((None, None, D, 2 * D_EXPERT), lambda e, i: (o, e, 0, 0),
                               pipeline_mode=pl.Buffered(1)),
                  pl.BlockSpec((None, None, D_EXPERT, D), lambda e, i: (o, e, 0, 0),
                               pipeline_mode=pl.Buffered(1)),
                  pl.BlockSpec((tm, D), lambda e, i: (i, 0))],
        out_specs=pl.BlockSpec((tm, D), lambda e, i: (i, 0)),
        out_shape=jax.ShapeDtypeStruct((ROWS, D), F32),
        scratch_shapes=[pltpu.VMEM((D, 2 * D_EXPERT), BF16),
                        pltpu.VMEM((D_EXPERT, D), BF16)],
        input_output_aliases={4: 0},
        compiler_params=_cparams(("arbitrary", "arbitrary")),
        name="moe",
    )(h, gates, w_gu, w_dn, y0)


def _resid_kernel(x_ref, y_ref, mod_ref, nmod_ref, g_ref, *out_refs, final):
    if final:
        xo_ref, ho_ref = None, out_refs[0]
    else:
        xo_ref, ho_ref = out_refs
    _residual_epilogue(x_ref[...], y_ref[...], mod_ref, nmod_ref, g_ref, xo_ref, ho_ref, 5, 0, 1, final)


def resid(x, y, mods, layer, gains, final):
    tm = 512
    rowspec = pl.BlockSpec((tm, D), lambda i: (i, 0))
    if final:
        out_specs = [rowspec]
        out_shape = [jax.ShapeDtypeStruct((ROWS, D), F32)]
        nmod, gain = _modspec(layer, tm), pl.BlockSpec((1, D), lambda i: (0, 0))
    else:
        out_specs = [rowspec, rowspec]
        out_shape = [jax.ShapeDtypeStruct((ROWS, D), F32), jax.ShapeDtypeStruct((ROWS, D), BF16)]
        nmod, gain = _modspec(layer + 1, tm), _gainspec(layer + 1)
    return pl.pallas_call(
        functools.partial(_resid_kernel, final=final),
        grid=(ROWS // tm,),
        in_specs=[rowspec, rowspec, _modspec(layer, tm), nmod, gain],
        out_specs=out_specs,
        out_shape=out_shape,
        compiler_params=_cparams(("arbitrary",)),
        name="resid",
    )(x, y, mods, mods, gains)


def _softmax_pv(scores, values):
    m = None
    for s in scores:
        mi = jnp.max(s, axis=-1, keepdims=True)
        m = mi if m is None else jnp.maximum(m, mi)
    num, den = None, None
    for s, v in zip(scores, values):
        p = jnp.exp(s - m)
        li = jnp.sum(p, axis=-1, keepdims=True)
        oi = jnp.dot(p.astype(BF16), v, preferred_element_type=F32)
        num = oi if num is None else num + oi
        den = li if den is None else den + li
    return num / den


def _qk(q, k):
    return lax.dot_general(q, k, (((1,), (1,)), ((), ())), preferred_element_type=F32)


def _head_rmsnorm(x, gain, n_heads):
    parts = []
    for h in range(n_heads):
        xh = x[:, h * HD:(h + 1) * HD]
        ms = jnp.mean(xh * xh, axis=-1, keepdims=True)
        parts.append(xh * lax.rsqrt(ms + EPS) * gain)
    return parts


def _rope128(x, cos, sin_signed):
    lane = lax.broadcasted_iota(jnp.int32, x.shape, 1)
    up = pltpu.roll(x, LANES - 16, 1)
    dn = pltpu.roll(x, 16, 1)
    partner = jnp.where((lane % 32) < 16, up, dn)
    return x * cos + partner * sin_signed


def _rope_tables():
    t = np.arange(T_LAT)
    n_f = HD // 4
    inv = ROPE_THETA ** (-np.arange(n_f, dtype=np.float32) / n_f)
    cos = np.zeros((T_LAT, HD), np.float32)
    sin = np.zeros((T_LAT, HD), np.float32)
    for half, pos in ((0, t // GRID_W), (1, t % GRID_W)):
        ang = pos[:, None].astype(np.float32) * inv[None, :]
        c, s = np.cos(ang), np.sin(ang)
        base = half * (HD // 2)
        cos[:, base:base + n_f] = c
        cos[:, base + n_f:base + 2 * n_f] = c
        sin[:, base:base + n_f] = -s
        sin[:, base + n_f:base + 2 * n_f] = s
    return np.tile(cos, (1, 2)), np.tile(sin, (1, 2))


def _attn_a_ctx_kernel(q_ref, kv_ref, qg_ref, kg_ref, *rest):
    o_ref, ko_ref, vo_ref = rest[-3:]
    q = q_ref[...]
    kv = kv_ref[...]
    k = kv[:, :A_KV_W]
    v = kv[:, A_KV_W:]
    kn = _head_rmsnorm(k, kg_ref[...], A_KV_HEADS)
    ko_ref[...] = jnp.concatenate(kn, axis=-1)
    vo_ref[...] = v
    qn = _head_rmsnorm(q, qg_ref[...], A_HEADS)
    group = A_HEADS // A_KV_HEADS
    outs = []
    for h in range(A_HEADS):
        g = h // group
        kh = kn[g].astype(BF16)
        vh = v[:, g * HD:(g + 1) * HD].astype(BF16)
        qh = (qn[h] * ATT_SCALE).astype(BF16)
        outs.append(_softmax_pv([_qk(qh, kh)], [vh]))
    o_ref[...] = jnp.concatenate(outs, axis=-1).astype(BF16)


def attn_a_ctx(z, gq, gk, e, prev_caches):
    t = T_CTX
    n_even = gq.shape[0]
    cache_spec = pl.BlockSpec((None, None, t, A_KV_W), lambda b: (b, e, 0, 0))
    cache_shape = jax.ShapeDtypeStruct((N_CTX, n_even, t, A_KV_W), F32)
    extra = [] if prev_caches is None else list(prev_caches)
    aliases = {} if prev_caches is None else {4: 1, 5: 2}
    return pl.pallas_call(
        _attn_a_ctx_kernel,
        grid=(N_CTX,),
        in_specs=[pl.BlockSpec((t, A_W), lambda b: (b, 0)),
                  pl.BlockSpec((t, 2 * A_KV_W), lambda b: (b, A_W // (2 * A_KV_W))),
                  pl.BlockSpec((None, 1, HD), lambda b: (e, 0, 0)),
                  pl.BlockSpec((None, 1, HD), lambda b: (e, 0, 0))] + [_untouched() for _ in extra],
        out_specs=[pl.BlockSpec((t, A_W), lambda b: (b, 0)), cache_spec, cache_spec],
        out_shape=[jax.ShapeDtypeStruct((ROWS, A_W), BF16), cache_shape, cache_shape],
        input_output_aliases=aliases,
        compiler_params=_cparams(("arbitrary",)),
        name="attn_a_ctx",
    )(z, z, gq, gk, *extra)


A_LAT_TQ = 256


def _attn_a_lat_kernel(q_ref, kv_ref, ck_ref, cv_ref, qg_ref, kg_ref, cq_ref, sq_ref, ckk_ref, skk_ref,
                       prev_ref, o_ref, k_s, v_s):
    del prev_ref

    @pl.when(pl.program_id(1) == 0)
    def _():
        kv = kv_ref[...]
        kn = jnp.concatenate(_head_rmsnorm(kv[:, :A_KV_W], kg_ref[...], A_KV_HEADS), axis=-1)
        k_s[...] = _rope128(kn, ckk_ref[...], skk_ref[...]).astype(BF16)
        v_s[...] = kv[:, A_KV_W:].astype(BF16)

    qn = _head_rmsnorm(q_ref[...], qg_ref[...], A_HEADS)
    cos, sin = cq_ref[...], sq_ref[...]
    qr = []
    for c in range(A_HEADS // 2):
        slab = _rope128(jnp.concatenate(qn[2 * c:2 * c + 2], axis=-1), cos, sin)
        qr.append(slab[:, :HD])
        qr.append(slab[:, HD:])
    ck = ck_ref[...].astype(BF16)
    cv = cv_ref[...].astype(BF16)
    kk = k_s[...]
    vv = v_s[...]
    group = A_HEADS // A_KV_HEADS
    outs = []
    for h in range(A_HEADS):
        g = h // group
        sl = slice(g * HD, (g + 1) * HD)
        qh = (qr[h] * ATT_SCALE).astype(BF16)
        outs.append(_softmax_pv([_qk(qh, kk[:, sl]), _qk(qh, ck[:, sl])], [vv[:, sl], cv[:, sl]]))
    o_ref[...] = jnp.concatenate(outs, axis=-1).astype(BF16)


def attn_a_lat(z, ck, cv, gq, gk, e, y_prev):
    tq = A_LAT_TQ
    nq = T_LAT // tq
    cos, sin = _rope_tables()
    cos, sin = jnp.asarray(cos), jnp.asarray(sin)
    row0 = ROWS_CTX // tq
    seq0 = ROWS_CTX // T_LAT
    cache_spec = pl.BlockSpec((None, None, PAST, A_KV_W), lambda b, i: (b, e, 0, 0))
    return pl.pallas_call(
        _attn_a_lat_kernel,
        grid=(N_LAT, nq),
        in_specs=[pl.BlockSpec((tq, A_W), lambda b, i: (row0 + b * nq + i, 0)),
                  pl.BlockSpec((T_LAT, 2 * A_KV_W), lambda b, i: (seq0 + b, A_W // (2 * A_KV_W))),
                  cache_spec, cache_spec,
                  pl.BlockSpec((None, 1, HD), lambda b, i: (e, 0, 0)),
                  pl.BlockSpec((None, 1, HD), lambda b, i: (e, 0, 0)),
                  pl.BlockSpec((tq, LANES), lambda b, i: (i, 0)),
                  pl.BlockSpec((tq, LANES), lambda b, i: (i, 0)),
                  pl.BlockSpec((T_LAT, LANES), lambda b, i: (0, 0)),
                  pl.BlockSpec((T_LAT, LANES), lambda b, i: (0, 0)),
                  _untouched()],
        out_specs=pl.BlockSpec((tq, A_W), lambda b, i: (row0 + b * nq + i, 0)),
        out_shape=jax.ShapeDtypeStruct((ROWS, A_W), BF16),
        scratch_shapes=[pltpu.VMEM((T_LAT, A_KV_W), BF16),
                        pltpu.VMEM((T_LAT, A_KV_W), BF16)],
        input_output_aliases={10: 0},
        compiler_params=_cparams(("arbitrary", "arbitrary")),
        name="attn_a_lat",
    )(z, z, ck, cv, gq, gk, cos, sin, cos, sin, y_prev)


def _attn_c_ctx_kernel(q_ref, k_ref, v_ref, *rest):
    o_ref, ko_ref, vo_ref = rest[-3:]
    q = q_ref[...]
    k = k_ref[...]
    v = v_ref[...]
    ko_ref[...] = k
    vo_ref[...] = v
    outs = []
    for h in range(q.shape[1] // HD):
        sl = slice(h * HD, (h + 1) * HD)
        qh = (q[:, sl] * ATT_SCALE).astype(BF16)
        outs.append(_softmax_pv([_qk(qh, k[:, sl].astype(BF16))], [v[:, sl].astype(BF16)]))
    o_ref[...] = jnp.concatenate(outs, axis=-1).astype(BF16)


C_CTX_HEADS_PER_STEP = 8


def attn_c_ctx(z, o, n_odd, prev_caches):
    t = T_CTX
    wblk = C_CTX_HEADS_PER_STEP * HD
    nhp = C_W // wblk
    cache_spec = pl.BlockSpec((None, None, t, wblk), lambda b, p: (b, o, 0, p))
    cache_shape = jax.ShapeDtypeStruct((N_CTX, n_odd, t, C_W), F32)
    extra = [] if prev_caches is None else list(prev_caches)
    aliases = {} if prev_caches is None else {3: 1, 4: 2}
    return pl.pallas_call(
        _attn_c_ctx_kernel,
        grid=(N_CTX, nhp),
        in_specs=[pl.BlockSpec((t, wblk), lambda b, p: (b, p)),
                  pl.BlockSpec((t, wblk), lambda b, p: (b, nhp + p)),
                  pl.BlockSpec((t, wblk), lambda b, p: (b, 2 * nhp + p))] + [_untouched() for _ in extra],
        out_specs=[pl.BlockSpec((t, wblk), lambda b, p: (b, p)), cache_spec, cache_spec],
        out_shape=[jax.ShapeDtypeStruct((ROWS, C_W), BF16), cache_shape, cache_shape],
        input_output_aliases=aliases,
        compiler_params=_cparams(("arbitrary", "arbitrary")),
        name="attn_c_ctx",
    )(z, z, z, *extra)


NA_GRID_ROWS = T_LAT // GRID_W
NA_WIN = NA_ROWS * GRID_W
NA_DR = 2 * NA_ROWS - 1
NA_DC = 2 * NA_COLS - 1


def _na_row_start(r):
    return min(max(r - NA_ROWS // 2, 0), NA_GRID_ROWS - NA_ROWS)


def _na_groups():
    groups, r = [], 0
    while r < NA_GRID_ROWS:
        r1 = r
        while r1 + 1 < NA_GRID_ROWS and (_na_row_start(r1 + 1) == _na_row_start(r) or r1 + 1 - r < 4):
            r1 += 1
        lo = _na_row_start(r)
        hi = _na_row_start(r1) + NA_ROWS
        pairs = -(-(hi - lo) // 2)
        lo = min(lo, NA_GRID_ROWS - 2 * pairs)
        groups.append((r, r1, lo, pairs))
        r = r1 + 1
    return groups


def _na_pair_codes():
    codes = []
    plan = []
    for r0, r1, lo, pairs in _na_groups():
        rows = []
        for r in range(r0, r1 + 1):
            rs = _na_row_start(r)
            row = []
            for p in range(pairs):
                code = tuple((kr - r + NA_ROWS - 1) if rs <= kr < rs + NA_ROWS else None
                             for kr in (lo + 2 * p, lo + 2 * p + 1))
                if code not in codes:
                    codes.append(code)
                row.append(codes.index(code))
            rows.append(row)
        plan.append(rows)
    return codes, plan


def _na_bias_table(rpb):
    n_l = rpb.shape[0]
    col = np.arange(GRID_W)
    cs = np.clip(col - NA_COLS // 2, 0, GRID_W - NA_COLS)
    col_in = (col[None, :] >= cs[:, None]) & (col[None, :] < cs[:, None] + NA_COLS)
    period = GRID_W + 1
    seq = jnp.concatenate([rpb, jnp.zeros((n_l, C_HEADS, NA_DR, period - NA_DC), F32)], axis=-1)
    seq = jnp.roll(seq, -(NA_COLS - 1), axis=-1)
    tile = jnp.tile(seq, (1, 1, 1, GRID_W))[..., :GRID_W * GRID_W].reshape(n_l, C_HEADS, NA_DR, GRID_W, GRID_W)
    tile = jnp.where(jnp.asarray(col_in), tile, NEG_INF)
    masked = jnp.full((n_l, C_HEADS, GRID_W, GRID_W), NEG_INF, F32)
    pick = lambda dr: masked if dr is None else tile[:, :, dr]
    codes, _ = _na_pair_codes()
    return jnp.stack([jnp.concatenate([pick(a), pick(b)], axis=-1) for a, b in codes], axis=2)


def _attn_na_kernel(q_ref, k_ref, v_ref, ck_ref, cv_ref, bias_ref, prev_ref, o_ref):
    del prev_ref
    _, plan = _na_pair_codes()
    heads = []
    for h in range(LANES // HD):
        sl = slice(h * HD, (h + 1) * HD)
        q = (q_ref[:, sl] * ATT_SCALE).astype(BF16)
        k = k_ref[:, sl].astype(BF16)
        v = v_ref[:, sl].astype(BF16)
        ck = ck_ref[:, sl].astype(BF16)
        cv = cv_ref[:, sl].astype(BF16)
        rows = []
        for (r0, r1, lo, pairs), codes in zip(_na_groups(), plan):
            qg = q[r0 * GRID_W:(r1 + 1) * GRID_W]
            kw = k[lo * GRID_W:(lo + 2 * pairs) * GRID_W]
            vw = v[lo * GRID_W:(lo + 2 * pairs) * GRID_W]
            bias = jnp.concatenate(
                [jnp.concatenate([bias_ref[h, c] for c in row], axis=-1) for row in codes], axis=0)
            s_nb = _qk(qg, kw) + bias
            s_cx = _qk(qg, ck)
            rows.append(_softmax_pv([s_nb, s_cx], [vw, cv]))
        heads.append(jnp.concatenate(rows, axis=0))
    o_ref[...] = jnp.concatenate(heads, axis=-1).astype(BF16)


def attn_na(z, ck, cv, bias, o, y_prev):
    nhp = C_W // LANES
    hpb = LANES // HD
    seq0 = ROWS_CTX // T_LAT
    cache_spec = pl.BlockSpec((None, None, PAST, LANES), lambda p, b: (b, o, 0, p))
    return pl.pallas_call(
        _attn_na_kernel,
        grid=(nhp, N_LAT),
        in_specs=[pl.BlockSpec((T_LAT, LANES), lambda p, b: (seq0 + b, p)),
                  pl.BlockSpec((T_LAT, LANES), lambda p, b: (seq0 + b, nhp + p)),
                  pl.BlockSpec((T_LAT, LANES), lambda p, b: (seq0 + b, 2 * nhp + p)),
                  cache_spec, cache_spec,
                  pl.BlockSpec((None, hpb, bias.shape[2], GRID_W, LANES), lambda p, b: (o, p, 0, 0, 0)),
                  _untouched()],
        out_specs=pl.BlockSpec((T_LAT, LANES), lambda p, b: (seq0 + b, p)),
        out_shape=jax.ShapeDtypeStruct((ROWS, C_W), BF16),
        input_output_aliases={6: 0},
        compiler_params=_cparams(("arbitrary", "arbitrary")),
        name="attn_na",
    )(z, z, z, ck, cv, bias, y_prev)


def _seg_sum(x, n_heads):
    parts = []
    for h in range(n_heads):
        s = jnp.sum(x[:, h * HD:(h + 1) * HD], axis=-1, keepdims=True)
        parts.append(jnp.broadcast_to(s, (x.shape[0], HD)))
    return jnp.concatenate(parts, axis=-1)


PREP_TM = 256


def _seq_len_at(row_start):
    return jnp.where(row_start < ROWS_CTX, T_CTX, T_LAT)


def _rwkv_prep_kernel(z_ref, zprev_ref, znext_ref, mu_ref, kkw_ref, w0_ref, w2_ref, a0_ref, a2_ref, ka_ref,
                      rk_ref, g2_ref, o_rk_ref, o_wk_ref, o_bv_ref, g_ref, bonus_ref):
    z = z_ref[:, A_IN:]
    t = z.shape[0]
    start = pl.program_id(0) * t
    seq_len = _seq_len_at(start)
    pos = (start - jnp.where(start < ROWS_CTX, 0, ROWS_CTX)) % seq_len
    halo_prev = jnp.where(pos == 0, 0.0, zprev_ref[SUBLANES - 1:SUBLANES, A_IN:])
    halo_next = jnp.where(pos + t == seq_len, 0.0, znext_ref[0:1, A_IN:])
    row = lax.broadcasted_iota(jnp.int32, (t, 1), 0)
    prev = jnp.where(row == 0, halo_prev, pltpu.roll(z, 1, 0))
    nxt = jnp.where(row == t - 1, halo_next, pltpu.roll(z, t - 1, 0))
    m = z + mu_ref[...] * (0.5 * (prev + nxt) - z)
    r = m[:, :B_W]
    k = m[:, B_W:2 * B_W]
    v = m[:, 2 * B_W:3 * B_W]
    o = 3 * B_W
    wd = m[:, o:o + 2 * LORA_W]
    ad = m[:, o + 2 * LORA_W:o + 2 * LORA_W + 2 * LORA_A]
    gd = m[:, o + 2 * LORA_W + 2 * LORA_A:]

    kkr = k * kkw_ref[...]
    kk = kkr * lax.rsqrt(_seg_sum(kkr * kkr, B_HEADS) + 1e-12)

    wl = w0_ref[...] + jnp.dot(jnp.tanh(wd), w2_ref[...], precision=HIGHEST, preferred_element_type=F32)
    decay = jnp.exp(-float(np.exp(-0.5)) * jax.nn.sigmoid(wl))
    a = jax.nn.sigmoid(a0_ref[...] + jnp.dot(ad, a2_ref[...], precision=HIGHEST, preferred_element_type=F32))
    k2 = jnp.concatenate([k, k], axis=-1)
    ka2 = jnp.concatenate([ka_ref[...], ka_ref[...]], axis=-1)
    kk2 = jnp.concatenate([kk, kk], axis=-1)

    kd = k2 * (1.0 + (a - 1.0) * ka2)
    bb = kk2 * a
    for h in range(B_HEADS):
        sl = slice(h * HD, (h + 1) * HD)
        o_rk_ref[pl.ds(h, t, stride=B_HEADS), :] = jnp.concatenate([r[:, sl], kk[:, sl]], axis=-1)
        for dr in range(2):
            sd = slice(dr * B_W + h * HD, dr * B_W + (h + 1) * HD)
            row = pl.ds(dr * B_HEADS + h, t, stride=2 * B_HEADS)
            o_wk_ref[row, :] = jnp.concatenate([decay[:, sd], kd[:, sd]], axis=-1)
            o_bv_ref[row, :] = jnp.concatenate([bb[:, sd], v[:, sl]], axis=-1)
    g_ref[...] = jnp.dot(jax.nn.sigmoid(gd).astype(BF16), g2_ref[...].astype(BF16), preferred_element_type=F32)
    bonus_ref[...] = _seg_sum(r * k * rk_ref[...], B_HEADS) * v


def rwkv_prep(z, params):
    tm = PREP_TM
    full = lambda a: pl.BlockSpec(a.shape, lambda s: (0,) * a.ndim)
    out_rows = (B_HEADS, 2 * B_HEADS, 2 * B_HEADS)
    per = tm // SUBLANES
    last = ROWS // SUBLANES - 1
    rowspec = pl.BlockSpec((tm, B_W), lambda s: (s, 0))
    return pl.pallas_call(
        _rwkv_prep_kernel,
        grid=(ROWS // tm,),
        in_specs=[pl.BlockSpec((tm, EVEN_IN), lambda s: (s, 0)),
                  pl.BlockSpec((SUBLANES, EVEN_IN), lambda s: (jnp.maximum(s * per - 1, 0), 0)),
                  pl.BlockSpec((SUBLANES, EVEN_IN), lambda s: (jnp.minimum((s + 1) * per, last), 0))]
                 + [full(p) for p in params],
        out_specs=[pl.BlockSpec((tm * n, LANES), lambda s: (s, 0)) for n in out_rows] + [rowspec, rowspec],
        out_shape=[jax.ShapeDtypeStruct((ROWS * n, LANES), F32) for n in out_rows]
                  + [jax.ShapeDtypeStruct((ROWS, B_W), F32)] * 2,
        compiler_params=_cparams(("arbitrary",)),
        name="rwkv_prep",
    )(z, z, z, *params)


def rwkv_prep_all(z, prm, e):
    w2 = prm['b_w2'][e]
    a2 = prm['b_a2'][e]
    zero = jnp.zeros((LORA_W, B_W), F32)
    w2bd = jnp.concatenate([jnp.concatenate([w2[0], zero], 1), jnp.concatenate([zero, w2[1]], 1)], 0)
    a2bd = jnp.concatenate([jnp.concatenate([a2[0], zero], 1), jnp.concatenate([zero, a2[1]], 1)], 0)
    params = (prm['b_mu'][e][None], prm['b_kk'][e][None], prm['b_w0'][e].reshape(1, 2 * B_W), w2bd,
              prm['b_a0'][e].reshape(1, 2 * B_W), a2bd, prm['b_ka'][e][None],
              prm['b_rk'][e].reshape(1, B_W), prm['b_g2'][e])
    return rwkv_prep(z, params)


SCAN_CHAINS = 8


def _tree_sum(xs):
    while len(xs) > 1:
        xs = [xs[i] + xs[i + 1] for i in range(0, len(xs) - 1, 2)] + ([xs[-1]] if len(xs) % 2 else [])
    return xs[0]


def _scan_step(rk_b, wk_b, bv_b, vv, s_ref, ni):
    n_acc = max(1, SCAN_CHAINS // ni)

    def bcast(ref, row):
        return jnp.broadcast_to(ref[pl.ds(row, 1), :], (SUBLANES, LANES))

    sa = [[None] * n_acc for _ in range(ni)]
    for j in range(HD):
        kkj = bcast(rk_b, HD + j)
        for g in range(ni):
            p = s_ref[j, pl.ds(g * SUBLANES, SUBLANES), :] * kkj
            a = j % n_acc
            sa[g][a] = p if sa[g][a] is None else sa[g][a] + p
    sa = [-_tree_sum(x) for x in sa]
    ya = [[None] * n_acc for _ in range(ni)]
    for j in range(HD):
        wj = bcast(wk_b, j)
        kj = bcast(wk_b, HD + j)
        bj = bcast(bv_b, j)
        rj = bcast(rk_b, j)
        for g in range(ni):
            sl = pl.ds(g * SUBLANES, SUBLANES)
            s_new = s_ref[j, sl, :] * wj + sa[g] * bj + vv[g] * kj
            s_ref[j, sl, :] = s_new
            p = s_new * rj
            a = j % n_acc
            ya[g][a] = p if ya[g][a] is None else ya[g][a] + p
    return [_tree_sum(ya[g]) for g in range(ni)]


def _scan_pairs(tc, relayout, run):
    relayout(0, 0)

    def pair(k, carry):
        i0 = 2 * k
        relayout(i0 + 1, 1)
        run(i0, 0)
        relayout(jnp.minimum(i0 + 2, tc - 1), 0)
        run(i0 + 1, 1)
        return carry

    lax.fori_loop(0, tc // 2, pair, 0)


def _scan_ctx_kernel(rk_ref, wk_ref, bv_ref, y_ref, st_ref, rk0, wk0, bv0, rk1, wk1, bv1, s_ref, *, tc):
    d = pl.program_id(0)
    c = pl.program_id(1)
    ni = HD // SUBLANES
    bufs = ((rk0, wk0, bv0), (rk1, wk1, bv1))

    @pl.when(c == 0)
    def _():
        s_ref[...] = jnp.zeros_like(s_ref)

    def t_of(i):
        return i + d * (tc - 1 - 2 * i)

    def relayout(i, slot):
        t = t_of(i)
        for src, dst in zip((rk_ref, wk_ref, bv_ref), bufs[slot]):
            dst[...] = src[:, :, t].reshape(LANES, LANES).T

    def run(i, slot):
        rk_b, wk_b, bv_b = bufs[slot]
        vv = [bv_b[pl.ds(HD + g * SUBLANES, SUBLANES), :] for g in range(ni)]
        y = jnp.concatenate(_scan_step(rk_b, wk_b, bv_b, vv, s_ref, ni), axis=0)
        y_ref[:, :, t_of(i)] = y.T.reshape(CTX_GROUPS, ROW_SUB, B_HEADS, HD)

    _scan_pairs(tc, relayout, run)

    @pl.when(c == pl.num_programs(1) - 1)
    def _():
        st_ref[...] = s_ref[...]


def _rows5(x, heads):
    return x.reshape(ROW_GROUPS, ROW_SUB, T_CTX, heads, x.shape[-1])


def _operand_bufs():
    return [pltpu.VMEM((LANES, LANES), F32)] * 6


def rwkv_scan_ctx(prep, tc):
    rk, wk, bv = prep[:3]
    nc = T_CTX // tc

    def chunk(d, c):
        return c + d * (nc - 1 - 2 * c)

    blk = (CTX_GROUPS, ROW_SUB, tc, B_HEADS, LANES)
    shared = pl.BlockSpec(blk, lambda d, c: (0, 0, chunk(d, c), 0, 0))
    perdir = pl.BlockSpec(blk, lambda d, c: (0, 0, chunk(d, c), d, 0))
    yspec = pl.BlockSpec((CTX_GROUPS, ROW_SUB, tc, B_HEADS, HD), lambda d, c: (0, 0, chunk(d, c), d, 0))
    return pl.pallas_call(
        functools.partial(_scan_ctx_kernel, tc=tc),
        grid=(2, nc),
        in_specs=[shared, perdir, perdir],
        out_specs=[yspec, pl.BlockSpec((None, HD, HD, LANES), lambda d, c: (d, 0, 0, 0))],
        out_shape=[jax.ShapeDtypeStruct((ROW_GROUPS, ROW_SUB, T_CTX, 2 * B_HEADS, HD), F32),
                   jax.ShapeDtypeStruct((2, HD, HD, LANES), F32)],
        scratch_shapes=_operand_bufs() + [pltpu.VMEM((HD, HD, LANES), F32)],
        compiler_params=_cparams(("arbitrary", "arbitrary")),
        name="rwkv_scan_ctx",
    )(_rows5(rk, B_HEADS), _rows5(wk, 2 * B_HEADS), _rows5(bv, 2 * B_HEADS))


LAT_REP = LANES // (N_LAT * B_HEADS)


def _scan_lat_kernel(rk_ref, wk_ref, bv_ref, s0_ref, yprev_ref, y_ref, rk0, wk0, bv0, rk1, wk1, bv1, s_ref,
                     y_s, *, tc):
    del yprev_ref
    d = pl.program_id(0)
    c = pl.program_id(1)
    n = N_LAT * B_HEADS
    bufs = ((rk0, wk0, bv0), (rk1, wk1, bv1))

    @pl.when(c == 0)
    def _():
        s_ref[...] = s0_ref[...]

    def t_of(i):
        return i + d * (tc - 1 - 2 * i)

    def relayout(i, slot):
        t = t_of(i)
        for src, dst in zip((rk_ref, wk_ref, bv_ref), bufs[slot]):
            m = src[:, 0, t].reshape(n, LANES)
            dst[...] = jnp.concatenate([m] * LAT_REP, axis=0).T

    def run(i, slot):
        rk_b, wk_b, bv_b = bufs[slot]
        group = lax.broadcasted_iota(jnp.int32, (SUBLANES, LANES), 1) // n
        v = bv_b[pl.ds(HD, SUBLANES), :]
        for g in range(1, LAT_REP):
            v = jnp.where(group == g, bv_b[pl.ds(HD + g * SUBLANES, SUBLANES), :], v)
        (y_s[t_of(i)],) = _scan_step(rk_b, wk_b, bv_b, [v], s_ref, 1)

    _scan_pairs(tc, relayout, run)

    def store(t, carry):
        y = y_s[t]
        rows = [y] + [pltpu.roll(y, LANES - g * n, 1) for g in range(1, LAT_REP)]
        yt = jnp.concatenate(rows, axis=0).T
        y_ref[:, 0, t] = yt[:n].reshape(N_LAT, B_HEADS, HD)
        return carry

    lax.fori_loop(0, tc, store, 0, unroll=8)


def rwkv_scan_lat(prep, s0f, s0b, y_prev, tc):
    rk, wk, bv = prep[:3]
    nc = T_LAT // tc
    per_sub = T_CTX // tc
    g0 = CTX_GROUPS // N_LAT

    def pos(d, c):
        cc = c + d * (nc - 1 - 2 * c)
        return cc // per_sub, cc % per_sub

    def shared_map(d, c):
        q, off = pos(d, c)
        return (g0, q, off, 0, 0)

    def perdir_map(d, c):
        q, off = pos(d, c)
        return (g0, q, off, d, 0)

    blk = (N_LAT, 1, tc, B_HEADS, LANES)
    s0 = jnp.stack([s0f, s0b]).reshape(2, N_LAT, B_HEADS, LAT_REP, SUBLANES, HD)
    s0 = s0.transpose(0, 5, 4, 3, 1, 2).reshape(2, HD, SUBLANES, LANES)
    return pl.pallas_call(
        functools.partial(_scan_lat_kernel, tc=tc),
        grid=(2, nc),
        in_specs=[pl.BlockSpec(blk, shared_map), pl.BlockSpec(blk, perdir_map), pl.BlockSpec(blk, perdir_map),
                  pl.BlockSpec((None, HD, SUBLANES, LANES), lambda d, c: (d, 0, 0, 0)),
                  _untouched()],
        out_specs=pl.BlockSpec((N_LAT, 1, tc, B_HEADS, HD), perdir_map),
        out_shape=jax.ShapeDtypeStruct((ROW_GROUPS, ROW_SUB, T_CTX, 2 * B_HEADS, HD), F32),
        scratch_shapes=_operand_bufs() + [pltpu.VMEM((HD, SUBLANES, LANES), F32),
                                          pltpu.VMEM((tc, SUBLANES, LANES), F32)],
        input_output_aliases={4: 0},
        compiler_params=_cparams(("arbitrary", "arbitrary")),
        name="rwkv_scan_lat",
    )(_rows5(rk, B_HEADS), _rows5(wk, 2 * B_HEADS), _rows5(bv, 2 * B_HEADS), s0, y_prev)


def _rwkv_post_kernel(y_ref, bonus_ref, g_ref, lng_ref, lnb_ref, o_ref):
    tm = o_ref.shape[0]
    outs = []
    for h in range(B_HEADS):
        sl = slice(h * HD, (h + 1) * HD)
        y = (y_ref[pl.ds(h, tm, stride=2 * B_HEADS), :]
             + y_ref[pl.ds(B_HEADS + h, tm, stride=2 * B_HEADS), :])
        yc = y - jnp.mean(y, axis=-1, keepdims=True)
        var = jnp.mean(yc * yc, axis=-1, keepdims=True)
        yn = yc * lax.rsqrt(var + GN_EPS) * lng_ref[:, sl] + lnb_ref[:, sl]
        outs.append((yn + bonus_ref[:, sl]) * g_ref[:, sl])
    o_ref[...] = jnp.concatenate(outs, axis=-1).astype(BF16)


def rwkv_post(y, bonus, g, lng, lnb, e):
    tm = 256
    rowspec = pl.BlockSpec((tm, B_W), lambda i: (i, 0))
    vecspec = pl.BlockSpec((None, 1, B_W), lambda i: (e, 0, 0))
    return pl.pallas_call(
        _rwkv_post_kernel,
        grid=(ROWS // tm,),
        in_specs=[pl.BlockSpec((tm * 2 * B_HEADS, HD), lambda i: (i, 0)), rowspec, rowspec, vecspec, vecspec],
        out_specs=rowspec,
        out_shape=jax.ShapeDtypeStruct((ROWS, B_W), BF16),
        compiler_params=_cparams(("arbitrary",)),
        name="rwkv_post",
    )(y, bonus, g, lng, lnb)


def kernel(x_prompt, x_sample, cache_a_k, cache_a_v, state_b_fwd, state_b_bwd, cache_c_k, cache_c_v, c, c_ctx,
           ada_w, ada_b, norm1_g, norm2_g, final_norm_g, w_in_e, w_out_e, a_q_gain, a_k_gain, b_mu, b_w0, b_w2,
           b_a0, b_a2, b_g2, b_kk, b_ka, b_rk, b_ln_g, b_ln_b, ffn_w_gu, ffn_w_dn, w_in_o, w_out_o, c_rpb,
           router_w, router_b, moe_w_gu, moe_w_dn):
    n_even, n_odd = w_in_e.shape[0], w_in_o.shape[0]
    prm = dict(b_mu=b_mu, b_w0=b_w0, b_w2=b_w2, b_a0=b_a0, b_a2=b_a2, b_g2=b_g2, b_kk=b_kk, b_ka=b_ka,
               b_rk=b_rk)
    cond = jnp.zeros((N_MODS, D), F32).at[0].set(c_ctx).at[1:1 + N_LAT].set(c)
    mods = ada_all(cond, ada_w, ada_b).reshape(DEPTH, N_MODS, 6, D)
    g1 = norm1_g.reshape(DEPTH, 1, D)
    g2 = norm2_g.reshape(DEPTH, 1, D)
    gq = a_q_gain.reshape(n_even, 1, HD)
    gk = a_k_gain.reshape(n_even, 1, HD)
    lng = b_ln_g.reshape(n_even, 1, B_W)
    lnb = b_ln_b.reshape(n_even, 1, B_W)
    ck_a = cache_a_k.reshape(N_LAT, n_even, PAST, A_KV_W)
    cv_a = cache_a_v.reshape(N_LAT, n_even, PAST, A_KV_W)
    ck_c = cache_c_k.reshape(N_LAT, n_odd, PAST, C_W)
    cv_c = cache_c_v.reshape(N_LAT, n_odd, PAST, C_W)
    na_bias = _na_bias_table(c_rpb)
    rw_p = jnp.zeros((n_odd, D, LANES), F32).at[:, :, :N_EXPERTS].set(router_w)
    rb_p = jnp.zeros((n_odd, 1, LANES), F32).at[:, 0, :N_EXPERTS].set(router_b)

    x, h = first_norm(x_prompt, x_sample, g1, mods)

    a_caches, c_caches = None, None
    new_sf, new_sb = [], []
    y_final = None
    for l in range(DEPTH):
        if l % 2 == 0:
            e = l // 2
            z = mm_in(h, w_in_e, e, tn=EVEN_IN // 3)
            y_a, k_new, v_new = attn_a_ctx(z, gq, gk, e, a_caches)
            a_caches = (k_new, v_new)
            y_a = attn_a_lat(z, ck_a, cv_a, gq, gk, e, y_a)
            prep = rwkv_prep_all(z, prm, e)
            y_s, st = rwkv_scan_ctx(prep, tc=32)
            y_s = rwkv_scan_lat(prep, state_b_fwd[:, e], state_b_bwd[:, e], y_s, tc=64)
            st = st.reshape(2, HD, HD, N_CTX, B_HEADS).transpose(0, 3, 4, 2, 1)
            new_sf.append(st[0])
            new_sb.append(st[1])
            y_b = rwkv_post(y_s.reshape(ROWS * 2 * B_HEADS, HD), prep[4], prep[3], lng, lnb, e)
            x, h = mm_out([y_a, y_b], w_out_e, e, x, mods, l, g2, l)
            x, h = ffn(h, ffn_w_gu, ffn_w_dn, e, x, mods, l, g1)
        else:
            o = l // 2
            z = mm_in(h, w_in_o, o, tn=C_W)
            y, k_new, v_new = attn_c_ctx(z, o, n_odd, c_caches)
            c_caches = (k_new, v_new)
            y = attn_na(z, ck_c, cv_c, na_bias, o, y)
            x, h = mm_out([y], w_out_o, o, x, mods, l, g2, l)
            gates = router(x, g2, mods, l, rw_p, rb_p, o)
            y_moe = moe_dense(h, gates, moe_w_gu, moe_w_dn, o)
            if l + 1 < DEPTH:
                x, h = resid(x, y_moe, mods, l, g1, False)
            else:
                (y_final,) = resid(x, y_moe, mods, l, final_norm_g[None], True)

    y_prompt = y_final[:ROWS_CTX].reshape(N_CTX, T_CTX, D)
    y_sample = y_final[ROWS_CTX:].reshape(N_LAT, T_LAT, D)
    return (y_prompt, y_sample,
            a_caches[0].reshape(N_CTX, n_even, T_CTX, A_KV_HEADS, HD),
            a_caches[1].reshape(N_CTX, n_even, T_CTX, A_KV_HEADS, HD),
            jnp.stack(new_sf, axis=1), jnp.stack(new_sb, axis=1),
            c_caches[0].reshape(N_CTX, n_odd, T_CTX, C_HEADS, HD),
            c_caches[1].reshape(N_CTX, n_odd, T_CTX, C_HEADS, HD))
```

```python
import functools

import numpy as np
import jax
import jax.numpy as jnp
from jax import lax
from jax.experimental import pallas as pl
from jax.experimental.pallas import tpu as pltpu

F32 = jnp.float32
BF16 = jnp.bfloat16
HIGHEST = lax.Precision.HIGHEST

D = 1024
N_CTX, T_CTX = 16, 256
N_LAT, T_LAT = 2, 1024
ROWS_CTX = N_CTX * T_CTX
ROWS_LAT = N_LAT * T_LAT
ROWS = ROWS_CTX + ROWS_LAT
DEPTH = 4
GRID_W = 64
HD = 64
A_HEADS, A_KV_HEADS, B_HEADS, C_HEADS = 8, 2, 8, 16
A_W, A_KV_W, B_W, C_W = A_HEADS * HD, A_KV_HEADS * HD, B_HEADS * HD, C_HEADS * HD
A_IN = A_W + 2 * A_KV_W
LORA_W, LORA_A, LORA_G = 64, 64, 128
B_IN = 3 * B_W + 2 * LORA_W + 2 * LORA_A + LORA_G
EVEN_IN = A_IN + B_IN
PAST = 512
NA_ROWS, NA_COLS = 8, 16
D_FF = 2816
N_EXPERTS = 8
D_EXPERT = 1408
ROPE_THETA = 10000.0
EPS = 1e-6
GN_EPS = 64e-5
NEG_INF = -1e30
ATT_SCALE = HD ** -0.5

LANES = 128
SUBLANES = 8
VMEM_LIMIT = 56 * 1024 * 1024

N_MODS = 8

ROW_SUB = T_LAT // T_CTX
ROW_GROUPS = ROWS // (ROW_SUB * T_CTX)
CTX_GROUPS = N_CTX // ROW_SUB


def _cparams(sem):
    return pltpu.CompilerParams(dimension_semantics=sem, vmem_limit_bytes=VMEM_LIMIT)


def _untouched():
    return pl.BlockSpec(memory_space=pl.ANY)


def _mod_index(row_start):
    return jnp.where(row_start < ROWS_CTX, 0, 1 + (row_start - ROWS_CTX) // T_LAT)


def _modspec(layer, tm, nargs=1):
    if nargs == 1:
        return pl.BlockSpec((None, 1, 6, D), lambda i: (layer, _mod_index(i * tm), 0, 0))
    return pl.BlockSpec((None, 1, 6, D), lambda i, j: (layer, _mod_index(i * tm), 0, 0))


def _gainspec(layer, nargs=1):
    if nargs == 1:
        return pl.BlockSpec((None, 1, D), lambda i: (layer, 0, 0))
    return pl.BlockSpec((None, 1, D), lambda i, j: (layer, 0, 0))


def _modnorm(x, g, shift, scale):
    ms = jnp.mean(x * x, axis=-1, keepdims=True)
    return (x * lax.rsqrt(ms + EPS) * g) * (1.0 + scale) + shift


def _silu(x):
    return x * jax.nn.sigmoid(x)


def _ada_kernel(c_ref, w_ref, b_ref, o_ref):
    s = _silu(c_ref[...]).astype(BF16)
    o_ref[0] = jnp.dot(s, w_ref[0].astype(BF16), preferred_element_type=F32) + b_ref[0]


def ada_all(cond, ada_w, ada_b):
    tn = 1536
    n = 6 * D
    return pl.pallas_call(
        _ada_kernel,
        grid=(DEPTH, n // tn),
        in_specs=[pl.BlockSpec((N_MODS, D), lambda l, j: (0, 0)),
                  pl.BlockSpec((1, D, tn), lambda l, j: (l, 0, j)),
                  pl.BlockSpec((1, 1, tn), lambda l, j: (l, 0, j))],
        out_specs=pl.BlockSpec((1, N_MODS, tn), lambda l, j: (l, 0, j)),
        out_shape=jax.ShapeDtypeStruct((DEPTH, N_MODS, n), F32),
        compiler_params=_cparams(("arbitrary", "arbitrary")),
        name="ada",
    )(cond, ada_w, ada_b.reshape(DEPTH, 1, n))


def _first_norm_kernel(xp_ref, xs_ref, g_ref, mod_ref, x_ref, h_ref, *, n_ctx_blocks):
    i = pl.program_id(0)

    def emit(src):
        x = src[...]
        x_ref[...] = x
        h_ref[...] = _modnorm(x, g_ref[...], mod_ref[0, 0:1, :], mod_ref[0, 1:2, :]).astype(BF16)

    @pl.when(i < n_ctx_blocks)
    def _():
        emit(xp_ref)

    @pl.when(i >= n_ctx_blocks)
    def _():
        emit(xs_ref)


def first_norm(x_prompt, x_sample, gains, mods):
    tm = 512
    nc = ROWS_CTX // tm
    rowspec = pl.BlockSpec((tm, D), lambda i: (i, 0))
    return pl.pallas_call(
        functools.partial(_first_norm_kernel, n_ctx_blocks=nc),
        grid=(ROWS // tm,),
        in_specs=[pl.BlockSpec((tm, D), lambda i: (jnp.minimum(i, nc - 1), 0)),
                  pl.BlockSpec((tm, D), lambda i: (jnp.maximum(i - nc, 0), 0)),
                  _gainspec(0), _modspec(0, tm)],
        out_specs=[rowspec, rowspec],
        out_shape=[jax.ShapeDtypeStruct((ROWS, D), F32), jax.ShapeDtypeStruct((ROWS, D), BF16)],
        compiler_params=_cparams(("arbitrary",)),
        name="first_norm",
    )(x_prompt.reshape(ROWS_CTX, D), x_sample.reshape(ROWS_LAT, D), gains, mods)


def _mm_in_kernel(h_ref, w_ref, o_ref, wb_ref):
    @pl.when(pl.program_id(1) == 0)
    def _():
        wb_ref[...] = w_ref[...].astype(BF16)

    o_ref[...] = jnp.dot(h_ref[...], wb_ref[...], preferred_element_type=F32)


def mm_in(h, w, layer, tn, tm=1024):
    _, k, n = w.shape
    return pl.pallas_call(
        _mm_in_kernel,
        grid=(n // tn, ROWS // tm),
        in_specs=[pl.BlockSpec((tm, k), lambda j, i: (i, 0)),
                  pl.BlockSpec((None, k, tn), lambda j, i: (layer, 0, j))],
        out_specs=pl.BlockSpec((tm, tn), lambda j, i: (i, j)),
        out_shape=jax.ShapeDtypeStruct((ROWS, n), F32),
        scratch_shapes=[pltpu.VMEM((k, tn), BF16)],
        compiler_params=_cparams(("arbitrary", "arbitrary")),
        name="mm_in",
    )(h, w)


def _residual_epilogue(x, acc, mod_ref, nmod_ref, g_ref, xo_ref, ho_ref, gate_idx, shift_idx, scale_idx, final):
    gate = mod_ref[0, gate_idx:gate_idx + 1, :]
    xn = x + gate * acc
    if final:
        ms = jnp.mean(xn * xn, axis=-1, keepdims=True)
        ho_ref[...] = xn * lax.rsqrt(ms + EPS) * g_ref[...]
    else:
        xo_ref[...] = xn
        shift = nmod_ref[0, shift_idx:shift_idx + 1, :]
        scale = nmod_ref[0, scale_idx:scale_idx + 1, :]
        ho_ref[...] = _modnorm(xn, g_ref[...], shift, scale).astype(BF16)


def _mm_out_kernel(*refs, n_parts, gate_idx, shift_idx, scale_idx):
    y_refs = refs[:n_parts]
    w_refs = refs[n_parts:2 * n_parts]
    x_ref, mod_ref, nmod_ref, g_ref, xo_ref, ho_ref = refs[2 * n_parts:2 * n_parts + 6]
    wb_refs = refs[2 * n_parts + 6:]

    @pl.when(pl.program_id(0) == 0)
    def _():
        for w_ref, wb_ref in zip(w_refs, wb_refs):
            wb_ref[...] = w_ref[...].astype(BF16)

    acc = None
    for y_ref, wb_ref in zip(y_refs, wb_refs):
        p = jnp.dot(y_ref[...], wb_ref[...], preferred_element_type=F32)
        acc = p if acc is None else acc + p
    _residual_epilogue(x_ref[...], acc, mod_ref, nmod_ref, g_ref, xo_ref, ho_ref,
                       gate_idx, shift_idx, scale_idx, False)


def mm_out(parts, w, layer_w, x, mods, layer, gains, gain_layer):
    tm = 512
    kp = parts[0].shape[1]
    n_parts = len(parts)
    rowspec = pl.BlockSpec((tm, D), lambda i: (i, 0))
    return pl.pallas_call(
        functools.partial(_mm_out_kernel, n_parts=n_parts, gate_idx=2, shift_idx=3, scale_idx=4),
        grid=(ROWS // tm,),
        in_specs=[pl.BlockSpec((tm, kp), lambda i: (i, 0)) for _ in parts]
                 + [pl.BlockSpec((None, kp, D), lambda i, p=p: (layer_w, p, 0)) for p in range(n_parts)]
                 + [rowspec, _modspec(layer, tm), _modspec(layer, tm), _gainspec(gain_layer)],
        out_specs=[rowspec, rowspec],
        out_shape=[jax.ShapeDtypeStruct((ROWS, D), F32), jax.ShapeDtypeStruct((ROWS, D), BF16)],
        scratch_shapes=[pltpu.VMEM((kp, D), BF16) for _ in parts],
        compiler_params=_cparams(("arbitrary",)),
        name="mm_out",
    )(*parts, *([w] * n_parts), x, mods, mods, gains)


FFN_TF = 256
FFN_NF = D_FF // FFN_TF


def _ffn_kernel(h_ref, wg_ref, wu_ref, wd_ref, x_ref, mod_ref, nmod_ref, g_ref, xo_ref, ho_ref,
                wg_s, wu_s, wd_s, acc_ref):
    i = pl.program_id(0)
    f = pl.program_id(1)

    @pl.when(i == 0)
    def _():
        wg_s[f] = wg_ref[...].astype(BF16)
        wu_s[f] = wu_ref[...].astype(BF16)
        wd_s[f] = wd_ref[...].astype(BF16)

    @pl.when(f == 0)
    def _():
        acc_ref[...] = jnp.zeros_like(acc_ref)

    h = h_ref[...]
    gp = jnp.dot(h, wg_s[f], preferred_element_type=F32)
    up = jnp.dot(h, wu_s[f], preferred_element_type=F32)
    a = (_silu(gp) * up).astype(BF16)
    acc_ref[...] += jnp.dot(a, wd_s[f], preferred_element_type=F32)

    @pl.when(f == FFN_NF - 1)
    def _():
        _residual_epilogue(x_ref[...], acc_ref[...], mod_ref, nmod_ref, g_ref, xo_ref, ho_ref, 5, 0, 1, False)


def ffn(h, w_gu, w_dn, e, x, mods, layer, gains):
    tm = 512
    tf, nf = FFN_TF, FFN_NF

    def once(i, f):
        return jnp.where(i == 0, f, nf - 1)

    rowspec = pl.BlockSpec((tm, D), lambda i, f: (i, 0))
    return pl.pallas_call(
        _ffn_kernel,
        grid=(ROWS // tm, nf),
        in_specs=[rowspec,
                  pl.BlockSpec((None, D, tf), lambda i, f: (e, 0, once(i, f))),
                  pl.BlockSpec((None, D, tf), lambda i, f: (e, 0, nf + once(i, f))),
                  pl.BlockSpec((None, tf, D), lambda i, f: (e, once(i, f), 0)),
                  rowspec, _modspec(layer, tm, 2), _modspec(layer + 1, tm, 2), _gainspec(layer + 1, 2)],
        out_specs=[rowspec, rowspec],
        out_shape=[jax.ShapeDtypeStruct((ROWS, D), F32), jax.ShapeDtypeStruct((ROWS, D), BF16)],
        scratch_shapes=[pltpu.VMEM((nf, D, tf), BF16),
                        pltpu.VMEM((nf, D, tf), BF16),
                        pltpu.VMEM((nf, tf, D), BF16),
                        pltpu.VMEM((tm, D), F32)],
        compiler_params=_cparams(("arbitrary", "arbitrary")),
        name="ffn",
    )(h, w_gu, w_gu, w_dn, x, mods, mods, gains)


MOE_TILE = 1024
MOE_CHUNK = 320


def _router_kernel(x_ref, g_ref, mod_ref, rw_ref, rb_ref, gates_ref, pos_ref, gates_t_ref, pos_t_ref, carry_ref):
    h = _modnorm(x_ref[...], g_ref[...], mod_ref[0, 3:4, :], mod_ref[0, 4:5, :])
    logits = jnp.dot(h, rw_ref[...], precision=HIGHEST, preferred_element_type=F32)
    lane = lax.broadcasted_iota(jnp.int32, logits.shape, 1)
    sel = jnp.where(lane < N_EXPERTS, logits + rb_ref[...], -jnp.inf)
    m1 = jnp.max(sel, axis=-1, keepdims=True)
    i1 = jnp.min(jnp.where(sel == m1, lane, LANES), axis=-1, keepdims=True)
    sel2 = jnp.where(lane == i1, -jnp.inf, sel)
    m2 = jnp.max(sel2, axis=-1, keepdims=True)
    i2 = jnp.min(jnp.where(sel2 == m2, lane, LANES), axis=-1, keepdims=True)
    l1 = jnp.sum(jnp.where(lane == i1, logits, 0.0), axis=-1, keepdims=True)
    l2 = jnp.sum(jnp.where(lane == i2, logits, 0.0), axis=-1, keepdims=True)
    mx = jnp.maximum(l1, l2)
    e1 = jnp.exp(l1 - mx)
    e2 = jnp.exp(l2 - mx)
    den = e1 + e2
    gates = jnp.where(lane == i1, e1 / den, 0.0) + jnp.where(lane == i2, e2 / den, 0.0)
    gates_ref[...] = gates

    tm = gates.shape[0]
    routed = gates > 0.0
    tri = (lax.broadcasted_iota(jnp.int32, (tm, tm), 0) > lax.broadcasted_iota(jnp.int32, (tm, tm), 1))
    local = jnp.dot(jnp.where(tri, 1.0, 0.0).astype(BF16), jnp.where(routed, 1.0, 0.0).astype(BF16),
                    preferred_element_type=F32)

    @pl.when(pl.program_id(0) % (MOE_TILE // tm) == 0)
    def _():
        carry_ref[...] = jnp.zeros_like(carry_ref)

    pos = local + carry_ref[...]
    carry_ref[...] += jnp.sum(jnp.where(routed, 1.0, 0.0), axis=0, keepdims=True)
    pos_ref[...] = pos
    gates_t_ref[...] = gates.T[:SUBLANES]
    pos_t_ref[...] = pos.T[:SUBLANES]


def router(x, gains, mods, layer, rw_p, rb_p, o):
    tm = 512
    rowspec = pl.BlockSpec((tm, LANES), lambda i: (i, 0))
    colspec = pl.BlockSpec((SUBLANES, tm), lambda i: (0, i))
    return pl.pallas_call(
        _router_kernel,
        grid=(ROWS // tm,),
        in_specs=[pl.BlockSpec((tm, D), lambda i: (i, 0)),
                  _gainspec(layer), _modspec(layer, tm),
                  pl.BlockSpec((None, D, LANES), lambda i: (o, 0, 0)),
                  pl.BlockSpec((None, 1, LANES), lambda i: (o, 0, 0))],
        out_specs=[rowspec, rowspec, colspec, colspec],
        out_shape=[jax.ShapeDtypeStruct((ROWS, LANES), F32), jax.ShapeDtypeStruct((ROWS, LANES), F32),
                   jax.ShapeDtypeStruct((SUBLANES, ROWS), F32), jax.ShapeDtypeStruct((SUBLANES, ROWS), F32)],
        scratch_shapes=[pltpu.VMEM((1, LANES), F32)],
        compiler_params=_cparams(("arbitrary",)),
        name="router",
    )(x, gains, mods, rw_p, rb_p)


def _moe_kernel(cnt_ref, h_ref, gates_ref, pos_ref, gates_t_ref, pos_t_ref, wgu_ref, wdn_ref, o_ref, acc_ref):
    tile = pl.program_id(0)
    e = pl.program_id(1)
    tt = h_ref.shape[0]

    @pl.when(e == 0)
    def _():
        acc_ref[...] = jnp.zeros_like(acc_ref)

    lane = lax.broadcasted_iota(jnp.int32, (tt, LANES), 1)
    g_col = jnp.sum(jnp.where(lane == e, gates_ref[...], 0.0), axis=1, keepdims=True)
    p_col = jnp.sum(jnp.where(lane == e, pos_ref[...], 0.0), axis=1, keepdims=True)
    sub = lax.broadcasted_iota(jnp.int32, (SUBLANES, tt), 0)
    g_row = jnp.sum(jnp.where(sub == e, gates_t_ref[...], 0.0), axis=0, keepdims=True)
    p_row = jnp.sum(jnp.where(sub == e, pos_t_ref[...], 0.0), axis=0, keepdims=True)
    p_col = jnp.where(g_col > 0.0, p_col, -1.0)
    p_row = jnp.where(g_row > 0.0, p_row, -1.0)

    def chunk(ci, carry):
        base = (ci * MOE_CHUNK).astype(F32)
        take = p_row == lax.broadcasted_iota(jnp.int32, (MOE_CHUNK, tt), 0).astype(F32) + base
        xs = jnp.dot(jnp.where(take, 1.0, 0.0).astype(BF16), h_ref[...], preferred_element_type=F32)
        gu = jnp.dot(xs.astype(BF16), wgu_ref[...], preferred_element_type=F32)
        a = (_silu(gu[:, :D_EXPERT]) * gu[:, D_EXPERT:]).astype(BF16)
        y = jnp.dot(a, wdn_ref[...], preferred_element_type=F32)
        gate = jnp.sum(jnp.where(take, g_row, 0.0), axis=1, keepdims=True)
        put = p_col == lax.broadcasted_iota(jnp.int32, (tt, MOE_CHUNK), 1).astype(F32) + base
        acc_ref[...] += jnp.dot(jnp.where(put, 1.0, 0.0).astype(BF16), (gate * y).astype(BF16),
                                preferred_element_type=F32)
        return carry

    count = cnt_ref[tile * N_EXPERTS + e]
    lax.fori_loop(0, (count + MOE_CHUNK - 1) // MOE_CHUNK, chunk, 0)

    @pl.when(e == N_EXPERTS - 1)
    def _():
        o_ref[...] = acc_ref[...]


def moe_routed(h, routing, w_gu, w_dn, o):
    gates, pos, gates_t, pos_t = routing
    tt = MOE_TILE
    n_tiles = ROWS // tt
    counts = jnp.sum((gates[:, :N_EXPERTS] > 0.0).reshape(n_tiles, tt, N_EXPERTS), axis=1)
    counts = counts.astype(jnp.int32).reshape(n_tiles * N_EXPERTS)
    rowspec = pl.BlockSpec((tt, LANES), lambda t, e, c: (t, 0))
    colspec = pl.BlockSpec((SUBLANES, tt), lambda t, e, c: (0, t))
    grid_spec = pltpu.PrefetchScalarGridSpec(
        num_scalar_prefetch=1,
        grid=(n_tiles, N_EXPERTS),
        in_specs=[pl.BlockSpec((tt, D), lambda t, e, c: (t, 0)),
                  rowspec, rowspec, colspec, colspec,
                  pl.BlockSpec((None, None, D, 2 * D_EXPERT), lambda t, e, c: (o, e, 0, 0)),
                  pl.BlockSpec((None, None, D_EXPERT, D), lambda t, e, c: (o, e, 0, 0))],
        out_specs=pl.BlockSpec((tt, D), lambda t, e, c: (t, 0)),
        scratch_shapes=[pltpu.VMEM((tt, D), F32)])
    return pl.pallas_call(
        _moe_kernel,
        grid_spec=grid_spec,
        out_shape=jax.ShapeDtypeStruct((ROWS, D), F32),
        compiler_params=_cparams(("arbitrary", "arbitrary")),
        name="moe",
    )(counts, h, gates, pos, gates_t, pos_t, w_gu, w_dn)


def _resid_kernel(x_ref, y_ref, mod_ref, nmod_ref, g_ref, *out_refs, final):
    if final:
        xo_ref, ho_ref = None, out_refs[0]
    else:
        xo_ref, ho_ref = out_refs
    _residual_epilogue(x_ref[...], y_ref[...], mod_ref, nmod_ref, g_ref, xo_ref, ho_ref, 5, 0, 1, final)


def resid(x, y, mods, layer, gains, final):
    tm = 512
    rowspec = pl.BlockSpec((tm, D), lambda i: (i, 0))
    if final:
        out_specs = [rowspec]
        out_shape = [jax.ShapeDtypeStruct((ROWS, D), F32)]
        nmod, gain = _modspec(layer, tm), pl.BlockSpec((1, D), lambda i: (0, 0))
    else:
        out_specs = [rowspec, rowspec]
        out_shape = [jax.ShapeDtypeStruct((ROWS, D), F32), jax.ShapeDtypeStruct((ROWS, D), BF16)]
        nmod, gain = _modspec(layer + 1, tm), _gainspec(layer + 1)
    return pl.pallas_call(
        functools.partial(_resid_kernel, final=final),
        grid=(ROWS // tm,),
        in_specs=[rowspec, rowspec, _modspec(layer, tm), nmod, gain],
        out_specs=out_specs,
        out_shape=out_shape,
        compiler_params=_cparams(("arbitrary",)),
        name="resid",
    )(x, y, mods, mods, gains)


def _softmax_pv(scores, values):
    m = None
    for s in scores:
        mi = jnp.max(s, axis=-1, keepdims=True)
        m = mi if m is None else jnp.maximum(m, mi)
    num, den = None, None
    for s, v in zip(scores, values):
        p = jnp.exp(s - m)
        li = jnp.sum(p, axis=-1, keepdims=True)
        oi = jnp.dot(p.astype(BF16), v, preferred_element_type=F32)
        num = oi if num is None else num + oi
        den = li if den is None else den + li
    return num / den


def _qk(q, k):
    return lax.dot_general(q, k, (((1,), (1,)), ((), ())), preferred_element_type=F32)


def _head_rmsnorm(x, gain, n_heads):
    parts = []
    for h in range(n_heads):
        xh = x[:, h * HD:(h + 1) * HD]
        ms = jnp.mean(xh * xh, axis=-1, keepdims=True)
        parts.append(xh * lax.rsqrt(ms + EPS) * gain)
    return parts


def _rope128(x, cos, sin_signed):
    lane = lax.broadcasted_iota(jnp.int32, x.shape, 1)
    up = pltpu.roll(x, LANES - 16, 1)
    dn = pltpu.roll(x, 16, 1)
    partner = jnp.where((lane % 32) < 16, up, dn)
    return x * cos + partner * sin_signed


def _rope_tables():
    t = np.arange(T_LAT)
    n_f = HD // 4
    inv = ROPE_THETA ** (-np.arange(n_f, dtype=np.float32) / n_f)
    cos = np.zeros((T_LAT, HD), np.float32)
    sin = np.zeros((T_LAT, HD), np.float32)
    for half, pos in ((0, t // GRID_W), (1, t % GRID_W)):
        ang = pos[:, None].astype(np.float32) * inv[None, :]
        c, s = np.cos(ang), np.sin(ang)
        base = half * (HD // 2)
        cos[:, base:base + n_f] = c
        cos[:, base + n_f:base + 2 * n_f] = c
        sin[:, base:base + n_f] = -s
        sin[:, base + n_f:base + 2 * n_f] = s
    return np.tile(cos, (1, 2)), np.tile(sin, (1, 2))


def _attn_a_ctx_kernel(q_ref, kv_ref, qg_ref, kg_ref, *rest):
    o_ref, ko_ref, vo_ref = rest[-3:]
    q = q_ref[...]
    kv = kv_ref[...]
    k = kv[:, :A_KV_W]
    v = kv[:, A_KV_W:]
    kn = _head_rmsnorm(k, kg_ref[...], A_KV_HEADS)
    ko_ref[...] = jnp.concatenate(kn, axis=-1)
    vo_ref[...] = v
    qn = _head_rmsnorm(q, qg_ref[...], A_HEADS)
    group = A_HEADS // A_KV_HEADS
    outs = []
    for h in range(A_HEADS):
        g = h // group
        kh = kn[g].astype(BF16)
        vh = v[:, g * HD:(g + 1) * HD].astype(BF16)
        qh = (qn[h] * ATT_SCALE).astype(BF16)
        outs.append(_softmax_pv([_qk(qh, kh)], [vh]))
    o_ref[...] = jnp.concatenate(outs, axis=-1).astype(BF16)


def attn_a_ctx(z, gq, gk, e, prev_caches):
    t = T_CTX
    n_even = gq.shape[0]
    cache_spec = pl.BlockSpec((None, None, t, A_KV_W), lambda b: (b, e, 0, 0))
    cache_shape = jax.ShapeDtypeStruct((N_CTX, n_even, t, A_KV_W), F32)
    extra = [] if prev_caches is None else list(prev_caches)
    aliases = {} if prev_caches is None else {4: 1, 5: 2}
    return pl.pallas_call(
        _attn_a_ctx_kernel,
        grid=(N_CTX,),
        in_specs=[pl.BlockSpec((t, A_W), lambda b: (b, 0)),
                  pl.BlockSpec((t, 2 * A_KV_W), lambda b: (b, A_W // (2 * A_KV_W))),
                  pl.BlockSpec((None, 1, HD), lambda b: (e, 0, 0)),
                  pl.BlockSpec((None, 1, HD), lambda b: (e, 0, 0))] + [_untouched() for _ in extra],
        out_specs=[pl.BlockSpec((t, A_W), lambda b: (b, 0)), cache_spec, cache_spec],
        out_shape=[jax.ShapeDtypeStruct((ROWS, A_W), BF16), cache_shape, cache_shape],
        input_output_aliases=aliases,
        compiler_params=_cparams(("arbitrary",)),
        name="attn_a_ctx",
    )(z, z, gq, gk, *extra)


A_LAT_TQ = 256


def _attn_a_lat_kernel(q_ref, kv_ref, ck_ref, cv_ref, qg_ref, kg_ref, cq_ref, sq_ref, ckk_ref, skk_ref,
                       prev_ref, o_ref, k_s, v_s):
    del prev_ref

    @pl.when(pl.program_id(1) == 0)
    def _():
        kv = kv_ref[...]
        kn = jnp.concatenate(_head_rmsnorm(kv[:, :A_KV_W], kg_ref[...], A_KV_HEADS), axis=-1)
        k_s[...] = _rope128(kn, ckk_ref[...], skk_ref[...]).astype(BF16)
        v_s[...] = kv[:, A_KV_W:].astype(BF16)

    qn = _head_rmsnorm(q_ref[...], qg_ref[...], A_HEADS)
    cos, sin = cq_ref[...], sq_ref[...]
    qr = []
    for c in range(A_HEADS // 2):
        slab = _rope128(jnp.concatenate(qn[2 * c:2 * c + 2], axis=-1), cos, sin)
        qr.append(slab[:, :HD])
        qr.append(slab[:, HD:])
    ck = ck_ref[...].astype(BF16)
    cv = cv_ref[...].astype(BF16)
    kk = k_s[...]
    vv = v_s[...]
    group = A_HEADS // A_KV_HEADS
    outs = []
    for h in range(A_HEADS):
        g = h // group
        sl = slice(g * HD, (g + 1) * HD)
        qh = (qr[h] * ATT_SCALE).astype(BF16)
        outs.append(_softmax_pv([_qk(qh, kk[:, sl]), _qk(qh, ck[:, sl])], [vv[:, sl], cv[:, sl]]))
    o_ref[...] = jnp.concatenate(outs, axis=-1).astype(BF16)


def attn_a_lat(z, ck, cv, gq, gk, e, y_prev):
    tq = A_LAT_TQ
    nq = T_LAT // tq
    cos, sin = _rope_tables()
    cos, sin = jnp.asarray(cos), jnp.asarray(sin)
    row0 = ROWS_CTX // tq
    seq0 = ROWS_CTX // T_LAT
    cache_spec = pl.BlockSpec((None, None, PAST, A_KV_W), lambda b, i: (b, e, 0, 0))
    return pl.pallas_call(
        _attn_a_lat_kernel,
        grid=(N_LAT, nq),
        in_specs=[pl.BlockSpec((tq, A_W), lambda b, i: (row0 + b * nq + i, 0)),
                  pl.BlockSpec((T_LAT, 2 * A_KV_W), lambda b, i: (seq0 + b, A_W // (2 * A_KV_W))),
                  cache_spec, cache_spec,
                  pl.BlockSpec((None, 1, HD), lambda b, i: (e, 0, 0)),
                  pl.BlockSpec((None, 1, HD), lambda b, i: (e, 0, 0)),
                  pl.BlockSpec((tq, LANES), lambda b, i: (i, 0)),
                  pl.BlockSpec((tq, LANES), lambda b, i: (i, 0)),
                  pl.BlockSpec((T_LAT, LANES), lambda b, i: (0, 0)),
                  pl.BlockSpec((T_LAT, LANES), lambda b, i: (0, 0)),
                  _untouched()],
        out_specs=pl.BlockSpec((tq, A_W), lambda b, i: (row0 + b * nq + i, 0)),
        out_shape=jax.ShapeDtypeStruct((ROWS, A_W), BF16),
        scratch_shapes=[pltpu.VMEM((T_LAT, A_KV_W), BF16),
                        pltpu.VMEM((T_LAT, A_KV_W), BF16)],
        input_output_aliases={10: 0},
        compiler_params=_cparams(("arbitrary", "arbitrary")),
        name="attn_a_lat",
    )(z, z, ck, cv, gq, gk, cos, sin, cos, sin, y_prev)


def _attn_c_ctx_kernel(q_ref, k_ref, v_ref, *rest):
    o_ref, ko_ref, vo_ref = rest[-3:]
    q = q_ref[...]
    k = k_ref[...]
    v = v_ref[...]
    ko_ref[...] = k
    vo_ref[...] = v
    outs = []
    for h in range(q.shape[1] // HD):
        sl = slice(h * HD, (h + 1) * HD)
        qh = (q[:, sl] * ATT_SCALE).astype(BF16)
        outs.append(_softmax_pv([_qk(qh, k[:, sl].astype(BF16))], [v[:, sl].astype(BF16)]))
    o_ref[...] = jnp.concatenate(outs, axis=-1).astype(BF16)


C_CTX_HEADS_PER_STEP = 8


def attn_c_ctx(z, o, n_odd, prev_caches):
    t = T_CTX
    wblk = C_CTX_HEADS_PER_STEP * HD
    nhp = C_W // wblk
    cache_spec = pl.BlockSpec((None, None, t, wblk), lambda b, p: (b, o, 0, p))
    cache_shape = jax.ShapeDtypeStruct((N_CTX, n_odd, t, C_W), F32)
    extra = [] if prev_caches is None else list(prev_caches)
    aliases = {} if prev_caches is None else {3: 1, 4: 2}
    return pl.pallas_call(
        _attn_c_ctx_kernel,
        grid=(N_CTX, nhp),
        in_specs=[pl.BlockSpec((t, wblk), lambda b, p: (b, p)),
                  pl.BlockSpec((t, wblk), lambda b, p: (b, nhp + p)),
                  pl.BlockSpec((t, wblk), lambda b, p: (b, 2 * nhp + p))] + [_untouched() for _ in extra],
        out_specs=[pl.BlockSpec((t, wblk), lambda b, p: (b, p)), cache_spec, cache_spec],
        out_shape=[jax.ShapeDtypeStruct((ROWS, C_W), BF16), cache_shape, cache_shape],
        input_output_aliases=aliases,
        compiler_params=_cparams(("arbitrary", "arbitrary")),
        name="attn_c_ctx",
    )(z, z, z, *extra)


NA_GRID_ROWS = T_LAT // GRID_W
NA_WIN = NA_ROWS * GRID_W
NA_DR = 2 * NA_ROWS - 1
NA_DC = 2 * NA_COLS - 1


def _na_row_start(r):
    return min(max(r - NA_ROWS // 2, 0), NA_GRID_ROWS - NA_ROWS)


def _na_groups():
    groups, r = [], 0
    while r < NA_GRID_ROWS:
        r1 = r
        while r1 + 1 < NA_GRID_ROWS and (_na_row_start(r1 + 1) == _na_row_start(r) or r1 + 1 - r < 4):
            r1 += 1
        lo = _na_row_start(r)
        hi = _na_row_start(r1) + NA_ROWS
        pairs = -(-(hi - lo) // 2)
        lo = min(lo, NA_GRID_ROWS - 2 * pairs)
        groups.append((r, r1, lo, pairs))
        r = r1 + 1
    return groups


def _na_pair_codes():
    codes = []
    plan = []
    for r0, r1, lo, pairs in _na_groups():
        rows = []
        for r in range(r0, r1 + 1):
            rs = _na_row_start(r)
            row = []
            for p in range(pairs):
                code = tuple((kr - r + NA_ROWS - 1) if rs <= kr < rs + NA_ROWS else None
                             for kr in (lo + 2 * p, lo + 2 * p + 1))
                if code not in codes:
                    codes.append(code)
                row.append(codes.index(code))
            rows.append(row)
        plan.append(rows)
    return codes, plan


def _na_bias_table(rpb):
    n_l = rpb.shape[0]
    col = np.arange(GRID_W)
    cs = np.clip(col - NA_COLS // 2, 0, GRID_W - NA_COLS)
    col_in = (col[None, :] >= cs[:, None]) & (col[None, :] < cs[:, None] + NA_COLS)
    period = GRID_W + 1
    seq = jnp.concatenate([rpb, jnp.zeros((n_l, C_HEADS, NA_DR, period - NA_DC), F32)], axis=-1)
    seq = jnp.roll(seq, -(NA_COLS - 1), axis=-1)
    tile = jnp.tile(seq, (1, 1, 1, GRID_W))[..., :GRID_W * GRID_W].reshape(n_l, C_HEADS, NA_DR, GRID_W, GRID_W)
    tile = jnp.where(jnp.asarray(col_in), tile, NEG_INF)
    masked = jnp.full((n_l, C_HEADS, GRID_W, GRID_W), NEG_INF, F32)
    pick = lambda dr: masked if dr is None else tile[:, :, dr]
    codes, _ = _na_pair_codes()
    return jnp.stack([jnp.concatenate([pick(a), pick(b)], axis=-1) for a, b in codes], axis=2)


def _attn_na_kernel(q_ref, k_ref, v_ref, ck_ref, cv_ref, bias_ref, prev_ref, o_ref):
    del prev_ref
    _, plan = _na_pair_codes()
    heads = []
    for h in range(LANES // HD):
        sl = slice(h * HD, (h + 1) * HD)
        q = (q_ref[:, sl] * ATT_SCALE).astype(BF16)
        k = k_ref[:, sl].astype(BF16)
        v = v_ref[:, sl].astype(BF16)
        ck = ck_ref[:, sl].astype(BF16)
        cv = cv_ref[:, sl].astype(BF16)
        rows = []
        for (r0, r1, lo, pairs), codes in zip(_na_groups(), plan):
            qg = q[r0 * GRID_W:(r1 + 1) * GRID_W]
            kw = k[lo * GRID_W:(lo + 2 * pairs) * GRID_W]
            vw = v[lo * GRID_W:(lo + 2 * pairs) * GRID_W]
            bias = jnp.concatenate(
                [jnp.concatenate([bias_ref[h, c] for c in row], axis=-1) for row in codes], axis=0)
            s_nb = _qk(qg, kw) + bias
            s_cx = _qk(qg, ck)
            rows.append(_softmax_pv([s_nb, s_cx], [vw, cv]))
        heads.append(jnp.concatenate(rows, axis=0))
    o_ref[...] = jnp.concatenate(heads, axis=-1).astype(BF16)


def attn_na(z, ck, cv, bias, o, y_prev):
    nhp = C_W // LANES
    hpb = LANES // HD
    seq0 = ROWS_CTX // T_LAT
    cache_spec = pl.BlockSpec((None, None, PAST, LANES), lambda p, b: (b, o, 0, p))
    return pl.pallas_call(
        _attn_na_kernel,
        grid=(nhp, N_LAT),
        in_specs=[pl.BlockSpec((T_LAT, LANES), lambda p, b: (seq0 + b, p)),
                  pl.BlockSpec((T_LAT, LANES), lambda p, b: (seq0 + b, nhp + p)),
                  pl.BlockSpec((T_LAT, LANES), lambda p, b: (seq0 + b, 2 * nhp + p)),
                  cache_spec, cache_spec,
                  pl.BlockSpec((None, hpb, bias.shape[2], GRID_W, LANES), lambda p, b: (o, p, 0, 0, 0)),
                  _untouched()],
        out_specs=pl.BlockSpec((T_LAT, LANES), lambda p, b: (seq0 + b, p)),
        out_shape=jax.ShapeDtypeStruct((ROWS, C_W), BF16),
        input_output_aliases={6: 0},
        compiler_params=_cparams(("arbitrary", "arbitrary")),
        name="attn_na",
    )(z, z, z, ck, cv, bias, y_prev)


def _seg_sum(x, n_heads):
    parts = []
    for h in range(n_heads):
        s = jnp.sum(x[:, h * HD:(h + 1) * HD], axis=-1, keepdims=True)
        parts.append(jnp.broadcast_to(s, (x.shape[0], HD)))
    return jnp.concatenate(parts, axis=-1)


PREP_TM = 256


def _seq_len_at(row_start):
    return jnp.where(row_start < ROWS_CTX, T_CTX, T_LAT)


def _rwkv_prep_kernel(z_ref, zprev_ref, znext_ref, mu_ref, kkw_ref, w0_ref, w2_ref, a0_ref, a2_ref, ka_ref,
                      rk_ref, g2_ref, o_rk_ref, o_wk_ref, o_bv_ref, g_ref, bonus_ref):
    z = z_ref[:, A_IN:]
    t = z.shape[0]
    start = pl.program_id(0) * t
    seq_len = _seq_len_at(start)
    pos = (start - jnp.where(start < ROWS_CTX, 0, ROWS_CTX)) % seq_len
    halo_prev = jnp.where(pos == 0, 0.0, zprev_ref[SUBLANES - 1:SUBLANES, A_IN:])
    halo_next = jnp.where(pos + t == seq_len, 0.0, znext_ref[0:1, A_IN:])
    row = lax.broadcasted_iota(jnp.int32, (t, 1), 0)
    prev = jnp.where(row == 0, halo_prev, pltpu.roll(z, 1, 0))
    nxt = jnp.where(row == t - 1, halo_next, pltpu.roll(z, t - 1, 0))
    m = z + mu_ref[...] * (0.5 * (prev + nxt) - z)
    r = m[:, :B_W]
    k = m[:, B_W:2 * B_W]
    v = m[:, 2 * B_W:3 * B_W]
    o = 3 * B_W
    wd = m[:, o:o + 2 * LORA_W]
    ad = m[:, o + 2 * LORA_W:o + 2 * LORA_W + 2 * LORA_A]
    gd = m[:, o + 2 * LORA_W + 2 * LORA_A:]

    kkr = k * kkw_ref[...]
    kk = kkr * lax.rsqrt(_seg_sum(kkr * kkr, B_HEADS) + 1e-12)

    wl = w0_ref[...] + jnp.dot(jnp.tanh(wd), w2_ref[...], precision=HIGHEST, preferred_element_type=F32)
    decay = jnp.exp(-float(np.exp(-0.5)) * jax.nn.sigmoid(wl))
    a = jax.nn.sigmoid(a0_ref[...] + jnp.dot(ad, a2_ref[...], precision=HIGHEST, preferred_element_type=F32))
    k2 = jnp.concatenate([k, k], axis=-1)
    ka2 = jnp.concatenate([ka_ref[...], ka_ref[...]], axis=-1)
    kk2 = jnp.concatenate([kk, kk], axis=-1)

    kd = k2 * (1.0 + (a - 1.0) * ka2)
    bb = kk2 * a
    for h in range(B_HEADS):
        sl = slice(h * HD, (h + 1) * HD)
        o_rk_ref[pl.ds(h, t, stride=B_HEADS), :] = jnp.concatenate([r[:, sl], kk[:, sl]], axis=-1)
        for dr in range(2):
            sd = slice(dr * B_W + h * HD, dr * B_W + (h + 1) * HD)
            row = pl.ds(dr * B_HEADS + h, t, stride=2 * B_HEADS)
            o_wk_ref[row, :] = jnp.concatenate([decay[:, sd], kd[:, sd]], axis=-1)
            o_bv_ref[row, :] = jnp.concatenate([bb[:, sd], v[:, sl]], axis=-1)
    g_ref[...] = jnp.dot(jax.nn.sigmoid(gd).astype(BF16), g2_ref[...].astype(BF16), preferred_element_type=F32)
    bonus_ref[...] = _seg_sum(r * k * rk_ref[...], B_HEADS) * v


def rwkv_prep(z, params):
    tm = PREP_TM
    full = lambda a: pl.BlockSpec(a.shape, lambda s: (0,) * a.ndim)
    out_rows = (B_HEADS, 2 * B_HEADS, 2 * B_HEADS)
    per = tm // SUBLANES
    last = ROWS // SUBLANES - 1
    rowspec = pl.BlockSpec((tm, B_W), lambda s: (s, 0))
    return pl.pallas_call(
        _rwkv_prep_kernel,
        grid=(ROWS // tm,),
        in_specs=[pl.BlockSpec((tm, EVEN_IN), lambda s: (s, 0)),
                  pl.BlockSpec((SUBLANES, EVEN_IN), lambda s: (jnp.maximum(s * per - 1, 0), 0)),
                  pl.BlockSpec((SUBLANES, EVEN_IN), lambda s: (jnp.minimum((s + 1) * per, last), 0))]
                 + [full(p) for p in params],
        out_specs=[pl.BlockSpec((tm * n, LANES), lambda s: (s, 0)) for n in out_rows] + [rowspec, rowspec],
        out_shape=[jax.ShapeDtypeStruct((ROWS * n, LANES), F32) for n in out_rows]
                  + [jax.ShapeDtypeStruct((ROWS, B_W), F32)] * 2,
        compiler_params=_cparams(("arbitrary",)),
        name="rwkv_prep",
    )(z, z, z, *params)


def rwkv_prep_all(z, prm, e):
    w2 = prm['b_w2'][e]
    a2 = prm['b_a2'][e]
    zero = jnp.zeros((LORA_W, B_W), F32)
    w2bd = jnp.concatenate([jnp.concatenate([w2[0], zero], 1), jnp.concatenate([zero, w2[1]], 1)], 0)
    a2bd = jnp.concatenate([jnp.concatenate([a2[0], zero], 1), jnp.concatenate([zero, a2[1]], 1)], 0)
    params = (prm['b_mu'][e][None], prm['b_kk'][e][None], prm['b_w0'][e].reshape(1, 2 * B_W), w2bd,
              prm['b_a0'][e].reshape(1, 2 * B_W), a2bd, prm['b_ka'][e][None],
              prm['b_rk'][e].reshape(1, B_W), prm['b_g2'][e])
    return rwkv_prep(z, params)


SCAN_CHAINS = 8


def _tree_sum(xs):
    while len(xs) > 1:
        xs = [xs[i] + xs[i + 1] for i in range(0, len(xs) - 1, 2)] + ([xs[-1]] if len(xs) % 2 else [])
    return xs[0]


def _scan_step(rk_b, wk_b, bv_b, vv, s_ref, ni):
    n_acc = max(1, SCAN_CHAINS // ni)

    def bcast(ref, row):
        return jnp.broadcast_to(ref[pl.ds(row, 1), :], (SUBLANES, LANES))

    sa = [[None] * n_acc for _ in range(ni)]
    for j in range(HD):
        kkj = bcast(rk_b, HD + j)
        for g in range(ni):
            p = s_ref[j, pl.ds(g * SUBLANES, SUBLANES), :] * kkj
            a = j % n_acc
            sa[g][a] = p if sa[g][a] is None else sa[g][a] + p
    sa = [-_tree_sum(x) for x in sa]
    ya = [[None] * n_acc for _ in range(ni)]
    for j in range(HD):
        wj = bcast(wk_b, j)
        kj = bcast(wk_b, HD + j)
        bj = bcast(bv_b, j)
        rj = bcast(rk_b, j)
        for g in range(ni):
            sl = pl.ds(g * SUBLANES, SUBLANES)
            s_new = s_ref[j, sl, :] * wj + sa[g] * bj + vv[g] * kj
            s_ref[j, sl, :] = s_new
            p = s_new * rj
            a = j % n_acc
            ya[g][a] = p if ya[g][a] is None else ya[g][a] + p
    return [_tree_sum(ya[g]) for g in range(ni)]


def _scan_pairs(tc, relayout, run):
    relayout(0, 0)

    def pair(k, carry):
        i0 = 2 * k
        relayout(i0 + 1, 1)
        run(i0, 0)
        relayout(jnp.minimum(i0 + 2, tc - 1), 0)
        run(i0 + 1, 1)
        return carry

    lax.fori_loop(0, tc // 2, pair, 0)


def _scan_ctx_kernel(rk_ref, wk_ref, bv_ref, y_ref, st_ref, rk0, wk0, bv0, rk1, wk1, bv1, s_ref, *, tc):
    d = pl.program_id(0)
    c = pl.program_id(1)
    ni = HD // SUBLANES
    bufs = ((rk0, wk0, bv0), (rk1, wk1, bv1))

    @pl.when(c == 0)
    def _():
        s_ref[...] = jnp.zeros_like(s_ref)

    def t_of(i):
        return i + d * (tc - 1 - 2 * i)

    def relayout(i, slot):
        t = t_of(i)
        for src, dst in zip((rk_ref, wk_ref, bv_ref), bufs[slot]):
            dst[...] = src[:, :, t].reshape(LANES, LANES).T

    def run(i, slot):
        rk_b, wk_b, bv_b = bufs[slot]
        vv = [bv_b[pl.ds(HD + g * SUBLANES, SUBLANES), :] for g in range(ni)]
        y = jnp.concatenate(_scan_step(rk_b, wk_b, bv_b, vv, s_ref, ni), axis=0)
        y_ref[:, :, t_of(i)] = y.T.reshape(CTX_GROUPS, ROW_SUB, B_HEADS, HD)

    _scan_pairs(tc, relayout, run)

    @pl.when(c == pl.num_programs(1) - 1)
    def _():
        st_ref[...] = s_ref[...]


def _rows5(x, heads):
    return x.reshape(ROW_GROUPS, ROW_SUB, T_CTX, heads, x.shape[-1])


def _operand_bufs():
    return [pltpu.VMEM((LANES, LANES), F32)] * 6


def rwkv_scan_ctx(prep, tc):
    rk, wk, bv = prep[:3]
    nc = T_CTX // tc

    def chunk(d, c):
        return c + d * (nc - 1 - 2 * c)

    blk = (CTX_GROUPS, ROW_SUB, tc, B_HEADS, LANES)
    shared = pl.BlockSpec(blk, lambda d, c: (0, 0, chunk(d, c), 0, 0))
    perdir = pl.BlockSpec(blk, lambda d, c: (0, 0, chunk(d, c), d, 0))
    yspec = pl.BlockSpec((CTX_GROUPS, ROW_SUB, tc, B_HEADS, HD), lambda d, c: (0, 0, chunk(d, c), d, 0))
    return pl.pallas_call(
        functools.partial(_scan_ctx_kernel, tc=tc),
        grid=(2, nc),
        in_specs=[shared, perdir, perdir],
        out_specs=[yspec, pl.BlockSpec((None, HD, HD, LANES), lambda d, c: (d, 0, 0, 0))],
        out_shape=[jax.ShapeDtypeStruct((ROW_GROUPS, ROW_SUB, T_CTX, 2 * B_HEADS, HD), F32),
                   jax.ShapeDtypeStruct((2, HD, HD, LANES), F32)],
        scratch_shapes=_operand_bufs() + [pltpu.VMEM((HD, HD, LANES), F32)],
        compiler_params=_cparams(("arbitrary", "arbitrary")),
        name="rwkv_scan_ctx",
    )(_rows5(rk, B_HEADS), _rows5(wk, 2 * B_HEADS), _rows5(bv, 2 * B_HEADS))


LAT_REP = LANES // (N_LAT * B_HEADS)


def _scan_lat_kernel(rk_ref, wk_ref, bv_ref, s0_ref, yprev_ref, y_ref, rk0, wk0, bv0, rk1, wk1, bv1, s_ref,
                     y_s, *, tc):
    del yprev_ref
    d = pl.program_id(0)
    c = pl.program_id(1)
    n = N_LAT * B_HEADS
    bufs = ((rk0, wk0, bv0), (rk1, wk1, bv1))

    @pl.when(c == 0)
    def _():
        s_ref[...] = s0_ref[...]

    def t_of(i):
        return i + d * (tc - 1 - 2 * i)

    def relayout(i, slot):
        t = t_of(i)
        for src, dst in zip((rk_ref, wk_ref, bv_ref), bufs[slot]):
            m = src[:, 0, t].reshape(n, LANES)
            dst[...] = jnp.concatenate([m] * LAT_REP, axis=0).T

    def run(i, slot):
        rk_b, wk_b, bv_b = bufs[slot]
        group = lax.broadcasted_iota(jnp.int32, (SUBLANES, LANES), 1) // n
        v = bv_b[pl.ds(HD, SUBLANES), :]
        for g in range(1, LAT_REP):
            v = jnp.where(group == g, bv_b[pl.ds(HD + g * SUBLANES, SUBLANES), :], v)
        (y_s[t_of(i)],) = _scan_step(rk_b, wk_b, bv_b, [v], s_ref, 1)

    _scan_pairs(tc, relayout, run)

    def store(t, carry):
        y = y_s[t]
        rows = [y] + [pltpu.roll(y, LANES - g * n, 1) for g in range(1, LAT_REP)]
        yt = jnp.concatenate(rows, axis=0).T
        y_ref[:, 0, t] = yt[:n].reshape(N_LAT, B_HEADS, HD)
        return carry

    lax.fori_loop(0, tc, store, 0, unroll=8)


def rwkv_scan_lat(prep, s0f, s0b, y_prev, tc):
    rk, wk, bv = prep[:3]
    nc = T_LAT // tc
    per_sub = T_CTX // tc
    g0 = CTX_GROUPS // N_LAT

    def pos(d, c):
        cc = c + d * (nc - 1 - 2 * c)
        return cc // per_sub, cc % per_sub

    def shared_map(d, c):
        q, off = pos(d, c)
        return (g0, q, off, 0, 0)

    def perdir_map(d, c):
        q, off = pos(d, c)
        return (g0, q, off, d, 0)

    blk = (N_LAT, 1, tc, B_HEADS, LANES)
    s0 = jnp.stack([s0f, s0b]).reshape(2, N_LAT, B_HEADS, LAT_REP, SUBLANES, HD)
    s0 = s0.transpose(0, 5, 4, 3, 1, 2).reshape(2, HD, SUBLANES, LANES)
    return pl.pallas_call(
        functools.partial(_scan_lat_kernel, tc=tc),
        grid=(2, nc),
        in_specs=[pl.BlockSpec(blk, shared_map), pl.BlockSpec(blk, perdir_map), pl.BlockSpec(blk, perdir_map),
                  pl.BlockSpec((None, HD, SUBLANES, LANES), lambda d, c: (d, 0, 0, 0)),
                  _untouched()],
        out_specs=pl.BlockSpec((N_LAT, 1, tc, B_HEADS, HD), perdir_map),
        out_shape=jax.ShapeDtypeStruct((ROW_GROUPS, ROW_SUB, T_CTX, 2 * B_HEADS, HD), F32),
        scratch_shapes=_operand_bufs() + [pltpu.VMEM((HD, SUBLANES, LANES), F32),
                                          pltpu.VMEM((tc, SUBLANES, LANES), F32)],
        input_output_aliases={4: 0},
        compiler_params=_cparams(("arbitrary", "arbitrary")),
        name="rwkv_scan_lat",
    )(_rows5(rk, B_HEADS), _rows5(wk, 2 * B_HEADS), _rows5(bv, 2 * B_HEADS), s0, y_prev)


def _rwkv_post_kernel(y_ref, bonus_ref, g_ref, lng_ref, lnb_ref, o_ref):
    tm = o_ref.shape[0]
    outs = []
    for h in range(B_HEADS):
        sl = slice(h * HD, (h + 1) * HD)
        y = (y_ref[pl.ds(h, tm, stride=2 * B_HEADS), :]
             + y_ref[pl.ds(B_HEADS + h, tm, stride=2 * B_HEADS), :])
        yc = y - jnp.mean(y, axis=-1, keepdims=True)
        var = jnp.mean(yc * yc, axis=-1, keepdims=True)
        yn = yc * lax.rsqrt(var + GN_EPS) * lng_ref[:, sl] + lnb_ref[:, sl]
        outs.append((yn + bonus_ref[:, sl]) * g_ref[:, sl])
    o_ref[...] = jnp.concatenate(outs, axis=-1).astype(BF16)


def rwkv_post(y, bonus, g, lng, lnb, e):
    tm = 256
    rowspec = pl.BlockSpec((tm, B_W), lambda i: (i, 0))
    vecspec = pl.BlockSpec((None, 1, B_W), lambda i: (e, 0, 0))
    return pl.pallas_call(
        _rwkv_post_kernel,
        grid=(ROWS // tm,),
        in_specs=[pl.BlockSpec((tm * 2 * B_HEADS, HD), lambda i: (i, 0)), rowspec, rowspec, vecspec, vecspec],
        out_specs=rowspec,
        out_shape=jax.ShapeDtypeStruct((ROWS, B_W), BF16),
        compiler_params=_cparams(("arbitrary",)),
        name="rwkv_post",
    )(y, bonus, g, lng, lnb)


def kernel(x_prompt, x_sample, cache_a_k, cache_a_v, state_b_fwd, state_b_bwd, cache_c_k, cache_c_v, c, c_ctx,
           ada_w, ada_b, norm1_g, norm2_g, final_norm_g, w_in_e, w_out_e, a_q_gain, a_k_gain, b_mu, b_w0, b_w2,
           b_a0, b_a2, b_g2, b_kk, b_ka, b_rk, b_ln_g, b_ln_b, ffn_w_gu, ffn_w_dn, w_in_o, w_out_o, c_rpb,
           router_w, router_b, moe_w_gu, moe_w_dn):
    n_even, n_odd = w_in_e.shape[0], w_in_o.shape[0]
    prm = dict(b_mu=b_mu, b_w0=b_w0, b_w2=b_w2, b_a0=b_a0, b_a2=b_a2, b_g2=b_g2, b_kk=b_kk, b_ka=b_ka,
               b_rk=b_rk)
    cond = jnp.zeros((N_MODS, D), F32).at[0].set(c_ctx).at[1:1 + N_LAT].set(c)
    mods = ada_all(cond, ada_w, ada_b).reshape(DEPTH, N_MODS, 6, D)
    g1 = norm1_g.reshape(DEPTH, 1, D)
    g2 = norm2_g.reshape(DEPTH, 1, D)
    gq = a_q_gain.reshape(n_even, 1, HD)
    gk = a_k_gain.reshape(n_even, 1, HD)
    lng = b_ln_g.reshape(n_even, 1, B_W)
    lnb = b_ln_b.reshape(n_even, 1, B_W)
    ck_a = cache_a_k.reshape(N_LAT, n_even, PAST, A_KV_W)
    cv_a = cache_a_v.reshape(N_LAT, n_even, PAST, A_KV_W)
    ck_c = cache_c_k.reshape(N_LAT, n_odd, PAST, C_W)
    cv_c = cache_c_v.reshape(N_LAT, n_odd, PAST, C_W)
    na_bias = _na_bias_table(c_rpb)
    rw_p = jnp.zeros((n_odd, D, LANES), F32).at[:, :, :N_EXPERTS].set(router_w)
    rb_p = jnp.zeros((n_odd, 1, LANES), F32).at[:, 0, :N_EXPERTS].set(router_b)
    moe_gu_b = moe_w_gu.astype(BF16)
    moe_dn_b = moe_w_dn.astype(BF16)

    x, h = first_norm(x_prompt, x_sample, g1, mods)

    a_caches, c_caches = None, None
    new_sf, new_sb = [], []
    y_final = None
    for l in range(DEPTH):
        if l % 2 == 0:
            e = l // 2
            z = mm_in(h, w_in_e, e, tn=EVEN_IN // 3)
            y_a, k_new, v_new = attn_a_ctx(z, gq, gk, e, a_caches)
            a_caches = (k_new, v_new)
            y_a = attn_a_lat(z, ck_a, cv_a, gq, gk, e, y_a)
            prep = rwkv_prep_all(z, prm, e)
            y_s, st = rwkv_scan_ctx(prep, tc=32)
            y_s = rwkv_scan_lat(prep, state_b_fwd[:, e], state_b_bwd[:, e], y_s, tc=64)
            st = st.reshape(2, HD, HD, N_CTX, B_HEADS).transpose(0, 3, 4, 2, 1)
            new_sf.append(st[0])
            new_sb.append(st[1])
            y_b = rwkv_post(y_s.reshape(ROWS * 2 * B_HEADS, HD), prep[4], prep[3], lng, lnb, e)
            x, h = mm_out([y_a, y_b], w_out_e, e, x, mods, l, g2, l)
            x, h = ffn(h, ffn_w_gu, ffn_w_dn, e, x, mods, l, g1)
        else:
            o = l // 2
            z = mm_in(h, w_in_o, o, tn=C_W)
            y, k_new, v_new = attn_c_ctx(z, o, n_odd, c_caches)
            c_caches = (k_new, v_new)
            y = attn_na(z, ck_c, cv_c, na_bias, o, y)
            x, h = mm_out([y], w_out_o, o, x, mods, l, g2, l)
            routing = router(x, g2, mods, l, rw_p, rb_p, o)
            y_moe = moe_routed(h, routing, moe_gu_b, moe_dn_b, o)
            if l + 1 < DEPTH:
                x, h = resid(x, y_moe, mods, l, g1, False)
            else:
                (y_final,) = resid(x, y_moe, mods, l, final_norm_g[None], True)

    y_prompt = y_final[:ROWS_CTX].reshape(N_CTX, T_CTX, D)
    y_sample = y_final[ROWS_CTX:].reshape(N_LAT, T_LAT, D)
    return (y_prompt, y_sample,
            a_caches[0].reshape(N_CTX, n_even, T_CTX, A_KV_HEADS, HD),
            a_caches[1].reshape(N_CTX, n_even, T_CTX, A_KV_HEADS, HD),
            jnp.stack(new_sf, axis=1), jnp.stack(new_sb, axis=1),
            c_caches[0].reshape(N_CTX, n_odd, T_CTX, C_HEADS, HD),
            c_caches[1].reshape(N_CTX, n_odd, T_CTX, C_HEADS, HD))
```

```python
import functools

import numpy as np
import jax
import jax.numpy as jnp
from jax import lax
from jax.experimental import pallas as pl
from jax.experimental.pallas import tpu as pltpu

F32 = jnp.float32
BF16 = jnp.bfloat16
HIGHEST = lax.Precision.HIGHEST

D = 1024
N_CTX, T_CTX = 16, 256
N_LAT, T_LAT = 2, 1024
ROWS_CTX = N_CTX * T_CTX
ROWS_LAT = N_LAT * T_LAT
ROWS = ROWS_CTX + ROWS_LAT
DEPTH = 4
GRID_W = 64
HD = 64
A_HEADS, A_KV_HEADS, B_HEADS, C_HEADS = 8, 2, 8, 16
A_W, A_KV_W, B_W, C_W = A_HEADS * HD, A_KV_HEADS * HD, B_HEADS * HD, C_HEADS * HD
A_IN = A_W + 2 * A_KV_W
LORA_W, LORA_A, LORA_G = 64, 64, 128
B_IN = 3 * B_W + 2 * LORA_W + 2 * LORA_A + LORA_G
EVEN_IN = A_IN + B_IN
PAST = 512
NA_ROWS, NA_COLS = 8, 16
D_FF = 2816
N_EXPERTS = 8
D_EXPERT = 1408
ROPE_THETA = 10000.0
EPS = 1e-6
GN_EPS = 64e-5
NEG_INF = -1e30
ATT_SCALE = HD ** -0.5

LANES = 128
SUBLANES = 8
VMEM_LIMIT = 56 * 1024 * 1024

N_MODS = 8

ROW_SUB = T_LAT // T_CTX
ROW_GROUPS = ROWS // (ROW_SUB * T_CTX)
CTX_GROUPS = N_CTX // ROW_SUB


def _cparams(sem):
    return pltpu.CompilerParams(dimension_semantics=sem, vmem_limit_bytes=VMEM_LIMIT)


def _untouched():
    return pl.BlockSpec(memory_space=pl.ANY)


def _mod_index(row_start):
    return jnp.where(row_start < ROWS_CTX, 0, 1 + (row_start - ROWS_CTX) // T_LAT)


def _modspec(layer, tm, nargs=1):
    if nargs == 1:
        return pl.BlockSpec((None, 1, 6, D), lambda i: (layer, _mod_index(i * tm), 0, 0))
    return pl.BlockSpec((None, 1, 6, D), lambda i, j: (layer, _mod_index(i * tm), 0, 0))


def _gainspec(layer, nargs=1):
    if nargs == 1:
        return pl.BlockSpec((None, 1, D), lambda i: (layer, 0, 0))
    return pl.BlockSpec((None, 1, D), lambda i, j: (layer, 0, 0))


def _modnorm(x, g, shift, scale):
    ms = jnp.mean(x * x, axis=-1, keepdims=True)
    return (x * lax.rsqrt(ms + EPS) * g) * (1.0 + scale) + shift


def _silu(x):
    return x * jax.nn.sigmoid(x)


def _ada_kernel(c_ref, w_ref, b_ref, o_ref):
    s = _silu(c_ref[...]).astype(BF16)
    o_ref[0] = jnp.dot(s, w_ref[0].astype(BF16), preferred_element_type=F32) + b_ref[0]


def ada_all(cond, ada_w, ada_b):
    tn = 1536
    n = 6 * D
    return pl.pallas_call(
        _ada_kernel,
        grid=(DEPTH, n // tn),
        in_specs=[pl.BlockSpec((N_MODS, D), lambda l, j: (0, 0)),
                  pl.BlockSpec((1, D, tn), lambda l, j: (l, 0, j)),
                  pl.BlockSpec((1, 1, tn), lambda l, j: (l, 0, j))],
        out_specs=pl.BlockSpec((1, N_MODS, tn), lambda l, j: (l, 0, j)),
        out_shape=jax.ShapeDtypeStruct((DEPTH, N_MODS, n), F32),
        compiler_params=_cparams(("arbitrary", "arbitrary")),
        name="ada",
    )(cond, ada_w, ada_b.reshape(DEPTH, 1, n))


def _first_norm_kernel(xp_ref, xs_ref, g_ref, mod_ref, x_ref, h_ref, *, n_ctx_blocks):
    i = pl.program_id(0)

    def emit(src):
        x = src[...]
        x_ref[...] = x
        h_ref[...] = _modnorm(x, g_ref[...], mod_ref[0, 0:1, :], mod_ref[0, 1:2, :]).astype(BF16)

    @pl.when(i < n_ctx_blocks)
    def _():
        emit(xp_ref)

    @pl.when(i >= n_ctx_blocks)
    def _():
        emit(xs_ref)


def first_norm(x_prompt, x_sample, gains, mods):
    tm = 512
    nc = ROWS_CTX // tm
    rowspec = pl.BlockSpec((tm, D), lambda i: (i, 0))
    return pl.pallas_call(
        functools.partial(_first_norm_kernel, n_ctx_blocks=nc),
        grid=(ROWS // tm,),
        in_specs=[pl.BlockSpec((tm, D), lambda i: (jnp.minimum(i, nc - 1), 0)),
                  pl.BlockSpec((tm, D), lambda i: (jnp.maximum(i - nc, 0), 0)),
                  _gainspec(0), _modspec(0, tm)],
        out_specs=[rowspec, rowspec],
        out_shape=[jax.ShapeDtypeStruct((ROWS, D), F32), jax.ShapeDtypeStruct((ROWS, D), BF16)],
        compiler_params=_cparams(("arbitrary",)),
        name="first_norm",
    )(x_prompt.reshape(ROWS_CTX, D), x_sample.reshape(ROWS_LAT, D), gains, mods)


def _mm_in_kernel(h_ref, w_ref, o_ref, wb_ref):
    @pl.when(pl.program_id(1) == 0)
    def _():
        wb_ref[...] = w_ref[...].astype(BF16)

    o_ref[...] = jnp.dot(h_ref[...], wb_ref[...], preferred_element_type=F32)


def mm_in(h, w, layer, tn, tm=1024):
    _, k, n = w.shape
    return pl.pallas_call(
        _mm_in_kernel,
        grid=(n // tn, ROWS // tm),
        in_specs=[pl.BlockSpec((tm, k), lambda j, i: (i, 0)),
                  pl.BlockSpec((None, k, tn), lambda j, i: (layer, 0, j))],
        out_specs=pl.BlockSpec((tm, tn), lambda j, i: (i, j)),
        out_shape=jax.ShapeDtypeStruct((ROWS, n), F32),
        scratch_shapes=[pltpu.VMEM((k, tn), BF16)],
        compiler_params=_cparams(("arbitrary", "arbitrary")),
        name="mm_in",
    )(h, w)


def _residual_epilogue(x, acc, mod_ref, nmod_ref, g_ref, xo_ref, ho_ref, gate_idx, shift_idx, scale_idx, final):
    gate = mod_ref[0, gate_idx:gate_idx + 1, :]
    xn = x + gate * acc
    if final:
        ms = jnp.mean(xn * xn, axis=-1, keepdims=True)
        ho_ref[...] = xn * lax.rsqrt(ms + EPS) * g_ref[...]
    else:
        xo_ref[...] = xn
        shift = nmod_ref[0, shift_idx:shift_idx + 1, :]
        scale = nmod_ref[0, scale_idx:scale_idx + 1, :]
        ho_ref[...] = _modnorm(xn, g_ref[...], shift, scale).astype(BF16)


def _mm_out_kernel(*refs, n_parts, gate_idx, shift_idx, scale_idx):
    y_refs = refs[:n_parts]
    w_refs = refs[n_parts:2 * n_parts]
    x_ref, mod_ref, nmod_ref, g_ref, xo_ref, ho_ref = refs[2 * n_parts:2 * n_parts + 6]
    wb_refs = refs[2 * n_parts + 6:]

    @pl.when(pl.program_id(0) == 0)
    def _():
        for w_ref, wb_ref in zip(w_refs, wb_refs):
            wb_ref[...] = w_ref[...].astype(BF16)

    acc = None
    for y_ref, wb_ref in zip(y_refs, wb_refs):
        p = jnp.dot(y_ref[...], wb_ref[...], preferred_element_type=F32)
        acc = p if acc is None else acc + p
    _residual_epilogue(x_ref[...], acc, mod_ref, nmod_ref, g_ref, xo_ref, ho_ref,
                       gate_idx, shift_idx, scale_idx, False)


def mm_out(parts, w, layer_w, x, mods, layer, gains, gain_layer):
    tm = 512
    kp = parts[0].shape[1]
    n_parts = len(parts)
    rowspec = pl.BlockSpec((tm, D), lambda i: (i, 0))
    return pl.pallas_call(
        functools.partial(_mm_out_kernel, n_parts=n_parts, gate_idx=2, shift_idx=3, scale_idx=4),
        grid=(ROWS // tm,),
        in_specs=[pl.BlockSpec((tm, kp), lambda i: (i, 0)) for _ in parts]
                 + [pl.BlockSpec((None, kp, D), lambda i, p=p: (layer_w, p, 0)) for p in range(n_parts)]
                 + [rowspec, _modspec(layer, tm), _modspec(layer, tm), _gainspec(gain_layer)],
        out_specs=[rowspec, rowspec],
        out_shape=[jax.ShapeDtypeStruct((ROWS, D), F32), jax.ShapeDtypeStruct((ROWS, D), BF16)],
        scratch_shapes=[pltpu.VMEM((kp, D), BF16) for _ in parts],
        compiler_params=_cparams(("arbitrary",)),
        name="mm_out",
    )(*parts, *([w] * n_parts), x, mods, mods, gains)


FFN_TF = 256
FFN_NF = D_FF // FFN_TF


def _ffn_kernel(h_ref, wg_ref, wu_ref, wd_ref, x_ref, mod_ref, nmod_ref, g_ref, xo_ref, ho_ref,
                wg_s, wu_s, wd_s, acc_ref):
    i = pl.program_id(0)
    f = pl.program_id(1)

    @pl.when(i == 0)
    def _():
        wg_s[f] = wg_ref[...].astype(BF16)
        wu_s[f] = wu_ref[...].astype(BF16)
        wd_s[f] = wd_ref[...].astype(BF16)

    @pl.when(f == 0)
    def _():
        acc_ref[...] = jnp.zeros_like(acc_ref)

    h = h_ref[...]
    gp = jnp.dot(h, wg_s[f], preferred_element_type=F32)
    up = jnp.dot(h, wu_s[f], preferred_element_type=F32)
    a = (_silu(gp) * up).astype(BF16)
    acc_ref[...] += jnp.dot(a, wd_s[f], preferred_element_type=F32)

    @pl.when(f == FFN_NF - 1)
    def _():
        _residual_epilogue(x_ref[...], acc_ref[...], mod_ref, nmod_ref, g_ref, xo_ref, ho_ref, 5, 0, 1, False)


def ffn(h, w_gu, w_dn, e, x, mods, layer, gains):
    tm = 512
    tf, nf = FFN_TF, FFN_NF

    def once(i, f):
        return jnp.where(i == 0, f, nf - 1)

    rowspec = pl.BlockSpec((tm, D), lambda i, f: (i, 0))
    return pl.pallas_call(
        _ffn_kernel,
        grid=(ROWS // tm, nf),
        in_specs=[rowspec,
                  pl.BlockSpec((None, D, tf), lambda i, f: (e, 0, once(i, f))),
                  pl.BlockSpec((None, D, tf), lambda i, f: (e, 0, nf + once(i, f))),
                  pl.BlockSpec((None, tf, D), lambda i, f: (e, once(i, f), 0)),
                  rowspec, _modspec(layer, tm, 2), _modspec(layer + 1, tm, 2), _gainspec(layer + 1, 2)],
        out_specs=[rowspec, rowspec],
        out_shape=[jax.ShapeDtypeStruct((ROWS, D), F32), jax.ShapeDtypeStruct((ROWS, D), BF16)],
        scratch_shapes=[pltpu.VMEM((nf, D, tf), BF16),
                        pltpu.VMEM((nf, D, tf), BF16),
                        pltpu.VMEM((nf, tf, D), BF16),
                        pltpu.VMEM((tm, D), F32)],
        compiler_params=_cparams(("arbitrary", "arbitrary")),
        name="ffn",
    )(h, w_gu, w_gu, w_dn, x, mods, mods, gains)


MOE_TILE = 1024
MOE_CHUNK = 320


def _router_kernel(x_ref, g_ref, mod_ref, rw_ref, rb_ref, gates_ref, pos_ref, gates_t_ref, pos_t_ref, carry_ref):
    h = _modnorm(x_ref[...], g_ref[...], mod_ref[0, 3:4, :], mod_ref[0, 4:5, :])
    logits = jnp.dot(h, rw_ref[...], precision=HIGHEST, preferred_element_type=F32)
    lane = lax.broadcasted_iota(jnp.int32, logits.shape, 1)
    sel = jnp.where(lane < N_EXPERTS, logits + rb_ref[...], -jnp.inf)
    m1 = jnp.max(sel, axis=-1, keepdims=True)
    i1 = jnp.min(jnp.where(sel == m1, lane, LANES), axis=-1, keepdims=True)
    sel2 = jnp.where(lane == i1, -jnp.inf, sel)
    m2 = jnp.max(sel2, axis=-1, keepdims=True)
    i2 = jnp.min(jnp.where(sel2 == m2, lane, LANES), axis=-1, keepdims=True)
    l1 = jnp.sum(jnp.where(lane == i1, logits, 0.0), axis=-1, keepdims=True)
    l2 = jnp.sum(jnp.where(lane == i2, logits, 0.0), axis=-1, keepdims=True)
    mx = jnp.maximum(l1, l2)
    e1 = jnp.exp(l1 - mx)
    e2 = jnp.exp(l2 - mx)
    den = e1 + e2
    gates = jnp.where(lane == i1, e1 / den, 0.0) + jnp.where(lane == i2, e2 / den, 0.0)
    gates_ref[...] = gates

    tm = gates.shape[0]
    routed = gates > 0.0
    tri = (lax.broadcasted_iota(jnp.int32, (tm, tm), 0) > lax.broadcasted_iota(jnp.int32, (tm, tm), 1))
    local = jnp.dot(jnp.where(tri, 1.0, 0.0).astype(BF16), jnp.where(routed, 1.0, 0.0).astype(BF16),
                    preferred_element_type=F32)

    @pl.when(pl.program_id(0) % (MOE_TILE // tm) == 0)
    def _():
        carry_ref[...] = jnp.zeros_like(carry_ref)

    pos = local + carry_ref[...]
    carry_ref[...] += jnp.sum(jnp.where(routed, 1.0, 0.0), axis=0, keepdims=True)
    pos_ref[...] = pos
    gates_t_ref[...] = gates.T[:SUBLANES]
    pos_t_ref[...] = pos.T[:SUBLANES]


def router(x, gains, mods, layer, rw_p, rb_p, o):
    tm = 512
    rowspec = pl.BlockSpec((tm, LANES), lambda i: (i, 0))
    colspec = pl.BlockSpec((SUBLANES, tm), lambda i: (0, i))
    return pl.pallas_call(
        _router_kernel,
        grid=(ROWS // tm,),
        in_specs=[pl.BlockSpec((tm, D), lambda i: (i, 0)),
                  _gainspec(layer), _modspec(layer, tm),
                  pl.BlockSpec((None, D, LANES), lambda i: (o, 0, 0)),
                  pl.BlockSpec((None, 1, LANES), lambda i: (o, 0, 0))],
        out_specs=[rowspec, rowspec, colspec, colspec],
        out_shape=[jax.ShapeDtypeStruct((ROWS, LANES), F32), jax.ShapeDtypeStruct((ROWS, LANES), F32),
                   jax.ShapeDtypeStruct((SUBLANES, ROWS), F32), jax.ShapeDtypeStruct((SUBLANES, ROWS), F32)],
        scratch_shapes=[pltpu.VMEM((1, LANES), F32)],
        compiler_params=_cparams(("arbitrary",)),
        name="router",
    )(x, gains, mods, rw_p, rb_p)


def _moe_kernel(cnt_ref, h_ref, gates_ref, pos_ref, gates_t_ref, pos_t_ref, wgu_ref, wdn_ref, o_ref, acc_ref):
    tile = pl.program_id(0)
    e = pl.program_id(1)
    tt = h_ref.shape[0]

    @pl.when(e == 0)
    def _():
        acc_ref[...] = jnp.zeros_like(acc_ref)

    lane = lax.broadcasted_iota(jnp.int32, (tt, LANES), 1)
    g_col = jnp.sum(jnp.where(lane == e, gates_ref[...], 0.0), axis=1, keepdims=True)
    p_col = jnp.sum(jnp.where(lane == e, pos_ref[...], 0.0), axis=1, keepdims=True)
    sub = lax.broadcasted_iota(jnp.int32, (SUBLANES, tt), 0)
    g_row = jnp.sum(jnp.where(sub == e, gates_t_ref[...], 0.0), axis=0, keepdims=True)
    p_row = jnp.sum(jnp.where(sub == e, pos_t_ref[...], 0.0), axis=0, keepdims=True)
    p_col = jnp.where(g_col > 0.0, p_col, -1.0)
    p_row = jnp.where(g_row > 0.0, p_row, -1.0)

    def chunk(ci, carry):
        base = (ci * MOE_CHUNK).astype(F32)
        take = p_row == lax.broadcasted_iota(jnp.int32, (MOE_CHUNK, tt), 0).astype(F32) + base
        xs = jnp.dot(jnp.where(take, 1.0, 0.0).astype(BF16), h_ref[...], preferred_element_type=F32)
        gu = jnp.dot(xs.astype(BF16), wgu_ref[...], preferred_element_type=F32)
        a = (_silu(gu[:, :D_EXPERT]) * gu[:, D_EXPERT:]).astype(BF16)
        y = jnp.dot(a, wdn_ref[...], preferred_element_type=F32)
        gate = jnp.sum(jnp.where(take, g_row, 0.0), axis=1, keepdims=True)
        put = p_col == lax.broadcasted_iota(jnp.int32, (tt, MOE_CHUNK), 1).astype(F32) + base
        acc_ref[...] += jnp.dot(jnp.where(put, 1.0, 0.0).astype(BF16), (gate * y).astype(BF16),
                                preferred_element_type=F32)
        return carry

    count = cnt_ref[tile * N_EXPERTS + e]
    lax.fori_loop(0, (count + MOE_CHUNK - 1) // MOE_CHUNK, chunk, 0)

    @pl.when(e == N_EXPERTS - 1)
    def _():
        o_ref[...] = acc_ref[...]


def moe_routed(h, routing, w_gu, w_dn, o):
    gates, pos, gates_t, pos_t = routing
    tt = MOE_TILE
    n_tiles = ROWS // tt
    counts = jnp.sum((gates[:, :N_EXPERTS] > 0.0).reshape(n_tiles, tt, N_EXPERTS), axis=1)
    counts = counts.astype(jnp.int32).reshape(n_tiles * N_EXPERTS)
    rowspec = pl.BlockSpec((tt, LANES), lambda t, e, c: (t, 0))
    colspec = pl.BlockSpec((SUBLANES, tt), lambda t, e, c: (0, t))
    grid_spec = pltpu.PrefetchScalarGridSpec(
        num_scalar_prefetch=1,
        grid=(n_tiles, N_EXPERTS),
        in_specs=[pl.BlockSpec((tt, D), lambda t, e, c: (t, 0)),
                  rowspec, rowspec, colspec, colspec,
                  pl.BlockSpec((None, None, D, 2 * D_EXPERT), lambda t, e, c: (o, e, 0, 0)),
                  pl.BlockSpec((None, None, D_EXPERT, D), lambda t, e, c: (o, e, 0, 0))],
        out_specs=pl.BlockSpec((tt, D), lambda t, e, c: (t, 0)),
        scratch_shapes=[pltpu.VMEM((tt, D), F32)])
    return pl.pallas_call(
        _moe_kernel,
        grid_spec=grid_spec,
        out_shape=jax.ShapeDtypeStruct((ROWS, D), F32),
        compiler_params=_cparams(("arbitrary", "arbitrary")),
        name="moe",
    )(counts, h, gates, pos, gates_t, pos_t, w_gu, w_dn)


def _resid_kernel(x_ref, y_ref, mod_ref, nmod_ref, g_ref, *out_refs, final):
    if not final:
        xo_ref, ho_ref = out_refs
        _residual_epilogue(x_ref[...], y_ref[...], mod_ref, nmod_ref, g_ref, xo_ref, ho_ref, 5, 0, 1, False)
        return
    n_ctx_blocks = ROWS_CTX // x_ref.shape[0]
    for ho_ref, mine in zip(out_refs, (pl.program_id(0) < n_ctx_blocks, pl.program_id(0) >= n_ctx_blocks)):
        @pl.when(mine)
        def _(ho_ref=ho_ref):
            _residual_epilogue(x_ref[...], y_ref[...], mod_ref, nmod_ref, g_ref, None, ho_ref, 5, 0, 1, True)


def resid(x, y, mods, layer, gains, final):
    tm = 512
    rowspec = pl.BlockSpec((tm, D), lambda i: (i, 0))
    if final:
        nb = ROWS_CTX // tm
        out_specs = [pl.BlockSpec((tm, D), lambda i: (jnp.minimum(i, nb - 1), 0)),
                     pl.BlockSpec((tm, D), lambda i: (jnp.maximum(i - nb, 0), 0))]
        out_shape = [jax.ShapeDtypeStruct((ROWS_CTX, D), F32), jax.ShapeDtypeStruct((ROWS_LAT, D), F32)]
        nmod, gain = _modspec(layer, tm), pl.BlockSpec((1, D), lambda i: (0, 0))
    else:
        out_specs = [rowspec, rowspec]
        out_shape = [jax.ShapeDtypeStruct((ROWS, D), F32), jax.ShapeDtypeStruct((ROWS, D), BF16)]
        nmod, gain = _modspec(layer + 1, tm), _gainspec(layer + 1)
    return pl.pallas_call(
        functools.partial(_resid_kernel, final=final),
        grid=(ROWS // tm,),
        in_specs=[rowspec, rowspec, _modspec(layer, tm), nmod, gain],
        out_specs=out_specs,
        out_shape=out_shape,
        compiler_params=_cparams(("arbitrary",)),
        name="resid",
    )(x, y, mods, mods, gains)


def _softmax_pv(scores, values):
    m = None
    for s in scores:
        mi = jnp.max(s, axis=-1, keepdims=True)
        m = mi if m is None else jnp.maximum(m, mi)
    num, den = None, None
    for s, v in zip(scores, values):
        p = jnp.exp(s - m)
        li = jnp.sum(p, axis=-1, keepdims=True)
        oi = jnp.dot(p.astype(BF16), v, preferred_element_type=F32)
        num = oi if num is None else num + oi
        den = li if den is None else den + li
    return num / den


def _qk(q, k):
    return lax.dot_general(q, k, (((1,), (1,)), ((), ())), preferred_element_type=F32)


def _head_rmsnorm(x, gain, n_heads):
    parts = []
    for h in range(n_heads):
        xh = x[:, h * HD:(h + 1) * HD]
        ms = jnp.mean(xh * xh, axis=-1, keepdims=True)
        parts.append(xh * lax.rsqrt(ms + EPS) * gain)
    return parts


def _rope128(x, cos, sin_signed):
    lane = lax.broadcasted_iota(jnp.int32, x.shape, 1)
    up = pltpu.roll(x, LANES - 16, 1)
    dn = pltpu.roll(x, 16, 1)
    partner = jnp.where((lane % 32) < 16, up, dn)
    return x * cos + partner * sin_signed


def _rope_tables():
    t = np.arange(T_LAT)
    n_f = HD // 4
    inv = ROPE_THETA ** (-np.arange(n_f, dtype=np.float32) / n_f)
    cos = np.zeros((T_LAT, HD), np.float32)
    sin = np.zeros((T_LAT, HD), np.float32)
    for half, pos in ((0, t // GRID_W), (1, t % GRID_W)):
        ang = pos[:, None].astype(np.float32) * inv[None, :]
        c, s = np.cos(ang), np.sin(ang)
        base = half * (HD // 2)
        cos[:, base:base + n_f] = c
        cos[:, base + n_f:base + 2 * n_f] = c
        sin[:, base:base + n_f] = -s
        sin[:, base + n_f:base + 2 * n_f] = s
    return np.tile(cos, (1, 2)), np.tile(sin, (1, 2))


def _attn_a_ctx_kernel(q_ref, kv_ref, qg_ref, kg_ref, *rest):
    o_ref, ko_ref, vo_ref = rest[-3:]
    q = q_ref[...]
    kv = kv_ref[...]
    k = kv[:, :A_KV_W]
    v = kv[:, A_KV_W:]
    kn = _head_rmsnorm(k, kg_ref[...], A_KV_HEADS)
    ko_ref[...] = jnp.concatenate(kn, axis=-1)
    vo_ref[...] = v
    qn = _head_rmsnorm(q, qg_ref[...], A_HEADS)
    group = A_HEADS // A_KV_HEADS
    outs = []
    for h in range(A_HEADS):
        g = h // group
        kh = kn[g].astype(BF16)
        vh = v[:, g * HD:(g + 1) * HD].astype(BF16)
        qh = (qn[h] * ATT_SCALE).astype(BF16)
        outs.append(_softmax_pv([_qk(qh, kh)], [vh]))
    o_ref[...] = jnp.concatenate(outs, axis=-1).astype(BF16)


def attn_a_ctx(z, gq, gk, e, prev_caches):
    t = T_CTX
    n_even = gq.shape[0]
    cache_spec = pl.BlockSpec((None, None, t, A_KV_W), lambda b: (b, e, 0, 0))
    cache_shape = jax.ShapeDtypeStruct((N_CTX, n_even, t, A_KV_W), F32)
    extra = [] if prev_caches is None else list(prev_caches)
    aliases = {} if prev_caches is None else {4: 1, 5: 2}
    return pl.pallas_call(
        _attn_a_ctx_kernel,
        grid=(N_CTX,),
        in_specs=[pl.BlockSpec((t, A_W), lambda b: (b, 0)),
                  pl.BlockSpec((t, 2 * A_KV_W), lambda b: (b, A_W // (2 * A_KV_W))),
                  pl.BlockSpec((None, 1, HD), lambda b: (e, 0, 0)),
                  pl.BlockSpec((None, 1, HD), lambda b: (e, 0, 0))] + [_untouched() for _ in extra],
        out_specs=[pl.BlockSpec((t, A_W), lambda b: (b, 0)), cache_spec, cache_spec],
        out_shape=[jax.ShapeDtypeStruct((ROWS, A_W), BF16), cache_shape, cache_shape],
        input_output_aliases=aliases,
        compiler_params=_cparams(("arbitrary",)),
        name="attn_a_ctx",
    )(z, z, gq, gk, *extra)


A_LAT_TQ = 256


def _attn_a_lat_kernel(q_ref, kv_ref, ck_ref, cv_ref, qg_ref, kg_ref, cq_ref, sq_ref, ckk_ref, skk_ref,
                       prev_ref, o_ref, k_s, v_s):
    del prev_ref

    @pl.when(pl.program_id(1) == 0)
    def _():
        kv = kv_ref[...]
        kn = jnp.concatenate(_head_rmsnorm(kv[:, :A_KV_W], kg_ref[...], A_KV_HEADS), axis=-1)
        k_s[...] = _rope128(kn, ckk_ref[...], skk_ref[...]).astype(BF16)
        v_s[...] = kv[:, A_KV_W:].astype(BF16)

    qn = _head_rmsnorm(q_ref[...], qg_ref[...], A_HEADS)
    cos, sin = cq_ref[...], sq_ref[...]
    qr = []
    for c in range(A_HEADS // 2):
        slab = _rope128(jnp.concatenate(qn[2 * c:2 * c + 2], axis=-1), cos, sin)
        qr.append(slab[:, :HD])
        qr.append(slab[:, HD:])
    ck = ck_ref[...].astype(BF16)
    cv = cv_ref[...].astype(BF16)
    kk = k_s[...]
    vv = v_s[...]
    group = A_HEADS // A_KV_HEADS
    outs = []
    for h in range(A_HEADS):
        g = h // group
        sl = slice(g * HD, (g + 1) * HD)
        qh = (qr[h] * ATT_SCALE).astype(BF16)
        outs.append(_softmax_pv([_qk(qh, kk[:, sl]), _qk(qh, ck[:, sl])], [vv[:, sl], cv[:, sl]]))
    o_ref[...] = jnp.concatenate(outs, axis=-1).astype(BF16)


def attn_a_lat(z, ck, cv, gq, gk, e, y_prev):
    tq = A_LAT_TQ
    nq = T_LAT // tq
    cos, sin = _rope_tables()
    cos, sin = jnp.asarray(cos), jnp.asarray(sin)
    row0 = ROWS_CTX // tq
    seq0 = ROWS_CTX // T_LAT
    cache_spec = pl.BlockSpec((None, None, PAST, A_KV_W), lambda b, i: (b, e, 0, 0))
    return pl.pallas_call(
        _attn_a_lat_kernel,
        grid=(N_LAT, nq),
        in_specs=[pl.BlockSpec((tq, A_W), lambda b, i: (row0 + b * nq + i, 0)),
                  pl.BlockSpec((T_LAT, 2 * A_KV_W), lambda b, i: (seq0 + b, A_W // (2 * A_KV_W))),
                  cache_spec, cache_spec,
                  pl.BlockSpec((None, 1, HD), lambda b, i: (e, 0, 0)),
                  pl.BlockSpec((None, 1, HD), lambda b, i: (e, 0, 0)),
                  pl.BlockSpec((tq, LANES), lambda b, i: (i, 0)),
                  pl.BlockSpec((tq, LANES), lambda b, i: (i, 0)),
                  pl.BlockSpec((T_LAT, LANES), lambda b, i: (0, 0)),
                  pl.BlockSpec((T_LAT, LANES), lambda b, i: (0, 0)),
                  _untouched()],
        out_specs=pl.BlockSpec((tq, A_W), lambda b, i: (row0 + b * nq + i, 0)),
        out_shape=jax.ShapeDtypeStruct((ROWS, A_W), BF16),
        scratch_shapes=[pltpu.VMEM((T_LAT, A_KV_W), BF16),
                        pltpu.VMEM((T_LAT, A_KV_W), BF16)],
        input_output_aliases={10: 0},
        compiler_params=_cparams(("arbitrary", "arbitrary")),
        name="attn_a_lat",
    )(z, z, ck, cv, gq, gk, cos, sin, cos, sin, y_prev)


def _attn_c_ctx_kernel(q_ref, k_ref, v_ref, *rest):
    o_ref, ko_ref, vo_ref = rest[-3:]
    q = q_ref[...]
    k = k_ref[...]
    v = v_ref[...]
    ko_ref[...] = k
    vo_ref[...] = v
    outs = []
    for h in range(q.shape[1] // HD):
        sl = slice(h * HD, (h + 1) * HD)
        qh = (q[:, sl] * ATT_SCALE).astype(BF16)
        outs.append(_softmax_pv([_qk(qh, k[:, sl].astype(BF16))], [v[:, sl].astype(BF16)]))
    o_ref[...] = jnp.concatenate(outs, axis=-1).astype(BF16)


C_CTX_HEADS_PER_STEP = 8


def attn_c_ctx(z, o, n_odd, prev_caches):
    t = T_CTX
    wblk = C_CTX_HEADS_PER_STEP * HD
    nhp = C_W // wblk
    cache_spec = pl.BlockSpec((None, None, t, wblk), lambda b, p: (b, o, 0, p))
    cache_shape = jax.ShapeDtypeStruct((N_CTX, n_odd, t, C_W), F32)
    extra = [] if prev_caches is None else list(prev_caches)
    aliases = {} if prev_caches is None else {3: 1, 4: 2}
    return pl.pallas_call(
        _attn_c_ctx_kernel,
        grid=(N_CTX, nhp),
        in_specs=[pl.BlockSpec((t, wblk), lambda b, p: (b, p)),
                  pl.BlockSpec((t, wblk), lambda b, p: (b, nhp + p)),
                  pl.BlockSpec((t, wblk), lambda b, p: (b, 2 * nhp + p))] + [_untouched() for _ in extra],
        out_specs=[pl.BlockSpec((t, wblk), lambda b, p: (b, p)), cache_spec, cache_spec],
        out_shape=[jax.ShapeDtypeStruct((ROWS, C_W), BF16), cache_shape, cache_shape],
        input_output_aliases=aliases,
        compiler_params=_cparams(("arbitrary", "arbitrary")),
        name="attn_c_ctx",
    )(z, z, z, *extra)


NA_GRID_ROWS = T_LAT // GRID_W
NA_WIN = NA_ROWS * GRID_W
NA_DR = 2 * NA_ROWS - 1
NA_DC = 2 * NA_COLS - 1


def _na_row_start(r):
    return min(max(r - NA_ROWS // 2, 0), NA_GRID_ROWS - NA_ROWS)


def _na_groups():
    groups, r = [], 0
    while r < NA_GRID_ROWS:
        r1 = r
        while r1 + 1 < NA_GRID_ROWS and (_na_row_start(r1 + 1) == _na_row_start(r) or r1 + 1 - r < 4):
            r1 += 1
        lo = _na_row_start(r)
        hi = _na_row_start(r1) + NA_ROWS
        pairs = -(-(hi - lo) // 2)
        lo = min(lo, NA_GRID_ROWS - 2 * pairs)
        groups.append((r, r1, lo, pairs))
        r = r1 + 1
    return groups


def _na_pair_codes():
    codes = []
    plan = []
    for r0, r1, lo, pairs in _na_groups():
        rows = []
        for r in range(r0, r1 + 1):
            rs = _na_row_start(r)
            row = []
            for p in range(pairs):
                code = tuple((kr - r + NA_ROWS - 1) if rs <= kr < rs + NA_ROWS else None
                             for kr in (lo + 2 * p, lo + 2 * p + 1))
                if code not in codes:
                    codes.append(code)
                row.append(codes.index(code))
            rows.append(row)
        plan.append(rows)
    return codes, plan


def _na_bias_table(rpb):
    n_l = rpb.shape[0]
    col = np.arange(GRID_W)
    cs = np.clip(col - NA_COLS // 2, 0, GRID_W - NA_COLS)
    col_in = (col[None, :] >= cs[:, None]) & (col[None, :] < cs[:, None] + NA_COLS)
    period = GRID_W + 1
    seq = jnp.concatenate([rpb, jnp.zeros((n_l, C_HEADS, NA_DR, period - NA_DC), F32)], axis=-1)
    seq = jnp.roll(seq, -(NA_COLS - 1), axis=-1)
    tile = jnp.tile(seq, (1, 1, 1, GRID_W))[..., :GRID_W * GRID_W].reshape(n_l, C_HEADS, NA_DR, GRID_W, GRID_W)
    tile = jnp.where(jnp.asarray(col_in), tile, NEG_INF)
    masked = jnp.full((n_l, C_HEADS, GRID_W, GRID_W), NEG_INF, F32)
    pick = lambda dr: masked if dr is None else tile[:, :, dr]
    codes, _ = _na_pair_codes()
    return jnp.stack([jnp.concatenate([pick(a), pick(b)], axis=-1) for a, b in codes], axis=2)


def _attn_na_kernel(q_ref, k_ref, v_ref, ck_ref, cv_ref, bias_ref, prev_ref, o_ref):
    del prev_ref
    _, plan = _na_pair_codes()
    heads = []
    for h in range(LANES // HD):
        sl = slice(h * HD, (h + 1) * HD)
        q = (q_ref[:, sl] * ATT_SCALE).astype(BF16)
        k = k_ref[:, sl].astype(BF16)
        v = v_ref[:, sl].astype(BF16)
        ck = ck_ref[:, sl].astype(BF16)
        cv = cv_ref[:, sl].astype(BF16)
        rows = []
        for (r0, r1, lo, pairs), codes in zip(_na_groups(), plan):
            qg = q[r0 * GRID_W:(r1 + 1) * GRID_W]
            kw = k[lo * GRID_W:(lo + 2 * pairs) * GRID_W]
            vw = v[lo * GRID_W:(lo + 2 * pairs) * GRID_W]
            bias = jnp.concatenate(
                [jnp.concatenate([bias_ref[h, c] for c in row], axis=-1) for row in codes], axis=0)
            s_nb = _qk(qg, kw) + bias
            s_cx = _qk(qg, ck)
            rows.append(_softmax_pv([s_nb, s_cx], [vw, cv]))
        heads.append(jnp.concatenate(rows, axis=0))
    o_ref[...] = jnp.concatenate(heads, axis=-1).astype(BF16)


def attn_na(z, ck, cv, bias, o, y_prev):
    nhp = C_W // LANES
    hpb = LANES // HD
    seq0 = ROWS_CTX // T_LAT
    cache_spec = pl.BlockSpec((None, None, PAST, LANES), lambda p, b: (b, o, 0, p))
    return pl.pallas_call(
        _attn_na_kernel,
        grid=(nhp, N_LAT),
        in_specs=[pl.BlockSpec((T_LAT, LANES), lambda p, b: (seq0 + b, p)),
                  pl.BlockSpec((T_LAT, LANES), lambda p, b: (seq0 + b, nhp + p)),
                  pl.BlockSpec((T_LAT, LANES), lambda p, b: (seq0 + b, 2 * nhp + p)),
                  cache_spec, cache_spec,
                  pl.BlockSpec((None, hpb, bias.shape[2], GRID_W, LANES), lambda p, b: (o, p, 0, 0, 0)),
                  _untouched()],
        out_specs=pl.BlockSpec((T_LAT, LANES), lambda p, b: (seq0 + b, p)),
        out_shape=jax.ShapeDtypeStruct((ROWS, C_W), BF16),
        input_output_aliases={6: 0},
        compiler_params=_cparams(("arbitrary", "arbitrary")),
        name="attn_na",
    )(z, z, z, ck, cv, bias, y_prev)


def _seg_sum(x, n_heads):
    parts = []
    for h in range(n_heads):
        s = jnp.sum(x[:, h * HD:(h + 1) * HD], axis=-1, keepdims=True)
        parts.append(jnp.broadcast_to(s, (x.shape[0], HD)))
    return jnp.concatenate(parts, axis=-1)


PREP_TM = 256


def _seq_len_at(row_start):
    return jnp.where(row_start < ROWS_CTX, T_CTX, T_LAT)


def _rwkv_prep_kernel(z_ref, zprev_ref, znext_ref, mu_ref, kkw_ref, w0_ref, w2_ref, a0_ref, a2_ref, ka_ref,
                      rk_ref, g2_ref, o_rk_ref, o_wk_ref, o_bv_ref, g_ref, bonus_ref):
    z = z_ref[:, A_IN:]
    t = z.shape[0]
    start = pl.program_id(0) * t
    seq_len = _seq_len_at(start)
    pos = (start - jnp.where(start < ROWS_CTX, 0, ROWS_CTX)) % seq_len
    halo_prev = jnp.where(pos == 0, 0.0, zprev_ref[SUBLANES - 1:SUBLANES, A_IN:])
    halo_next = jnp.where(pos + t == seq_len, 0.0, znext_ref[0:1, A_IN:])
    row = lax.broadcasted_iota(jnp.int32, (t, 1), 0)
    prev = jnp.where(row == 0, halo_prev, pltpu.roll(z, 1, 0))
    nxt = jnp.where(row == t - 1, halo_next, pltpu.roll(z, t - 1, 0))
    m = z + mu_ref[...] * (0.5 * (prev + nxt) - z)
    r = m[:, :B_W]
    k = m[:, B_W:2 * B_W]
    v = m[:, 2 * B_W:3 * B_W]
    o = 3 * B_W
    wd = m[:, o:o + 2 * LORA_W]
    ad = m[:, o + 2 * LORA_W:o + 2 * LORA_W + 2 * LORA_A]
    gd = m[:, o + 2 * LORA_W + 2 * LORA_A:]

    kkr = k * kkw_ref[...]
    kk = kkr * lax.rsqrt(_seg_sum(kkr * kkr, B_HEADS) + 1e-12)

    wl = w0_ref[...] + jnp.dot(jnp.tanh(wd), w2_ref[...], precision=HIGHEST, preferred_element_type=F32)
    decay = jnp.exp(-float(np.exp(-0.5)) * jax.nn.sigmoid(wl))
    a = jax.nn.sigmoid(a0_ref[...] + jnp.dot(ad, a2_ref[...], precision=HIGHEST, preferred_element_type=F32))
    k2 = jnp.concatenate([k, k], axis=-1)
    ka2 = jnp.concatenate([ka_ref[...], ka_ref[...]], axis=-1)
    kk2 = jnp.concatenate([kk, kk], axis=-1)

    kd = k2 * (1.0 + (a - 1.0) * ka2)
    bb = kk2 * a
    for h in range(B_HEADS):
        sl = slice(h * HD, (h + 1) * HD)
        o_rk_ref[pl.ds(h, t, stride=B_HEADS), :] = jnp.concatenate([r[:, sl], kk[:, sl]], axis=-1)
        for dr in range(2):
            sd = slice(dr * B_W + h * HD, dr * B_W + (h + 1) * HD)
            row = pl.ds(dr * B_HEADS + h, t, stride=2 * B_HEADS)
            o_wk_ref[row, :] = jnp.concatenate([decay[:, sd], kd[:, sd]], axis=-1)
            o_bv_ref[row, :] = jnp.concatenate([bb[:, sd], v[:, sl]], axis=-1)
    g_ref[...] = jnp.dot(jax.nn.sigmoid(gd).astype(BF16), g2_ref[...].astype(BF16), preferred_element_type=F32)
    bonus_ref[...] = _seg_sum(r * k * rk_ref[...], B_HEADS) * v


def rwkv_prep(z, params):
    tm = PREP_TM
    full = lambda a: pl.BlockSpec(a.shape, lambda s: (0,) * a.ndim)
    out_rows = (B_HEADS, 2 * B_HEADS, 2 * B_HEADS)
    per = tm // SUBLANES
    last = ROWS // SUBLANES - 1
    rowspec = pl.BlockSpec((tm, B_W), lambda s: (s, 0))
    return pl.pallas_call(
        _rwkv_prep_kernel,
        grid=(ROWS // tm,),
        in_specs=[pl.BlockSpec((tm, EVEN_IN), lambda s: (s, 0)),
                  pl.BlockSpec((SUBLANES, EVEN_IN), lambda s: (jnp.maximum(s * per - 1, 0), 0)),
                  pl.BlockSpec((SUBLANES, EVEN_IN), lambda s: (jnp.minimum((s + 1) * per, last), 0))]
                 + [full(p) for p in params],
        out_specs=[pl.BlockSpec((tm * n, LANES), lambda s: (s, 0)) for n in out_rows] + [rowspec, rowspec],
        out_shape=[jax.ShapeDtypeStruct((ROWS * n, LANES), F32) for n in out_rows]
                  + [jax.ShapeDtypeStruct((ROWS, B_W), F32)] * 2,
        compiler_params=_cparams(("arbitrary",)),
        name="rwkv_prep",
    )(z, z, z, *params)


def rwkv_prep_all(z, prm, e):
    w2 = prm['b_w2'][e]
    a2 = prm['b_a2'][e]
    zero = jnp.zeros((LORA_W, B_W), F32)
    w2bd = jnp.concatenate([jnp.concatenate([w2[0], zero], 1), jnp.concatenate([zero, w2[1]], 1)], 0)
    a2bd = jnp.concatenate([jnp.concatenate([a2[0], zero], 1), jnp.concatenate([zero, a2[1]], 1)], 0)
    params = (prm['b_mu'][e][None], prm['b_kk'][e][None], prm['b_w0'][e].reshape(1, 2 * B_W), w2bd,
              prm['b_a0'][e].reshape(1, 2 * B_W), a2bd, prm['b_ka'][e][None],
              prm['b_rk'][e].reshape(1, B_W), prm['b_g2'][e])
    return rwkv_prep(z, params)


SCAN_CHAINS = 8


def _tree_sum(xs):
    while len(xs) > 1:
        xs = [xs[i] + xs[i + 1] for i in range(0, len(xs) - 1, 2)] + ([xs[-1]] if len(xs) % 2 else [])
    return xs[0]


def _scan_step(rk_b, wk_b, bv_b, vv, s_ref, ni):
    n_acc = max(1, SCAN_CHAINS // ni)

    def bcast(ref, row):
        return jnp.broadcast_to(ref[pl.ds(row, 1), :], (SUBLANES, LANES))

    sa = [[None] * n_acc for _ in range(ni)]
    for j in range(HD):
        kkj = bcast(rk_b, HD + j)
        for g in range(ni):
            p = s_ref[j, pl.ds(g * SUBLANES, SUBLANES), :] * kkj
            a = j % n_acc
            sa[g][a] = p if sa[g][a] is None else sa[g][a] + p
    sa = [-_tree_sum(x) for x in sa]
    ya = [[None] * n_acc for _ in range(ni)]
    for j in range(HD):
        wj = bcast(wk_b, j)
        kj = bcast(wk_b, HD + j)
        bj = bcast(bv_b, j)
        rj = bcast(rk_b, j)
        for g in range(ni):
            sl = pl.ds(g * SUBLANES, SUBLANES)
            s_new = s_ref[j, sl, :] * wj + sa[g] * bj + vv[g] * kj
            s_ref[j, sl, :] = s_new
            p = s_new * rj
            a = j % n_acc
            ya[g][a] = p if ya[g][a] is None else ya[g][a] + p
    return [_tree_sum(ya[g]) for g in range(ni)]


def _scan_pairs(tc, relayout, run):
    relayout(0, 0)

    def pair(k, carry):
        i0 = 2 * k
        relayout(i0 + 1, 1)
        run(i0, 0)
        relayout(jnp.minimum(i0 + 2, tc - 1), 0)
        run(i0 + 1, 1)
        return carry

    lax.fori_loop(0, tc // 2, pair, 0)


def _scan_ctx_kernel(rk_ref, wk_ref, bv_ref, y_ref, st_ref, rk0, wk0, bv0, rk1, wk1, bv1, s_ref, *, tc):
    d = pl.program_id(0)
    c = pl.program_id(1)
    ni = HD // SUBLANES
    bufs = ((rk0, wk0, bv0), (rk1, wk1, bv1))

    @pl.when(c == 0)
    def _():
        s_ref[...] = jnp.zeros_like(s_ref)

    def t_of(i):
        return i + d * (tc - 1 - 2 * i)

    def relayout(i, slot):
        t = t_of(i)
        for src, dst in zip((rk_ref, wk_ref, bv_ref), bufs[slot]):
            dst[...] = src[:, :, t].reshape(LANES, LANES).T

    def run(i, slot):
        rk_b, wk_b, bv_b = bufs[slot]
        vv = [bv_b[pl.ds(HD + g * SUBLANES, SUBLANES), :] for g in range(ni)]
        y = jnp.concatenate(_scan_step(rk_b, wk_b, bv_b, vv, s_ref, ni), axis=0)
        y_ref[:, :, t_of(i)] = y.T.reshape(CTX_GROUPS, ROW_SUB, B_HEADS, HD)

    _scan_pairs(tc, relayout, run)

    @pl.when(c == pl.num_programs(1) - 1)
    def _():
        st_ref[...] = s_ref[...]


def _rows5(x, heads):
    return x.reshape(ROW_GROUPS, ROW_SUB, T_CTX, heads, x.shape[-1])


def _operand_bufs():
    return [pltpu.VMEM((LANES, LANES), F32)] * 6


def rwkv_scan_ctx(prep, tc):
    rk, wk, bv = prep[:3]
    nc = T_CTX // tc

    def chunk(d, c):
        return c + d * (nc - 1 - 2 * c)

    blk = (CTX_GROUPS, ROW_SUB, tc, B_HEADS, LANES)
    shared = pl.BlockSpec(blk, lambda d, c: (0, 0, chunk(d, c), 0, 0))
    perdir = pl.BlockSpec(blk, lambda d, c: (0, 0, chunk(d, c), d, 0))
    yspec = pl.BlockSpec((CTX_GROUPS, ROW_SUB, tc, B_HEADS, HD), lambda d, c: (0, 0, chunk(d, c), d, 0))
    return pl.pallas_call(
        functools.partial(_scan_ctx_kernel, tc=tc),
        grid=(2, nc),
        in_specs=[shared, perdir, perdir],
        out_specs=[yspec, pl.BlockSpec((None, HD, HD, LANES), lambda d, c: (d, 0, 0, 0))],
        out_shape=[jax.ShapeDtypeStruct((ROW_GROUPS, ROW_SUB, T_CTX, 2 * B_HEADS, HD), F32),
                   jax.ShapeDtypeStruct((2, HD, HD, LANES), F32)],
        scratch_shapes=_operand_bufs() + [pltpu.VMEM((HD, HD, LANES), F32)],
        compiler_params=_cparams(("arbitrary", "arbitrary")),
        name="rwkv_scan_ctx",
    )(_rows5(rk, B_HEADS), _rows5(wk, 2 * B_HEADS), _rows5(bv, 2 * B_HEADS))


LAT_REP = LANES // (2 * N_LAT * B_HEADS)


def _scan_lat_kernel(rkf_ref, rkb_ref, wkf_ref, wkb_ref, bvf_ref, bvb_ref, s0_ref, yf_ref, yb_ref,
                     rk0, wk0, bv0, rk1, wk1, bv1, s_ref, y_s, *, tc):
    c = pl.program_id(0)
    n = N_LAT * B_HEADS
    nd = 2 * n
    ni = HD // (LAT_REP * SUBLANES)
    bufs = ((rk0, wk0, bv0), (rk1, wk1, bv1))

    @pl.when(c == 0)
    def _():
        s_ref[...] = s0_ref[...]

    def relayout(i, slot):
        srcs = ((rkf_ref, rkb_ref), (wkf_ref, wkb_ref), (bvf_ref, bvb_ref))
        for (fwd, bwd), dst in zip(srcs, bufs[slot]):
            m = jnp.concatenate([fwd[:, 0, i].reshape(n, LANES), bwd[:, 0, tc - 1 - i].reshape(n, LANES)], axis=0)
            dst[...] = jnp.concatenate([m] * LAT_REP, axis=0).T

    def run(i, slot):
        rk_b, wk_b, bv_b = bufs[slot]
        group = lax.broadcasted_iota(jnp.int32, (SUBLANES, LANES), 1) // nd
        vv = []
        for ig in range(ni):
            v = bv_b[pl.ds(HD + ig * SUBLANES, SUBLANES), :]
            for g in range(1, LAT_REP):
                v = jnp.where(group == g, bv_b[pl.ds(HD + (g * ni + ig) * SUBLANES, SUBLANES), :], v)
            vv.append(v)
        y_s[i] = jnp.concatenate(_scan_step(rk_b, wk_b, bv_b, vv, s_ref, ni), axis=0)

    _scan_pairs(tc, relayout, run)

    def store(i, carry):
        y = y_s[i]
        rows = [y] + [pltpu.roll(y, LANES - g * nd, 1) for g in range(1, LAT_REP)]
        yt = jnp.concatenate(rows, axis=0).T
        yf_ref[:, 0, i] = yt[:n].reshape(N_LAT, B_HEADS, HD)
        yb_ref[:, 0, tc - 1 - i] = yt[n:nd].reshape(N_LAT, B_HEADS, HD)
        return carry

    lax.fori_loop(0, tc, store, 0, unroll=8)


def rwkv_scan_lat(prep, s0f, s0b, tc):
    rk, wk, bv = prep[:3]
    nc = T_LAT // tc
    per_sub = T_CTX // tc
    g0 = CTX_GROUPS // N_LAT

    def fwd_map(head_block):
        return lambda c: (g0, c // per_sub, c % per_sub, head_block, 0)

    def bwd_map(head_block):
        return lambda c: (g0, (nc - 1 - c) // per_sub, (nc - 1 - c) % per_sub, head_block, 0)

    blk = (N_LAT, 1, tc, B_HEADS, LANES)
    yblk = (N_LAT, 1, tc, B_HEADS, HD)
    ni8 = HD // LAT_REP
    s0 = jnp.stack([s0f, s0b]).reshape(2, N_LAT, B_HEADS, LAT_REP, ni8, HD)
    s0 = s0.transpose(5, 4, 3, 0, 1, 2).reshape(HD, ni8, LANES)
    yshape = jax.ShapeDtypeStruct((N_LAT, ROW_SUB, T_CTX, B_HEADS, HD), F32)
    return pl.pallas_call(
        functools.partial(_scan_lat_kernel, tc=tc),
        grid=(nc,),
        in_specs=[pl.BlockSpec(blk, fwd_map(0)), pl.BlockSpec(blk, bwd_map(0)),
                  pl.BlockSpec(blk, fwd_map(0)), pl.BlockSpec(blk, bwd_map(1)),
                  pl.BlockSpec(blk, fwd_map(0)), pl.BlockSpec(blk, bwd_map(1)),
                  pl.BlockSpec((HD, ni8, LANES), lambda c: (0, 0, 0))],
        out_specs=[pl.BlockSpec(yblk, lambda c: (0, c // per_sub, c % per_sub, 0, 0)),
                   pl.BlockSpec(yblk, lambda c: (0, (nc - 1 - c) // per_sub, (nc - 1 - c) % per_sub, 0, 0))],
        out_shape=[yshape, yshape],
        scratch_shapes=_operand_bufs() + [pltpu.VMEM((HD, ni8, LANES), F32),
                                          pltpu.VMEM((tc, ni8, LANES), F32)],
        compiler_params=_cparams(("arbitrary",)),
        name="rwkv_scan_lat",
    )(_rows5(rk, B_HEADS), _rows5(rk, B_HEADS), _rows5(wk, 2 * B_HEADS), _rows5(wk, 2 * B_HEADS),
      _rows5(bv, 2 * B_HEADS), _rows5(bv, 2 * B_HEADS), s0)


def _rwkv_post_kernel(yc_ref, ylf_ref, ylb_ref, bonus_ref, g_ref, lng_ref, lnb_ref, o_ref, *, n_ctx_blocks):
    tm = o_ref.shape[0]

    def emit(load):
        outs = []
        for h in range(B_HEADS):
            sl = slice(h * HD, (h + 1) * HD)
            y = load(h)
            yc = y - jnp.mean(y, axis=-1, keepdims=True)
            var = jnp.mean(yc * yc, axis=-1, keepdims=True)
            yn = yc * lax.rsqrt(var + GN_EPS) * lng_ref[:, sl] + lnb_ref[:, sl]
            outs.append((yn + bonus_ref[:, sl]) * g_ref[:, sl])
        o_ref[...] = jnp.concatenate(outs, axis=-1).astype(BF16)

    @pl.when(pl.program_id(0) < n_ctx_blocks)
    def _():
        emit(lambda h: yc_ref[pl.ds(h, tm, stride=2 * B_HEADS), :]
             + yc_ref[pl.ds(B_HEADS + h, tm, stride=2 * B_HEADS), :])

    @pl.when(pl.program_id(0) >= n_ctx_blocks)
    def _():
        emit(lambda h: ylf_ref[pl.ds(h, tm, stride=B_HEADS), :] + ylb_ref[pl.ds(h, tm, stride=B_HEADS), :])


def rwkv_post(y_ctx, y_lat_f, y_lat_b, bonus, g, lng, lnb, e):
    tm = 256
    nb = ROWS_CTX // tm
    rowspec = pl.BlockSpec((tm, B_W), lambda i: (i, 0))
    vecspec = pl.BlockSpec((None, 1, B_W), lambda i: (e, 0, 0))
    latspec = pl.BlockSpec((tm * B_HEADS, HD), lambda i: (jnp.maximum(i - nb, 0), 0))
    return pl.pallas_call(
        functools.partial(_rwkv_post_kernel, n_ctx_blocks=nb),
        grid=(ROWS // tm,),
        in_specs=[pl.BlockSpec((tm * 2 * B_HEADS, HD), lambda i: (jnp.minimum(i, nb - 1), 0)), latspec, latspec,
                  rowspec, rowspec, vecspec, vecspec],
        out_specs=rowspec,
        out_shape=jax.ShapeDtypeStruct((ROWS, B_W), BF16),
        compiler_params=_cparams(("arbitrary",)),
        name="rwkv_post",
    )(y_ctx, y_lat_f, y_lat_b, bonus, g, lng, lnb)


def kernel(x_prompt, x_sample, cache_a_k, cache_a_v, state_b_fwd, state_b_bwd, cache_c_k, cache_c_v, c, c_ctx,
           ada_w, ada_b, norm1_g, norm2_g, final_norm_g, w_in_e, w_out_e, a_q_gain, a_k_gain, b_mu, b_w0, b_w2,
           b_a0, b_a2, b_g2, b_kk, b_ka, b_rk, b_ln_g, b_ln_b, ffn_w_gu, ffn_w_dn, w_in_o, w_out_o, c_rpb,
           router_w, router_b, moe_w_gu, moe_w_dn):
    n_even, n_odd = w_in_e.shape[0], w_in_o.shape[0]
    prm = dict(b_mu=b_mu, b_w0=b_w0, b_w2=b_w2, b_a0=b_a0, b_a2=b_a2, b_g2=b_g2, b_kk=b_kk, b_ka=b_ka,
               b_rk=b_rk)
    cond = jnp.zeros((N_MODS, D), F32).at[0].set(c_ctx).at[1:1 + N_LAT].set(c)
    mods = ada_all(cond, ada_w, ada_b).reshape(DEPTH, N_MODS, 6, D)
    g1 = norm1_g.reshape(DEPTH, 1, D)
    g2 = norm2_g.reshape(DEPTH, 1, D)
    gq = a_q_gain.reshape(n_even, 1, HD)
    gk = a_k_gain.reshape(n_even, 1, HD)
    lng = b_ln_g.reshape(n_even, 1, B_W)
    lnb = b_ln_b.reshape(n_even, 1, B_W)
    ck_a = cache_a_k.reshape(N_LAT, n_even, PAST, A_KV_W)
    cv_a = cache_a_v.reshape(N_LAT, n_even, PAST, A_KV_W)
    ck_c = cache_c_k.reshape(N_LAT, n_odd, PAST, C_W)
    cv_c = cache_c_v.reshape(N_LAT, n_odd, PAST, C_W)
    na_bias = _na_bias_table(c_rpb)
    rw_p = jnp.zeros((n_odd, D, LANES), F32).at[:, :, :N_EXPERTS].set(router_w)
    rb_p = jnp.zeros((n_odd, 1, LANES), F32).at[:, 0, :N_EXPERTS].set(router_b)
    moe_gu_b = moe_w_gu.astype(BF16)
    moe_dn_b = moe_w_dn.astype(BF16)

    x, h = first_norm(x_prompt, x_sample, g1, mods)

    a_caches, c_caches = None, None
    new_sf, new_sb = [], []
    y_final = None
    for l in range(DEPTH):
        if l % 2 == 0:
            e = l // 2
            z = mm_in(h, w_in_e, e, tn=EVEN_IN // 3)
            y_a, k_new, v_new = attn_a_ctx(z, gq, gk, e, a_caches)
            a_caches = (k_new, v_new)
            y_a = attn_a_lat(z, ck_a, cv_a, gq, gk, e, y_a)
            prep = rwkv_prep_all(z, prm, e)
            y_s, st = rwkv_scan_ctx(prep, tc=32)
            y_lf, y_lb = rwkv_scan_lat(prep, state_b_fwd[:, e], state_b_bwd[:, e], tc=64)
            st = st.reshape(2, HD, HD, N_CTX, B_HEADS).transpose(0, 3, 4, 2, 1)
            new_sf.append(st[0])
            new_sb.append(st[1])
            y_b = rwkv_post(y_s.reshape(ROWS * 2 * B_HEADS, HD), y_lf.reshape(ROWS_LAT * B_HEADS, HD),
                            y_lb.reshape(ROWS_LAT * B_HEADS, HD), prep[4], prep[3], lng, lnb, e)
            x, h = mm_out([y_a, y_b], w_out_e, e, x, mods, l, g2, l)
            x, h = ffn(h, ffn_w_gu, ffn_w_dn, e, x, mods, l, g1)
        else:
            o = l // 2
            z = mm_in(h, w_in_o, o, tn=C_W)
            y, k_new, v_new = attn_c_ctx(z, o, n_odd, c_caches)
            c_caches = (k_new, v_new)
            y = attn_na(z, ck_c, cv_c, na_bias, o, y)
            x, h = mm_out([y], w_out_o, o, x, mods, l, g2, l)
            routing = router(x, g2, mods, l, rw_p, rb_p, o)
            y_moe = moe_routed(h, routing, moe_gu_b, moe_dn_b, o)
            if l + 1 < DEPTH:
                x, h = resid(x, y_moe, mods, l, g1, False)
            else:
                y_final = resid(x, y_moe, mods, l, final_norm_g[None], True)

    y_prompt = y_final[0].reshape(N_CTX, T_CTX, D)
    y_sample = y_final[1].reshape(N_LAT, T_LAT, D)
    return (y_prompt, y_sample,
            a_caches[0].reshape(N_CTX, n_even, T_CTX, A_KV_HEADS, HD),
            a_caches[1].reshape(N_CTX, n_even, T_CTX, A_KV_HEADS, HD),
            jnp.stack(new_sf, axis=1), jnp.stack(new_sb, axis=1),
            c_caches[0].reshape(N_CTX, n_odd, T_CTX, C_HEADS, HD),
            c_caches[1].reshape(N_CTX, n_odd, T_CTX, C_HEADS, HD))
```

```python
import functools

import numpy as np
import jax
import jax.numpy as jnp
from jax import lax
from jax.experimental import pallas as pl
from jax.experimental.pallas import tpu as pltpu

F32 = jnp.float32
BF16 = jnp.bfloat16
HIGHEST = lax.Precision.HIGHEST

D = 1024
N_CTX, T_CTX = 16, 256
N_LAT, T_LAT = 2, 1024
ROWS_CTX = N_CTX * T_CTX
ROWS_LAT = N_LAT * T_LAT
ROWS = ROWS_CTX + ROWS_LAT
DEPTH = 4
GRID_W = 64
HD = 64
A_HEADS, A_KV_HEADS, B_HEADS, C_HEADS = 8, 2, 8, 16
A_W, A_KV_W, B_W, C_W = A_HEADS * HD, A_KV_HEADS * HD, B_HEADS * HD, C_HEADS * HD
A_IN = A_W + 2 * A_KV_W
LORA_W, LORA_A, LORA_G = 64, 64, 128
B_IN = 3 * B_W + 2 * LORA_W + 2 * LORA_A + LORA_G
EVEN_IN = A_IN + B_IN
PAST = 512
NA_ROWS, NA_COLS = 8, 16
D_FF = 2816
N_EXPERTS = 8
D_EXPERT = 1408
ROPE_THETA = 10000.0
EPS = 1e-6
GN_EPS = 64e-5
NEG_INF = -1e30
ATT_SCALE = HD ** -0.5

LANES = 128
SUBLANES = 8
VMEM_LIMIT = 56 * 1024 * 1024

N_MODS = 8

ROW_SUB = T_LAT // T_CTX
ROW_GROUPS = ROWS // (ROW_SUB * T_CTX)
CTX_GROUPS = N_CTX // ROW_SUB


def _cparams(sem):
    return pltpu.CompilerParams(dimension_semantics=sem, vmem_limit_bytes=VMEM_LIMIT)


def _untouched():
    return pl.BlockSpec(memory_space=pl.ANY)


def _mod_index(row_start):
    return jnp.where(row_start < ROWS_CTX, 0, 1 + (row_start - ROWS_CTX) // T_LAT)


def _modspec(layer, tm, nargs=1):
    if nargs == 1:
        return pl.BlockSpec((None, 1, 6, D), lambda i: (layer, _mod_index(i * tm), 0, 0))
    return pl.BlockSpec((None, 1, 6, D), lambda i, j: (layer, _mod_index(i * tm), 0, 0))


def _gainspec(layer, nargs=1):
    if nargs == 1:
        return pl.BlockSpec((None, 1, D), lambda i: (layer, 0, 0))
    return pl.BlockSpec((None, 1, D), lambda i, j: (layer, 0, 0))


def _modnorm(x, g, shift, scale):
    ms = jnp.mean(x * x, axis=-1, keepdims=True)
    return (x * lax.rsqrt(ms + EPS) * g) * (1.0 + scale) + shift


def _silu(x):
    return x * jax.nn.sigmoid(x)


def _ada_kernel(c_ref, w_ref, b_ref, o_ref):
    s = _silu(c_ref[...]).astype(BF16)
    o_ref[0] = jnp.dot(s, w_ref[0].astype(BF16), preferred_element_type=F32) + b_ref[0]


def ada_all(cond, ada_w, ada_b):
    tn = 1536
    n = 6 * D
    return pl.pallas_call(
        _ada_kernel,
        grid=(DEPTH, n // tn),
        in_specs=[pl.BlockSpec((N_MODS, D), lambda l, j: (0, 0)),
                  pl.BlockSpec((1, D, tn), lambda l, j: (l, 0, j)),
                  pl.BlockSpec((1, 1, tn), lambda l, j: (l, 0, j))],
        out_specs=pl.BlockSpec((1, N_MODS, tn), lambda l, j: (l, 0, j)),
        out_shape=jax.ShapeDtypeStruct((DEPTH, N_MODS, n), F32),
        compiler_params=_cparams(("arbitrary", "arbitrary")),
        name="ada",
    )(cond, ada_w, ada_b.reshape(DEPTH, 1, n))


def _first_norm_kernel(xp_ref, xs_ref, g_ref, mod_ref, x_ref, h_ref, *, n_ctx_blocks):
    i = pl.program_id(0)

    def emit(src):
        x = src[...]
        x_ref[...] = x
        h_ref[...] = _modnorm(x, g_ref[...], mod_ref[0, 0:1, :], mod_ref[0, 1:2, :]).astype(BF16)

    @pl.when(i < n_ctx_blocks)
    def _():
        emit(xp_ref)

    @pl.when(i >= n_ctx_blocks)
    def _():
        emit(xs_ref)


def first_norm(x_prompt, x_sample, gains, mods):
    tm = 512
    nc = ROWS_CTX // tm
    rowspec = pl.BlockSpec((tm, D), lambda i: (i, 0))
    return pl.pallas_call(
        functools.partial(_first_norm_kernel, n_ctx_blocks=nc),
        grid=(ROWS // tm,),
        in_specs=[pl.BlockSpec((tm, D), lambda i: (jnp.minimum(i, nc - 1), 0)),
                  pl.BlockSpec((tm, D), lambda i: (jnp.maximum(i - nc, 0), 0)),
                  _gainspec(0), _modspec(0, tm)],
        out_specs=[rowspec, rowspec],
        out_shape=[jax.ShapeDtypeStruct((ROWS, D), F32), jax.ShapeDtypeStruct((ROWS, D), BF16)],
        compiler_params=_cparams(("arbitrary",)),
        name="first_norm",
    )(x_prompt.reshape(ROWS_CTX, D), x_sample.reshape(ROWS_LAT, D), gains, mods)


def _mm_in_kernel(h_ref, w_ref, o_ref, wb_ref):
    @pl.when(pl.program_id(1) == 0)
    def _():
        wb_ref[...] = w_ref[...].astype(BF16)

    o_ref[...] = jnp.dot(h_ref[...], wb_ref[...], preferred_element_type=F32)


def mm_in(h, w, layer, tn, tm=1024):
    _, k, n = w.shape
    return pl.pallas_call(
        _mm_in_kernel,
        grid=(n // tn, ROWS // tm),
        in_specs=[pl.BlockSpec((tm, k), lambda j, i: (i, 0)),
                  pl.BlockSpec((None, k, tn), lambda j, i: (layer, 0, j))],
        out_specs=pl.BlockSpec((tm, tn), lambda j, i: (i, j)),
        out_shape=jax.ShapeDtypeStruct((ROWS, n), F32),
        scratch_shapes=[pltpu.VMEM((k, tn), BF16)],
        compiler_params=_cparams(("arbitrary", "arbitrary")),
        name="mm_in",
    )(h, w)


def _residual_epilogue(x, acc, mod_ref, nmod_ref, g_ref, xo_ref, ho_ref, gate_idx, shift_idx, scale_idx, final):
    gate = mod_ref[0, gate_idx:gate_idx + 1, :]
    xn = x + gate * acc
    if final:
        ms = jnp.mean(xn * xn, axis=-1, keepdims=True)
        ho_ref[...] = xn * lax.rsqrt(ms + EPS) * g_ref[...]
    else:
        xo_ref[...] = xn
        shift = nmod_ref[0, shift_idx:shift_idx + 1, :]
        scale = nmod_ref[0, scale_idx:scale_idx + 1, :]
        ho_ref[...] = _modnorm(xn, g_ref[...], shift, scale).astype(BF16)


def _mm_out_kernel(*refs, n_parts, gate_idx, shift_idx, scale_idx):
    y_refs = refs[:n_parts]
    w_refs = refs[n_parts:2 * n_parts]
    x_ref, mod_ref, nmod_ref, g_ref, xo_ref, ho_ref = refs[2 * n_parts:2 * n_parts + 6]
    wb_refs = refs[2 * n_parts + 6:]

    @pl.when(pl.program_id(0) == 0)
    def _():
        for w_ref, wb_ref in zip(w_refs, wb_refs):
            wb_ref[...] = w_ref[...].astype(BF16)

    acc = None
    for y_ref, wb_ref in zip(y_refs, wb_refs):
        p = jnp.dot(y_ref[...], wb_ref[...], preferred_element_type=F32)
        acc = p if acc is None else acc + p
    _residual_epilogue(x_ref[...], acc, mod_ref, nmod_ref, g_ref, xo_ref, ho_ref,
                       gate_idx, shift_idx, scale_idx, False)


def mm_out(parts, w, layer_w, x, mods, layer, gains, gain_layer):
    tm = 1024
    kp = parts[0].shape[1]
    n_parts = len(parts)
    rowspec = pl.BlockSpec((tm, D), lambda i: (i, 0))
    return pl.pallas_call(
        functools.partial(_mm_out_kernel, n_parts=n_parts, gate_idx=2, shift_idx=3, scale_idx=4),
        grid=(ROWS // tm,),
        in_specs=[pl.BlockSpec((tm, kp), lambda i: (i, 0)) for _ in parts]
                 + [pl.BlockSpec((None, kp, D), lambda i, p=p: (layer_w, p, 0)) for p in range(n_parts)]
                 + [rowspec, _modspec(layer, tm), _modspec(layer, tm), _gainspec(gain_layer)],
        out_specs=[rowspec, rowspec],
        out_shape=[jax.ShapeDtypeStruct((ROWS, D), F32), jax.ShapeDtypeStruct((ROWS, D), BF16)],
        scratch_shapes=[pltpu.VMEM((kp, D), BF16) for _ in parts],
        compiler_params=_cparams(("arbitrary",)),
        name="mm_out",
    )(*parts, *([w] * n_parts), x, mods, mods, gains)


FFN_TF = 256
FFN_NF = D_FF // FFN_TF


def _ffn_kernel(h_ref, wg_ref, wu_ref, wd_ref, x_ref, mod_ref, nmod_ref, g_ref, xo_ref, ho_ref,
                wg_s, wu_s, wd_s, acc_ref):
    i = pl.program_id(0)
    f = pl.program_id(1)

    @pl.when(i == 0)
    def _():
        wg_s[f] = wg_ref[...].astype(BF16)
        wu_s[f] = wu_ref[...].astype(BF16)
        wd_s[f] = wd_ref[...].astype(BF16)

    @pl.when(f == 0)
    def _():
        acc_ref[...] = jnp.zeros_like(acc_ref)

    h = h_ref[...]
    gp = jnp.dot(h, wg_s[f], preferred_element_type=F32)
    up = jnp.dot(h, wu_s[f], preferred_element_type=F32)
    a = (_silu(gp) * up).astype(BF16)
    acc_ref[...] += jnp.dot(a, wd_s[f], preferred_element_type=F32)

    @pl.when(f == FFN_NF - 1)
    def _():
        _residual_epilogue(x_ref[...], acc_ref[...], mod_ref, nmod_ref, g_ref, xo_ref, ho_ref, 5, 0, 1, False)


def ffn(h, w_gu, w_dn, e, x, mods, layer, gains):
    tm = 1024
    tf, nf = FFN_TF, FFN_NF

    def once(i, f):
        return jnp.where(i == 0, f, nf - 1)

    rowspec = pl.BlockSpec((tm, D), lambda i, f: (i, 0))
    return pl.pallas_call(
        _ffn_kernel,
        grid=(ROWS // tm, nf),
        in_specs=[rowspec,
                  pl.BlockSpec((None, D, tf), lambda i, f: (e, 0, once(i, f))),
                  pl.BlockSpec((None, D, tf), lambda i, f: (e, 0, nf + once(i, f))),
                  pl.BlockSpec((None, tf, D), lambda i, f: (e, once(i, f), 0)),
                  rowspec, _modspec(layer, tm, 2), _modspec(layer + 1, tm, 2), _gainspec(layer + 1, 2)],
        out_specs=[rowspec, rowspec],
        out_shape=[jax.ShapeDtypeStruct((ROWS, D), F32), jax.ShapeDtypeStruct((ROWS, D), BF16)],
        scratch_shapes=[pltpu.VMEM((nf, D, tf), BF16),
                        pltpu.VMEM((nf, D, tf), BF16),
                        pltpu.VMEM((nf, tf, D), BF16),
                        pltpu.VMEM((tm, D), F32)],
        compiler_params=_cparams(("arbitrary", "arbitrary")),
        name="ffn",
    )(h, w_gu, w_gu, w_dn, x, mods, mods, gains)


MOE_TILE = 1024
MOE_CHUNK = 320


def _router_kernel(x_ref, g_ref, mod_ref, rw_ref, rb_ref, gates_ref, pos_ref, gates_t_ref, pos_t_ref, carry_ref):
    h = _modnorm(x_ref[...], g_ref[...], mod_ref[0, 3:4, :], mod_ref[0, 4:5, :])
    logits = jnp.dot(h, rw_ref[...], precision=HIGHEST, preferred_element_type=F32)
    lane = lax.broadcasted_iota(jnp.int32, logits.shape, 1)
    sel = jnp.where(lane < N_EXPERTS, logits + rb_ref[...], -jnp.inf)
    m1 = jnp.max(sel, axis=-1, keepdims=True)
    i1 = jnp.min(jnp.where(sel == m1, lane, LANES), axis=-1, keepdims=True)
    sel2 = jnp.where(lane == i1, -jnp.inf, sel)
    m2 = jnp.max(sel2, axis=-1, keepdims=True)
    i2 = jnp.min(jnp.where(sel2 == m2, lane, LANES), axis=-1, keepdims=True)
    l1 = jnp.sum(jnp.where(lane == i1, logits, 0.0), axis=-1, keepdims=True)
    l2 = jnp.sum(jnp.where(lane == i2, logits, 0.0), axis=-1, keepdims=True)
    mx = jnp.maximum(l1, l2)
    e1 = jnp.exp(l1 - mx)
    e2 = jnp.exp(l2 - mx)
    den = e1 + e2
    gates = jnp.where(lane == i1, e1 / den, 0.0) + jnp.where(lane == i2, e2 / den, 0.0)
    gates_ref[...] = gates

    tm = gates.shape[0]
    routed = gates > 0.0
    tri = (lax.broadcasted_iota(jnp.int32, (tm, tm), 0) > lax.broadcasted_iota(jnp.int32, (tm, tm), 1))
    local = jnp.dot(jnp.where(tri, 1.0, 0.0).astype(BF16), jnp.where(routed, 1.0, 0.0).astype(BF16),
                    preferred_element_type=F32)

    @pl.when(pl.program_id(0) % (MOE_TILE // tm) == 0)
    def _():
        carry_ref[...] = jnp.zeros_like(carry_ref)

    pos = local + carry_ref[...]
    carry_ref[...] += jnp.sum(jnp.where(routed, 1.0, 0.0), axis=0, keepdims=True)
    pos_ref[...] = pos
    gates_t_ref[...] = gates.T[:SUBLANES]
    pos_t_ref[...] = pos.T[:SUBLANES]


def router(x, gains, mods, layer, rw_p, rb_p, o):
    tm = 512
    rowspec = pl.BlockSpec((tm, LANES), lambda i: (i, 0))
    colspec = pl.BlockSpec((SUBLANES, tm), lambda i: (0, i))
    return pl.pallas_call(
        _router_kernel,
        grid=(ROWS // tm,),
        in_specs=[pl.BlockSpec((tm, D), lambda i: (i, 0)),
                  _gainspec(layer), _modspec(layer, tm),
                  pl.BlockSpec((None, D, LANES), lambda i: (o, 0, 0)),
                  pl.BlockSpec((None, 1, LANES), lambda i: (o, 0, 0))],
        out_specs=[rowspec, rowspec, colspec, colspec],
        out_shape=[jax.ShapeDtypeStruct((ROWS, LANES), F32), jax.ShapeDtypeStruct((ROWS, LANES), F32),
                   jax.ShapeDtypeStruct((SUBLANES, ROWS), F32), jax.ShapeDtypeStruct((SUBLANES, ROWS), F32)],
        scratch_shapes=[pltpu.VMEM((1, LANES), F32)],
        compiler_params=_cparams(("arbitrary",)),
        name="router",
    )(x, gains, mods, rw_p, rb_p)


def _moe_kernel(cnt_ref, h_ref, gates_ref, pos_ref, gates_t_ref, pos_t_ref, wgu_ref, wdn_ref, o_ref, acc_ref):
    tile = pl.program_id(0)
    e = pl.program_id(1)
    tt = h_ref.shape[0]

    @pl.when(e == 0)
    def _():
        acc_ref[...] = jnp.zeros_like(acc_ref)

    lane = lax.broadcasted_iota(jnp.int32, (tt, LANES), 1)
    g_col = jnp.sum(jnp.where(lane == e, gates_ref[...], 0.0), axis=1, keepdims=True)
    p_col = jnp.sum(jnp.where(lane == e, pos_ref[...], 0.0), axis=1, keepdims=True)
    sub = lax.broadcasted_iota(jnp.int32, (SUBLANES, tt), 0)
    g_row = jnp.sum(jnp.where(sub == e, gates_t_ref[...], 0.0), axis=0, keepdims=True)
    p_row = jnp.sum(jnp.where(sub == e, pos_t_ref[...], 0.0), axis=0, keepdims=True)
    p_col = jnp.where(g_col > 0.0, p_col, -1.0)
    p_row = jnp.where(g_row > 0.0, p_row, -1.0)

    def chunk(ci, carry):
        base = (ci * MOE_CHUNK).astype(F32)
        take = p_row == lax.broadcasted_iota(jnp.int32, (MOE_CHUNK, tt), 0).astype(F32) + base
        xs = jnp.dot(jnp.where(take, 1.0, 0.0).astype(BF16), h_ref[...], preferred_element_type=F32)
        gu = jnp.dot(xs.astype(BF16), wgu_ref[...], preferred_element_type=F32)
        a = (_silu(gu[:, :D_EXPERT]) * gu[:, D_EXPERT:]).astype(BF16)
        y = jnp.dot(a, wdn_ref[...], preferred_element_type=F32)
        gate = jnp.sum(jnp.where(take, g_row, 0.0), axis=1, keepdims=True)
        put = p_col == lax.broadcasted_iota(jnp.int32, (tt, MOE_CHUNK), 1).astype(F32) + base
        acc_ref[...] += jnp.dot(jnp.where(put, 1.0, 0.0).astype(BF16), (gate * y).astype(BF16),
                                preferred_element_type=F32)
        return carry

    count = cnt_ref[tile * N_EXPERTS + e]
    lax.fori_loop(0, (count + MOE_CHUNK - 1) // MOE_CHUNK, chunk, 0)

    @pl.when(e == N_EXPERTS - 1)
    def _():
        o_ref[...] = acc_ref[...]


def moe_routed(h, routing, w_gu, w_dn, o):
    gates, pos, gates_t, pos_t = routing
    tt = MOE_TILE
    n_tiles = ROWS // tt
    counts = jnp.sum((gates[:, :N_EXPERTS] > 0.0).reshape(n_tiles, tt, N_EXPERTS), axis=1)
    counts = counts.astype(jnp.int32).reshape(n_tiles * N_EXPERTS)
    rowspec = pl.BlockSpec((tt, LANES), lambda t, e, c: (t, 0))
    colspec = pl.BlockSpec((SUBLANES, tt), lambda t, e, c: (0, t))
    grid_spec = pltpu.PrefetchScalarGridSpec(
        num_scalar_prefetch=1,
        grid=(n_tiles, N_EXPERTS),
        in_specs=[pl.BlockSpec((tt, D), lambda t, e, c: (t, 0)),
                  rowspec, rowspec, colspec, colspec,
                  pl.BlockSpec((None, None, D, 2 * D_EXPERT), lambda t, e, c: (o, e, 0, 0)),
                  pl.BlockSpec((None, None, D_EXPERT, D), lambda t, e, c: (o, e, 0, 0))],
        out_specs=pl.BlockSpec((tt, D), lambda t, e, c: (t, 0)),
        scratch_shapes=[pltpu.VMEM((tt, D), F32)])
    return pl.pallas_call(
        _moe_kernel,
        grid_spec=grid_spec,
        out_shape=jax.ShapeDtypeStruct((ROWS, D), F32),
        compiler_params=_cparams(("arbitrary", "arbitrary")),
        name="moe",
    )(counts, h, gates, pos, gates_t, pos_t, w_gu, w_dn)


def _resid_kernel(x_ref, y_ref, mod_ref, nmod_ref, g_ref, *out_refs, final):
    if not final:
        xo_ref, ho_ref = out_refs
        _residual_epilogue(x_ref[...], y_ref[...], mod_ref, nmod_ref, g_ref, xo_ref, ho_ref, 5, 0, 1, False)
        return
    n_ctx_blocks = ROWS_CTX // x_ref.shape[0]
    for ho_ref, mine in zip(out_refs, (pl.program_id(0) < n_ctx_blocks, pl.program_id(0) >= n_ctx_blocks)):
        @pl.when(mine)
        def _(ho_ref=ho_ref):
            _residual_epilogue(x_ref[...], y_ref[...], mod_ref, nmod_ref, g_ref, None, ho_ref, 5, 0, 1, True)


def resid(x, y, mods, layer, gains, final):
    tm = 512
    rowspec = pl.BlockSpec((tm, D), lambda i: (i, 0))
    if final:
        nb = ROWS_CTX // tm
        out_specs = [pl.BlockSpec((tm, D), lambda i: (jnp.minimum(i, nb - 1), 0)),
                     pl.BlockSpec((tm, D), lambda i: (jnp.maximum(i - nb, 0), 0))]
        out_shape = [jax.ShapeDtypeStruct((ROWS_CTX, D), F32), jax.ShapeDtypeStruct((ROWS_LAT, D), F32)]
        nmod, gain = _modspec(layer, tm), pl.BlockSpec((1, D), lambda i: (0, 0))
    else:
        out_specs = [rowspec, rowspec]
        out_shape = [jax.ShapeDtypeStruct((ROWS, D), F32), jax.ShapeDtypeStruct((ROWS, D), BF16)]
        nmod, gain = _modspec(layer + 1, tm), _gainspec(layer + 1)
    return pl.pallas_call(
        functools.partial(_resid_kernel, final=final),
        grid=(ROWS // tm,),
        in_specs=[rowspec, rowspec, _modspec(layer, tm), nmod, gain],
        out_specs=out_specs,
        out_shape=out_shape,
        compiler_params=_cparams(("arbitrary",)),
        name="resid",
    )(x, y, mods, mods, gains)


def _softmax_pv(scores, values):
    m = None
    for s in scores:
        mi = jnp.max(s, axis=-1, keepdims=True)
        m = mi if m is None else jnp.maximum(m, mi)
    num, den = None, None
    for s, v in zip(scores, values):
        p = jnp.exp(s - m)
        li = jnp.sum(p, axis=-1, keepdims=True)
        oi = jnp.dot(p.astype(BF16), v, preferred_element_type=F32)
        num = oi if num is None else num + oi
        den = li if den is None else den + li
    return num / den


def _qk(q, k):
    return lax.dot_general(q, k, (((1,), (1,)), ((), ())), preferred_element_type=F32)


def _head_rmsnorm(x, gain, n_heads):
    parts = []
    for h in range(n_heads):
        xh = x[:, h * HD:(h + 1) * HD]
        ms = jnp.mean(xh * xh, axis=-1, keepdims=True)
        parts.append(xh * lax.rsqrt(ms + EPS) * gain)
    return parts


def _rope128(x, cos, sin_signed):
    lane = lax.broadcasted_iota(jnp.int32, x.shape, 1)
    up = pltpu.roll(x, LANES - 16, 1)
    dn = pltpu.roll(x, 16, 1)
    partner = jnp.where((lane % 32) < 16, up, dn)
    return x * cos + partner * sin_signed


def _rope_tables():
    t = np.arange(T_LAT)
    n_f = HD // 4
    inv = ROPE_THETA ** (-np.arange(n_f, dtype=np.float32) / n_f)
    cos = np.zeros((T_LAT, HD), np.float32)
    sin = np.zeros((T_LAT, HD), np.float32)
    for half, pos in ((0, t // GRID_W), (1, t % GRID_W)):
        ang = pos[:, None].astype(np.float32) * inv[None, :]
        c, s = np.cos(ang), np.sin(ang)
        base = half * (HD // 2)
        cos[:, base:base + n_f] = c
        cos[:, base + n_f:base + 2 * n_f] = c
        sin[:, base:base + n_f] = -s
        sin[:, base + n_f:base + 2 * n_f] = s
    return np.tile(cos, (1, 2)), np.tile(sin, (1, 2))


def _attn_a_ctx_kernel(q_ref, kv_ref, qg_ref, kg_ref, *rest):
    o_ref, ko_ref, vo_ref = rest[-3:]
    q = q_ref[...]
    kv = kv_ref[...]
    k = kv[:, :A_KV_W]
    v = kv[:, A_KV_W:]
    kn = _head_rmsnorm(k, kg_ref[...], A_KV_HEADS)
    ko_ref[...] = jnp.concatenate(kn, axis=-1)
    vo_ref[...] = v
    qn = _head_rmsnorm(q, qg_ref[...], A_HEADS)
    group = A_HEADS // A_KV_HEADS
    t = q.shape[0]
    outs = []
    for g in range(A_KV_HEADS):
        qs = jnp.concatenate([(qn[g * group + i] * ATT_SCALE).astype(BF16) for i in range(group)], axis=0)
        o = _softmax_pv([_qk(qs, kn[g].astype(BF16))], [v[:, g * HD:(g + 1) * HD].astype(BF16)])
        outs += [o[i * t:(i + 1) * t] for i in range(group)]
    o_ref[...] = jnp.concatenate(outs, axis=-1).astype(BF16)


def attn_a_ctx(z, gq, gk, e, prev_caches):
    t = T_CTX
    n_even = gq.shape[0]
    cache_spec = pl.BlockSpec((None, None, t, A_KV_W), lambda b: (b, e, 0, 0))
    cache_shape = jax.ShapeDtypeStruct((N_CTX, n_even, t, A_KV_W), F32)
    extra = [] if prev_caches is None else list(prev_caches)
    aliases = {} if prev_caches is None else {4: 1, 5: 2}
    return pl.pallas_call(
        _attn_a_ctx_kernel,
        grid=(N_CTX,),
        in_specs=[pl.BlockSpec((t, A_W), lambda b: (b, 0)),
                  pl.BlockSpec((t, 2 * A_KV_W), lambda b: (b, A_W // (2 * A_KV_W))),
                  pl.BlockSpec((None, 1, HD), lambda b: (e, 0, 0)),
                  pl.BlockSpec((None, 1, HD), lambda b: (e, 0, 0))] + [_untouched() for _ in extra],
        out_specs=[pl.BlockSpec((t, A_W), lambda b: (b, 0)), cache_spec, cache_spec],
        out_shape=[jax.ShapeDtypeStruct((ROWS, A_W), BF16), cache_shape, cache_shape],
        input_output_aliases=aliases,
        compiler_params=_cparams(("arbitrary",)),
        name="attn_a_ctx",
    )(z, z, gq, gk, *extra)


A_LAT_TQ = 256


def _attn_a_lat_kernel(q_ref, kv_ref, ck_ref, cv_ref, qg_ref, kg_ref, cq_ref, sq_ref, ckk_ref, skk_ref,
                       prev_ref, o_ref, k_s, v_s):
    del prev_ref

    @pl.when(pl.program_id(1) == 0)
    def _():
        kv = kv_ref[...]
        kn = jnp.concatenate(_head_rmsnorm(kv[:, :A_KV_W], kg_ref[...], A_KV_HEADS), axis=-1)
        k_s[...] = _rope128(kn, ckk_ref[...], skk_ref[...]).astype(BF16)
        v_s[...] = kv[:, A_KV_W:].astype(BF16)

    qn = _head_rmsnorm(q_ref[...], qg_ref[...], A_HEADS)
    cos, sin = cq_ref[...], sq_ref[...]
    qr = []
    for c in range(A_HEADS // 2):
        slab = _rope128(jnp.concatenate(qn[2 * c:2 * c + 2], axis=-1), cos, sin)
        qr.append(slab[:, :HD])
        qr.append(slab[:, HD:])
    ck = ck_ref[...].astype(BF16)
    cv = cv_ref[...].astype(BF16)
    kk = k_s[...]
    vv = v_s[...]
    group = A_HEADS // A_KV_HEADS
    outs = []
    for h in range(A_HEADS):
        g = h // group
        sl = slice(g * HD, (g + 1) * HD)
        qh = (qr[h] * ATT_SCALE).astype(BF16)
        outs.append(_softmax_pv([_qk(qh, kk[:, sl]), _qk(qh, ck[:, sl])], [vv[:, sl], cv[:, sl]]))
    o_ref[...] = jnp.concatenate(outs, axis=-1).astype(BF16)


def attn_a_lat(z, ck, cv, gq, gk, e, y_prev):
    tq = A_LAT_TQ
    nq = T_LAT // tq
    cos, sin = _rope_tables()
    cos, sin = jnp.asarray(cos), jnp.asarray(sin)
    row0 = ROWS_CTX // tq
    seq0 = ROWS_CTX // T_LAT
    cache_spec = pl.BlockSpec((None, None, PAST, A_KV_W), lambda b, i: (b, e, 0, 0))
    return pl.pallas_call(
        _attn_a_lat_kernel,
        grid=(N_LAT, nq),
        in_specs=[pl.BlockSpec((tq, A_W), lambda b, i: (row0 + b * nq + i, 0)),
                  pl.BlockSpec((T_LAT, 2 * A_KV_W), lambda b, i: (seq0 + b, A_W // (2 * A_KV_W))),
                  cache_spec, cache_spec,
                  pl.BlockSpec((None, 1, HD), lambda b, i: (e, 0, 0)),
                  pl.BlockSpec((None, 1, HD), lambda b, i: (e, 0, 0)),
                  pl.BlockSpec((tq, LANES), lambda b, i: (i, 0)),
                  pl.BlockSpec((tq, LANES), lambda b, i: (i, 0)),
                  pl.BlockSpec((T_LAT, LANES), lambda b, i: (0, 0)),
                  pl.BlockSpec((T_LAT, LANES), lambda b, i: (0, 0)),
                  _untouched()],
        out_specs=pl.BlockSpec((tq, A_W), lambda b, i: (row0 + b * nq + i, 0)),
        out_shape=jax.ShapeDtypeStruct((ROWS, A_W), BF16),
        scratch_shapes=[pltpu.VMEM((T_LAT, A_KV_W), BF16),
                        pltpu.VMEM((T_LAT, A_KV_W), BF16)],
        input_output_aliases={10: 0},
        compiler_params=_cparams(("arbitrary", "arbitrary")),
        name="attn_a_lat",
    )(z, z, ck, cv, gq, gk, cos, sin, cos, sin, y_prev)


def _attn_c_ctx_kernel(q_ref, k_ref, v_ref, *rest):
    o_ref, ko_ref, vo_ref = rest[-3:]
    q = q_ref[...]
    k = k_ref[...]
    v = v_ref[...]
    ko_ref[...] = k
    vo_ref[...] = v
    outs = []
    for h in range(q.shape[1] // HD):
        sl = slice(h * HD, (h + 1) * HD)
        qh = (q[:, sl] * ATT_SCALE).astype(BF16)
        outs.append(_softmax_pv([_qk(qh, k[:, sl].astype(BF16))], [v[:, sl].astype(BF16)]))
    o_ref[...] = jnp.concatenate(outs, axis=-1).astype(BF16)


C_CTX_HEADS_PER_STEP = 8


def attn_c_ctx(z, o, n_odd, prev_caches):
    t = T_CTX
    wblk = C_CTX_HEADS_PER_STEP * HD
    nhp = C_W // wblk
    cache_spec = pl.BlockSpec((None, None, t, wblk), lambda b, p: (b, o, 0, p))
    cache_shape = jax.ShapeDtypeStruct((N_CTX, n_odd, t, C_W), F32)
    extra = [] if prev_caches is None else list(prev_caches)
    aliases = {} if prev_caches is None else {3: 1, 4: 2}
    return pl.pallas_call(
        _attn_c_ctx_kernel,
        grid=(N_CTX, nhp),
        in_specs=[pl.BlockSpec((t, wblk), lambda b, p: (b, p)),
                  pl.BlockSpec((t, wblk), lambda b, p: (b, nhp + p)),
                  pl.BlockSpec((t, wblk), lambda b, p: (b, 2 * nhp + p))] + [_untouched() for _ in extra],
        out_specs=[pl.BlockSpec((t, wblk), lambda b, p: (b, p)), cache_spec, cache_spec],
        out_shape=[jax.ShapeDtypeStruct((ROWS, C_W), BF16), cache_shape, cache_shape],
        input_output_aliases=aliases,
        compiler_params=_cparams(("arbitrary", "arbitrary")),
        name="attn_c_ctx",
    )(z, z, z, *extra)


NA_GRID_ROWS = T_LAT // GRID_W
NA_WIN = NA_ROWS * GRID_W
NA_DR = 2 * NA_ROWS - 1
NA_DC = 2 * NA_COLS - 1


def _na_row_start(r):
    return min(max(r - NA_ROWS // 2, 0), NA_GRID_ROWS - NA_ROWS)


def _na_groups():
    groups, r = [], 0
    while r < NA_GRID_ROWS:
        r1 = r
        while r1 + 1 < NA_GRID_ROWS and (_na_row_start(r1 + 1) == _na_row_start(r) or r1 + 1 - r < 4):
            r1 += 1
        lo = _na_row_start(r)
        hi = _na_row_start(r1) + NA_ROWS
        pairs = -(-(hi - lo) // 2)
        lo = min(lo, NA_GRID_ROWS - 2 * pairs)
        groups.append((r, r1, lo, pairs))
        r = r1 + 1
    return groups


def _na_pair_codes():
    codes = []
    plan = []
    for r0, r1, lo, pairs in _na_groups():
        rows = []
        for r in range(r0, r1 + 1):
            rs = _na_row_start(r)
            row = []
            for p in range(pairs):
                code = tuple((kr - r + NA_ROWS - 1) if rs <= kr < rs + NA_ROWS else None
                             for kr in (lo + 2 * p, lo + 2 * p + 1))
                if code not in codes:
                    codes.append(code)
                row.append(codes.index(code))
            rows.append(row)
        plan.append(rows)
    return codes, plan


def _na_bias_table(rpb):
    n_l = rpb.shape[0]
    col = np.arange(GRID_W)
    cs = np.clip(col - NA_COLS // 2, 0, GRID_W - NA_COLS)
    col_in = (col[None, :] >= cs[:, None]) & (col[None, :] < cs[:, None] + NA_COLS)
    period = GRID_W + 1
    seq = jnp.concatenate([rpb, jnp.zeros((n_l, C_HEADS, NA_DR, period - NA_DC), F32)], axis=-1)
    seq = jnp.roll(seq, -(NA_COLS - 1), axis=-1)
    tile = jnp.tile(seq, (1, 1, 1, GRID_W))[..., :GRID_W * GRID_W].reshape(n_l, C_HEADS, NA_DR, GRID_W, GRID_W)
    tile = jnp.where(jnp.asarray(col_in), tile, NEG_INF)
    masked = jnp.full((n_l, C_HEADS, GRID_W, GRID_W), NEG_INF, F32)
    pick = lambda dr: masked if dr is None else tile[:, :, dr]
    codes, _ = _na_pair_codes()
    return jnp.stack([jnp.concatenate([pick(a), pick(b)], axis=-1) for a, b in codes], axis=2)


def _attn_na_kernel(q_ref, k_ref, v_ref, ck_ref, cv_ref, bias_ref, prev_ref, o_ref):
    del prev_ref
    _, plan = _na_pair_codes()
    heads = []
    for h in range(LANES // HD):
        sl = slice(h * HD, (h + 1) * HD)
        q = (q_ref[:, sl] * ATT_SCALE).astype(BF16)
        k = k_ref[:, sl].astype(BF16)
        v = v_ref[:, sl].astype(BF16)
        ck = ck_ref[:, sl].astype(BF16)
        cv = cv_ref[:, sl].astype(BF16)
        rows = []
        for (r0, r1, lo, pairs), codes in zip(_na_groups(), plan):
            qg = q[r0 * GRID_W:(r1 + 1) * GRID_W]
            kw = k[lo * GRID_W:(lo + 2 * pairs) * GRID_W]
            vw = v[lo * GRID_W:(lo + 2 * pairs) * GRID_W]
            bias = jnp.concatenate(
                [jnp.concatenate([bias_ref[h, c] for c in row], axis=-1) for row in codes], axis=0)
            s_nb = _qk(qg, kw) + bias
            s_cx = _qk(qg, ck)
            rows.append(_softmax_pv([s_nb, s_cx], [vw, cv]))
        heads.append(jnp.concatenate(rows, axis=0))
    o_ref[...] = jnp.concatenate(heads, axis=-1).astype(BF16)


def attn_na(z, ck, cv, bias, o, y_prev):
    nhp = C_W // LANES
    hpb = LANES // HD
    seq0 = ROWS_CTX // T_LAT
    cache_spec = pl.BlockSpec((None, None, PAST, LANES), lambda p, b: (b, o, 0, p))
    return pl.pallas_call(
        _attn_na_kernel,
        grid=(nhp, N_LAT),
        in_specs=[pl.BlockSpec((T_LAT, LANES), lambda p, b: (seq0 + b, p)),
                  pl.BlockSpec((T_LAT, LANES), lambda p, b: (seq0 + b, nhp + p)),
                  pl.BlockSpec((T_LAT, LANES), lambda p, b: (seq0 + b, 2 * nhp + p)),
                  cache_spec, cache_spec,
                  pl.BlockSpec((None, hpb, bias.shape[2], GRID_W, LANES), lambda p, b: (o, p, 0, 0, 0)),
                  _untouched()],
        out_specs=pl.BlockSpec((T_LAT, LANES), lambda p, b: (seq0 + b, p)),
        out_shape=jax.ShapeDtypeStruct((ROWS, C_W), BF16),
        input_output_aliases={6: 0},
        compiler_params=_cparams(("arbitrary", "arbitrary")),
        name="attn_na",
    )(z, z, z, ck, cv, bias, y_prev)


def _seg_sum(x, n_heads):
    parts = []
    for h in range(n_heads):
        s = jnp.sum(x[:, h * HD:(h + 1) * HD], axis=-1, keepdims=True)
        parts.append(jnp.broadcast_to(s, (x.shape[0], HD)))
    return jnp.concatenate(parts, axis=-1)


PREP_TM = 256


def _seq_len_at(row_start):
    return jnp.where(row_start < ROWS_CTX, T_CTX, T_LAT)


def _rwkv_prep_kernel(z_ref, zprev_ref, znext_ref, mu_ref, kkw_ref, w0_ref, w2_ref, a0_ref, a2_ref, ka_ref,
                      rk_ref, g2_ref, o_rk_ref, o_wk_ref, o_bv_ref, g_ref, bonus_ref):
    z = z_ref[:, A_IN:]
    t = z.shape[0]
    start = pl.program_id(0) * t
    seq_len = _seq_len_at(start)
    pos = (start - jnp.where(start < ROWS_CTX, 0, ROWS_CTX)) % seq_len
    halo_prev = jnp.where(pos == 0, 0.0, zprev_ref[SUBLANES - 1:SUBLANES, A_IN:])
    halo_next = jnp.where(pos + t == seq_len, 0.0, znext_ref[0:1, A_IN:])
    row = lax.broadcasted_iota(jnp.int32, (t, 1), 0)
    prev = jnp.where(row == 0, halo_prev, pltpu.roll(z, 1, 0))
    nxt = jnp.where(row == t - 1, halo_next, pltpu.roll(z, t - 1, 0))
    m = z + mu_ref[...] * (0.5 * (prev + nxt) - z)
    r = m[:, :B_W]
    k = m[:, B_W:2 * B_W]
    v = m[:, 2 * B_W:3 * B_W]
    o = 3 * B_W
    wd = m[:, o:o + 2 * LORA_W]
    ad = m[:, o + 2 * LORA_W:o + 2 * LORA_W + 2 * LORA_A]
    gd = m[:, o + 2 * LORA_W + 2 * LORA_A:]

    kkr = k * kkw_ref[...]
    kk = kkr * lax.rsqrt(_seg_sum(kkr * kkr, B_HEADS) + 1e-12)

    wl = w0_ref[...] + jnp.dot(jnp.tanh(wd), w2_ref[...], precision=HIGHEST, preferred_element_type=F32)
    decay = jnp.exp(-float(np.exp(-0.5)) * jax.nn.sigmoid(wl))
    a = jax.nn.sigmoid(a0_ref[...] + jnp.dot(ad, a2_ref[...], precision=HIGHEST, preferred_element_type=F32))
    k2 = jnp.concatenate([k, k], axis=-1)
    ka2 = jnp.concatenate([ka_ref[...], ka_ref[...]], axis=-1)
    kk2 = jnp.concatenate([kk, kk], axis=-1)

    kd = k2 * (1.0 + (a - 1.0) * ka2)
    bb = kk2 * a
    for h in range(B_HEADS):
        sl = slice(h * HD, (h + 1) * HD)
        o_rk_ref[pl.ds(h, t, stride=B_HEADS), :] = jnp.concatenate([r[:, sl], kk[:, sl]], axis=-1)
        for dr in range(2):
            sd = slice(dr * B_W + h * HD, dr * B_W + (h + 1) * HD)
            row = pl.ds(dr * B_HEADS + h, t, stride=2 * B_HEADS)
            o_wk_ref[row, :] = jnp.concatenate([decay[:, sd], kd[:, sd]], axis=-1)
            o_bv_ref[row, :] = jnp.concatenate([bb[:, sd], v[:, sl]], axis=-1)
    g_ref[...] = jnp.dot(jax.nn.sigmoid(gd).astype(BF16), g2_ref[...].astype(BF16), preferred_element_type=F32)
    bonus_ref[...] = _seg_sum(r * k * rk_ref[...], B_HEADS) * v


def rwkv_prep(z, params, e):
    tm = PREP_TM
    full = lambda a: pl.BlockSpec((None,) + a.shape[1:], lambda s: (e, 0, 0))
    out_rows = (B_HEADS, 2 * B_HEADS, 2 * B_HEADS)
    per = tm // SUBLANES
    last = ROWS // SUBLANES - 1
    rowspec = pl.BlockSpec((tm, B_W), lambda s: (s, 0))
    return pl.pallas_call(
        _rwkv_prep_kernel,
        grid=(ROWS // tm,),
        in_specs=[pl.BlockSpec((tm, EVEN_IN), lambda s: (s, 0)),
                  pl.BlockSpec((SUBLANES, EVEN_IN), lambda s: (jnp.maximum(s * per - 1, 0), 0)),
                  pl.BlockSpec((SUBLANES, EVEN_IN), lambda s: (jnp.minimum((s + 1) * per, last), 0))]
                 + [full(p) for p in params],
        out_specs=[pl.BlockSpec((tm * n, LANES), lambda s: (s, 0)) for n in out_rows] + [rowspec, rowspec],
        out_shape=[jax.ShapeDtypeStruct((ROWS * n, LANES), F32) for n in out_rows]
                  + [jax.ShapeDtypeStruct((ROWS, B_W), F32)] * 2,
        compiler_params=_cparams(("arbitrary",)),
        name="rwkv_prep",
    )(z, z, z, *params)


def rwkv_prep_params(prm):
    n_even = prm['b_mu'].shape[0]

    def block_diag(w):
        zero = jnp.zeros((n_even, LORA_W, B_W), F32)
        return jnp.concatenate([jnp.concatenate([w[:, 0], zero], 2), jnp.concatenate([zero, w[:, 1]], 2)], 1)

    row = lambda a: a.reshape(n_even, 1, -1)
    return (row(prm['b_mu']), row(prm['b_kk']), row(prm['b_w0']), block_diag(prm['b_w2']),
            row(prm['b_a0']), block_diag(prm['b_a2']), row(prm['b_ka']), row(prm['b_rk']), prm['b_g2'])


SCAN_CHAINS = 8


def _tree_sum(xs):
    while len(xs) > 1:
        xs = [xs[i] + xs[i + 1] for i in range(0, len(xs) - 1, 2)] + ([xs[-1]] if len(xs) % 2 else [])
    return xs[0]


def _scan_step(rk_b, wk_b, bv_b, vv, s_ref, ni):
    n_acc = max(1, SCAN_CHAINS // ni)

    def bcast(ref, row):
        return jnp.broadcast_to(ref[pl.ds(row, 1), :], (SUBLANES, LANES))

    sa = [[None] * n_acc for _ in range(ni)]
    for j in range(HD):
        kkj = bcast(rk_b, HD + j)
        for g in range(ni):
            p = s_ref[j, pl.ds(g * SUBLANES, SUBLANES), :] * kkj
            a = j % n_acc
            sa[g][a] = p if sa[g][a] is None else sa[g][a] + p
    sa = [-_tree_sum(x) for x in sa]
    ya = [[None] * n_acc for _ in range(ni)]
    for j in range(HD):
        wj = bcast(wk_b, j)
        kj = bcast(wk_b, HD + j)
        bj = bcast(bv_b, j)
        rj = bcast(rk_b, j)
        for g in range(ni):
            sl = pl.ds(g * SUBLANES, SUBLANES)
            s_new = s_ref[j, sl, :] * wj + sa[g] * bj + vv[g] * kj
            s_ref[j, sl, :] = s_new
            p = s_new * rj
            a = j % n_acc
            ya[g][a] = p if ya[g][a] is None else ya[g][a] + p
    return [_tree_sum(ya[g]) for g in range(ni)]


def _scan_pairs(tc, relayout, run):
    relayout(0, 0)

    def pair(k, carry):
        i0 = 2 * k
        relayout(i0 + 1, 1)
        run(i0, 0)
        relayout(jnp.minimum(i0 + 2, tc - 1), 0)
        run(i0 + 1, 1)
        return carry

    lax.fori_loop(0, tc // 2, pair, 0)


def _scan_ctx_kernel(rk_ref, wk_ref, bv_ref, y_ref, st_ref, rk0, wk0, bv0, rk1, wk1, bv1, s_ref, *, tc):
    d = pl.program_id(0)
    c = pl.program_id(1)
    ni = HD // SUBLANES
    bufs = ((rk0, wk0, bv0), (rk1, wk1, bv1))

    @pl.when(c == 0)
    def _():
        s_ref[...] = jnp.zeros_like(s_ref)

    def t_of(i):
        return i + d * (tc - 1 - 2 * i)

    def relayout(i, slot):
        t = t_of(i)
        for src, dst in zip((rk_ref, wk_ref, bv_ref), bufs[slot]):
            dst[...] = src[:, :, t].reshape(LANES, LANES).T

    def run(i, slot):
        rk_b, wk_b, bv_b = bufs[slot]
        vv = [bv_b[pl.ds(HD + g * SUBLANES, SUBLANES), :] for g in range(ni)]
        y = jnp.concatenate(_scan_step(rk_b, wk_b, bv_b, vv, s_ref, ni), axis=0)
        y_ref[:, :, t_of(i)] = y.T.reshape(CTX_GROUPS, ROW_SUB, B_HEADS, HD)

    _scan_pairs(tc, relayout, run)

    @pl.when(c == pl.num_programs(1) - 1)
    def _():
        st_ref[...] = s_ref[...]


def _rows5(x, heads):
    return x.reshape(ROW_GROUPS, ROW_SUB, T_CTX, heads, x.shape[-1])


def _operand_bufs():
    return [pltpu.VMEM((LANES, LANES), F32)] * 6


def rwkv_scan_ctx(prep, tc):
    rk, wk, bv = prep[:3]
    nc = T_CTX // tc

    def chunk(d, c):
        return c + d * (nc - 1 - 2 * c)

    blk = (CTX_GROUPS, ROW_SUB, tc, B_HEADS, LANES)
    shared = pl.BlockSpec(blk, lambda d, c: (0, 0, chunk(d, c), 0, 0))
    perdir = pl.BlockSpec(blk, lambda d, c: (0, 0, chunk(d, c), d, 0))
    yspec = pl.BlockSpec((CTX_GROUPS, ROW_SUB, tc, B_HEADS, HD), lambda d, c: (0, 0, chunk(d, c), d, 0))
    return pl.pallas_call(
        functools.partial(_scan_ctx_kernel, tc=tc),
        grid=(2, nc),
        in_specs=[shared, perdir, perdir],
        out_specs=[yspec, pl.BlockSpec((None, HD, HD, LANES), lambda d, c: (d, 0, 0, 0))],
        out_shape=[jax.ShapeDtypeStruct((ROW_GROUPS, ROW_SUB, T_CTX, 2 * B_HEADS, HD), F32),
                   jax.ShapeDtypeStruct((2, HD, HD, LANES), F32)],
        scratch_shapes=_operand_bufs() + [pltpu.VMEM((HD, HD, LANES), F32)],
        compiler_params=_cparams(("arbitrary", "arbitrary")),
        name="rwkv_scan_ctx",
    )(_rows5(rk, B_HEADS), _rows5(wk, 2 * B_HEADS), _rows5(bv, 2 * B_HEADS))


LAT_REP = LANES // (2 * N_LAT * B_HEADS)


def _scan_lat_kernel(rkf_ref, rkb_ref, wkf_ref, wkb_ref, bvf_ref, bvb_ref, s0_ref, yf_ref, yb_ref,
                     rk0, wk0, bv0, rk1, wk1, bv1, s_ref, y_s, *, tc):
    c = pl.program_id(0)
    n = N_LAT * B_HEADS
    nd = 2 * n
    ni = HD // (LAT_REP * SUBLANES)
    bufs = ((rk0, wk0, bv0), (rk1, wk1, bv1))

    @pl.when(c == 0)
    def _():
        s_ref[...] = s0_ref[...]

    def relayout(i, slot):
        srcs = ((rkf_ref, rkb_ref), (wkf_ref, wkb_ref), (bvf_ref, bvb_ref))
        for (fwd, bwd), dst in zip(srcs, bufs[slot]):
            m = jnp.concatenate([fwd[:, 0, i].reshape(n, LANES), bwd[:, 0, tc - 1 - i].reshape(n, LANES)], axis=0)
            dst[...] = jnp.concatenate([m] * LAT_REP, axis=0).T

    def run(i, slot):
        rk_b, wk_b, bv_b = bufs[slot]
        group = lax.broadcasted_iota(jnp.int32, (SUBLANES, LANES), 1) // nd
        vv = []
        for ig in range(ni):
            v = bv_b[pl.ds(HD + ig * SUBLANES, SUBLANES), :]
            for g in range(1, LAT_REP):
                v = jnp.where(group == g, bv_b[pl.ds(HD + (g * ni + ig) * SUBLANES, SUBLANES), :], v)
            vv.append(v)
        y_s[i] = jnp.concatenate(_scan_step(rk_b, wk_b, bv_b, vv, s_ref, ni), axis=0)

    _scan_pairs(tc, relayout, run)

    def store(i, carry):
        y = y_s[i]
        rows = [y] + [pltpu.roll(y, LANES - g * nd, 1) for g in range(1, LAT_REP)]
        yt = jnp.concatenate(rows, axis=0).T
        yf_ref[:, 0, i] = yt[:n].reshape(N_LAT, B_HEADS, HD)
        yb_ref[:, 0, tc - 1 - i] = yt[n:nd].reshape(N_LAT, B_HEADS, HD)
        return carry

    lax.fori_loop(0, tc, store, 0, unroll=8)


def rwkv_scan_lat(prep, s0f, s0b, tc):
    rk, wk, bv = prep[:3]
    nc = T_LAT // tc
    per_sub = T_CTX // tc
    g0 = CTX_GROUPS // N_LAT

    def fwd_map(head_block):
        return lambda c: (g0, c // per_sub, c % per_sub, head_block, 0)

    def bwd_map(head_block):
        return lambda c: (g0, (nc - 1 - c) // per_sub, (nc - 1 - c) % per_sub, head_block, 0)

    blk = (N_LAT, 1, tc, B_HEADS, LANES)
    yblk = (N_LAT, 1, tc, B_HEADS, HD)
    ni8 = HD // LAT_REP
    s0 = jnp.stack([s0f, s0b]).reshape(2, N_LAT, B_HEADS, LAT_REP, ni8, HD)
    s0 = s0.transpose(5, 4, 3, 0, 1, 2).reshape(HD, ni8, LANES)
    yshape = jax.ShapeDtypeStruct((N_LAT, ROW_SUB, T_CTX, B_HEADS, HD), F32)
    return pl.pallas_call(
        functools.partial(_scan_lat_kernel, tc=tc),
        grid=(nc,),
        in_specs=[pl.BlockSpec(blk, fwd_map(0)), pl.BlockSpec(blk, bwd_map(0)),
                  pl.BlockSpec(blk, fwd_map(0)), pl.BlockSpec(blk, bwd_map(1)),
                  pl.BlockSpec(blk, fwd_map(0)), pl.BlockSpec(blk, bwd_map(1)),
                  pl.BlockSpec((HD, ni8, LANES), lambda c: (0, 0, 0))],
        out_specs=[pl.BlockSpec(yblk, lambda c: (0, c // per_sub, c % per_sub, 0, 0)),
                   pl.BlockSpec(yblk, lambda c: (0, (nc - 1 - c) // per_sub, (nc - 1 - c) % per_sub, 0, 0))],
        out_shape=[yshape, yshape],
        scratch_shapes=_operand_bufs() + [pltpu.VMEM((HD, ni8, LANES), F32),
                                          pltpu.VMEM((tc, ni8, LANES), F32)],
        compiler_params=_cparams(("arbitrary",)),
        name="rwkv_scan_lat",
    )(_rows5(rk, B_HEADS), _rows5(rk, B_HEADS), _rows5(wk, 2 * B_HEADS), _rows5(wk, 2 * B_HEADS),
      _rows5(bv, 2 * B_HEADS), _rows5(bv, 2 * B_HEADS), s0)


def _rwkv_post_kernel(yc_ref, ylf_ref, ylb_ref, bonus_ref, g_ref, lng_ref, lnb_ref, o_ref, *, n_ctx_blocks):
    tm = o_ref.shape[0]

    def emit(load):
        outs = []
        for h in range(B_HEADS):
            sl = slice(h * HD, (h + 1) * HD)
            y = load(h)
            yc = y - jnp.mean(y, axis=-1, keepdims=True)
            var = jnp.mean(yc * yc, axis=-1, keepdims=True)
            yn = yc * lax.rsqrt(var + GN_EPS) * lng_ref[:, sl] + lnb_ref[:, sl]
            outs.append((yn + bonus_ref[:, sl]) * g_ref[:, sl])
        o_ref[...] = jnp.concatenate(outs, axis=-1).astype(BF16)

    @pl.when(pl.program_id(0) < n_ctx_blocks)
    def _():
        emit(lambda h: yc_ref[pl.ds(h, tm, stride=2 * B_HEADS), :]
             + yc_ref[pl.ds(B_HEADS + h, tm, stride=2 * B_HEADS), :])

    @pl.when(pl.program_id(0) >= n_ctx_blocks)
    def _():
        emit(lambda h: ylf_ref[pl.ds(h, tm, stride=B_HEADS), :] + ylb_ref[pl.ds(h, tm, stride=B_HEADS), :])


def rwkv_post(y_ctx, y_lat_f, y_lat_b, bonus, g, lng, lnb, e):
    tm = 256
    nb = ROWS_CTX // tm
    rowspec = pl.BlockSpec((tm, B_W), lambda i: (i, 0))
    vecspec = pl.BlockSpec((None, 1, B_W), lambda i: (e, 0, 0))
    latspec = pl.BlockSpec((tm * B_HEADS, HD), lambda i: (jnp.maximum(i - nb, 0), 0))
    return pl.pallas_call(
        functools.partial(_rwkv_post_kernel, n_ctx_blocks=nb),
        grid=(ROWS // tm,),
        in_specs=[pl.BlockSpec((tm * 2 * B_HEADS, HD), lambda i: (jnp.minimum(i, nb - 1), 0)), latspec, latspec,
                  rowspec, rowspec, vecspec, vecspec],
        out_specs=rowspec,
        out_shape=jax.ShapeDtypeStruct((ROWS, B_W), BF16),
        compiler_params=_cparams(("arbitrary",)),
        name="rwkv_post",
    )(y_ctx, y_lat_f, y_lat_b, bonus, g, lng, lnb)


def kernel(x_prompt, x_sample, cache_a_k, cache_a_v, state_b_fwd, state_b_bwd, cache_c_k, cache_c_v, c, c_ctx,
           ada_w, ada_b, norm1_g, norm2_g, final_norm_g, w_in_e, w_out_e, a_q_gain, a_k_gain, b_mu, b_w0, b_w2,
           b_a0, b_a2, b_g2, b_kk, b_ka, b_rk, b_ln_g, b_ln_b, ffn_w_gu, ffn_w_dn, w_in_o, w_out_o, c_rpb,
           router_w, router_b, moe_w_gu, moe_w_dn):
    n_even, n_odd = w_in_e.shape[0], w_in_o.shape[0]
    prep_params = rwkv_prep_params(dict(b_mu=b_mu, b_w0=b_w0, b_w2=b_w2, b_a0=b_a0, b_a2=b_a2, b_g2=b_g2,
                                        b_kk=b_kk, b_ka=b_ka, b_rk=b_rk))
    cond = jnp.zeros((N_MODS, D), F32).at[0].set(c_ctx).at[1:1 + N_LAT].set(c)
    mods = ada_all(cond, ada_w, ada_b).reshape(DEPTH, N_MODS, 6, D)
    g1 = norm1_g.reshape(DEPTH, 1, D)
    g2 = norm2_g.reshape(DEPTH, 1, D)
    gq = a_q_gain.reshape(n_even, 1, HD)
    gk = a_k_gain.reshape(n_even, 1, HD)
    lng = b_ln_g.reshape(n_even, 1, B_W)
    lnb = b_ln_b.reshape(n_even, 1, B_W)
    ck_a = cache_a_k.reshape(N_LAT, n_even, PAST, A_KV_W)
    cv_a = cache_a_v.reshape(N_LAT, n_even, PAST, A_KV_W)
    ck_c = cache_c_k.reshape(N_LAT, n_odd, PAST, C_W)
    cv_c = cache_c_v.reshape(N_LAT, n_odd, PAST, C_W)
    na_bias = _na_bias_table(c_rpb)
    rw_p = jnp.zeros((n_odd, D, LANES), F32).at[:, :, :N_EXPERTS].set(router_w)
    rb_p = jnp.zeros((n_odd, 1, LANES), F32).at[:, 0, :N_EXPERTS].set(router_b)
    moe_gu_b = moe_w_gu.astype(BF16)
    moe_dn_b = moe_w_dn.astype(BF16)

    x, h = first_norm(x_prompt, x_sample, g1, mods)

    a_caches, c_caches = None, None
    new_sf, new_sb = [], []
    y_final = None
    for l in range(DEPTH):
        if l % 2 == 0:
            e = l // 2
            z = mm_in(h, w_in_e, e, tn=EVEN_IN // 3)
            y_a, k_new, v_new = attn_a_ctx(z, gq, gk, e, a_caches)
            a_caches = (k_new, v_new)
            y_a = attn_a_lat(z, ck_a, cv_a, gq, gk, e, y_a)
            prep = rwkv_prep(z, prep_params, e)
            y_s, st = rwkv_scan_ctx(prep, tc=32)
            y_lf, y_lb = rwkv_scan_lat(prep, state_b_fwd[:, e], state_b_bwd[:, e], tc=64)
            st = st.reshape(2, HD, HD, N_CTX, B_HEADS).transpose(0, 3, 4, 2, 1)
            new_sf.append(st[0])
            new_sb.append(st[1])
            y_b = rwkv_post(y_s.reshape(ROWS * 2 * B_HEADS, HD), y_lf.reshape(ROWS_LAT * B_HEADS, HD),
                            y_lb.reshape(ROWS_LAT * B_HEADS, HD), prep[4], prep[3], lng, lnb, e)
            x, h = mm_out([y_a, y_b], w_out_e, e, x, mods, l, g2, l)
            x, h = ffn(h, ffn_w_gu, ffn_w_dn, e, x, mods, l, g1)
        else:
            o = l // 2
            z = mm_in(h, w_in_o, o, tn=C_W)
            y, k_new, v_new = attn_c_ctx(z, o, n_odd, c_caches)
            c_caches = (k_new, v_new)
            y = attn_na(z, ck_c, cv_c, na_bias, o, y)
            x, h = mm_out([y], w_out_o, o, x, mods, l, g2, l)
            routing = router(x, g2, mods, l, rw_p, rb_p, o)
            y_moe = moe_routed(h, routing, moe_gu_b, moe_dn_b, o)
            if l + 1 < DEPTH:
                x, h = resid(x, y_moe, mods, l, g1, False)
            else:
                y_final = resid(x, y_moe, mods, l, final_norm_g[None], True)

    y_prompt = y_final[0].reshape(N_CTX, T_CTX, D)
    y_sample = y_final[1].reshape(N_LAT, T_LAT, D)
    return (y_prompt, y_sample,
            a_caches[0].reshape(N_CTX, n_even, T_CTX, A_KV_HEADS, HD),
            a_caches[1].reshape(N_CTX, n_even, T_CTX, A_KV_HEADS, HD),
            jnp.stack(new_sf, axis=1), jnp.stack(new_sb, axis=1),
            c_caches[0].reshape(N_CTX, n_odd, T_CTX, C_HEADS, HD),
            c_caches[1].reshape(N_CTX, n_odd, T_CTX, C_HEADS, HD))
```

```python
import functools

import numpy as np
import jax
import jax.numpy as jnp
from jax import lax
from jax.experimental import pallas as pl
from jax.experimental.pallas import tpu as pltpu

F32 = jnp.float32
BF16 = jnp.bfloat16

D = 1024
N_CTX, T_CTX = 16, 256
N_LAT, T_LAT = 2, 1024
ROWS_CTX = N_CTX * T_CTX
ROWS_LAT = N_LAT * T_LAT
ROWS = ROWS_CTX + ROWS_LAT
DEPTH = 4
GRID_W = 64
HD = 64
A_HEADS, A_KV_HEADS, B_HEADS, C_HEADS = 8, 2, 8, 16
A_W, A_KV_W, B_W, C_W = A_HEADS * HD, A_KV_HEADS * HD, B_HEADS * HD, C_HEADS * HD
A_IN = A_W + 2 * A_KV_W
LORA_W, LORA_A, LORA_G = 64, 64, 128
B_IN = 3 * B_W + 2 * LORA_W + 2 * LORA_A + LORA_G
EVEN_IN = A_IN + B_IN
PAST = 512
NA_ROWS, NA_COLS = 8, 16
D_FF = 2816
N_EXPERTS = 8
D_EXPERT = 1408
ROPE_THETA = 10000.0
EPS = 1e-6
GN_EPS = 64e-5
NEG_INF = -1e30
ATT_SCALE = HD ** -0.5

LANES = 128
SUBLANES = 8
VMEM_LIMIT = 56 * 1024 * 1024

N_MODS = 8

ROW_SUB = T_LAT // T_CTX
ROW_GROUPS = ROWS // (ROW_SUB * T_CTX)
CTX_GROUPS = N_CTX // ROW_SUB


def _cparams(sem):
    return pltpu.CompilerParams(dimension_semantics=sem, vmem_limit_bytes=VMEM_LIMIT)


def _untouched():
    return pl.BlockSpec(memory_space=pl.ANY)


def _mod_index(row_start):
    return jnp.where(row_start < ROWS_CTX, 0, 1 + (row_start - ROWS_CTX) // T_LAT)


def _modspec(layer, tm, nargs=1):
    if nargs == 1:
        return pl.BlockSpec((None, 1, 6, D), lambda i: (layer, _mod_index(i * tm), 0, 0))
    return pl.BlockSpec((None, 1, 6, D), lambda i, j: (layer, _mod_index(i * tm), 0, 0))


def _gainspec(layer, nargs=1):
    if nargs == 1:
        return pl.BlockSpec((None, 1, D), lambda i: (layer, 0, 0))
    return pl.BlockSpec((None, 1, D), lambda i, j: (layer, 0, 0))


def _modnorm(x, g, shift, scale):
    ms = jnp.mean(x * x, axis=-1, keepdims=True)
    return (x * lax.rsqrt(ms + EPS) * g) * (1.0 + scale) + shift


def _silu(x):
    return x * jax.nn.sigmoid(x)


def _dot3(a, b):
    ah = a.astype(BF16)
    al = (a - ah.astype(F32)).astype(BF16)
    bh = b.astype(BF16)
    bl = (b - bh.astype(F32)).astype(BF16)
    dot = functools.partial(jnp.dot, preferred_element_type=F32)
    return dot(ah, bh) + dot(ah, bl) + dot(al, bh)


def _ada_kernel(c_ref, w_ref, b_ref, o_ref):
    s = _silu(c_ref[...]).astype(BF16)
    o_ref[0] = jnp.dot(s, w_ref[0].astype(BF16), preferred_element_type=F32) + b_ref[0]


def ada_all(cond, ada_w, ada_b):
    tn = 1536
    n = 6 * D
    return pl.pallas_call(
        _ada_kernel,
        grid=(DEPTH, n // tn),
        in_specs=[pl.BlockSpec((N_MODS, D), lambda l, j: (0, 0)),
                  pl.BlockSpec((1, D, tn), lambda l, j: (l, 0, j)),
                  pl.BlockSpec((1, 1, tn), lambda l, j: (l, 0, j))],
        out_specs=pl.BlockSpec((1, N_MODS, tn), lambda l, j: (l, 0, j)),
        out_shape=jax.ShapeDtypeStruct((DEPTH, N_MODS, n), F32),
        compiler_params=_cparams(("arbitrary", "arbitrary")),
        name="ada",
    )(cond, ada_w, ada_b.reshape(DEPTH, 1, n))


def _first_norm_kernel(xp_ref, xs_ref, g_ref, mod_ref, x_ref, h_ref, *, n_ctx_blocks):
    i = pl.program_id(0)

    def emit(src):
        x = src[...]
        x_ref[...] = x
        h_ref[...] = _modnorm(x, g_ref[...], mod_ref[0, 0:1, :], mod_ref[0, 1:2, :]).astype(BF16)

    @pl.when(i < n_ctx_blocks)
    def _():
        emit(xp_ref)

    @pl.when(i >= n_ctx_blocks)
    def _():
        emit(xs_ref)


def first_norm(x_prompt, x_sample, gains, mods):
    tm = 512
    nc = ROWS_CTX // tm
    rowspec = pl.BlockSpec((tm, D), lambda i: (i, 0))
    return pl.pallas_call(
        functools.partial(_first_norm_kernel, n_ctx_blocks=nc),
        grid=(ROWS // tm,),
        in_specs=[pl.BlockSpec((tm, D), lambda i: (jnp.minimum(i, nc - 1), 0)),
                  pl.BlockSpec((tm, D), lambda i: (jnp.maximum(i - nc, 0), 0)),
                  _gainspec(0), _modspec(0, tm)],
        out_specs=[rowspec, rowspec],
        out_shape=[jax.ShapeDtypeStruct((ROWS, D), F32), jax.ShapeDtypeStruct((ROWS, D), BF16)],
        compiler_params=_cparams(("arbitrary",)),
        name="first_norm",
    )(x_prompt.reshape(ROWS_CTX, D), x_sample.reshape(ROWS_LAT, D), gains, mods)


def _mm_in_kernel(h_ref, w_ref, o_ref, wb_ref):
    @pl.when(pl.program_id(1) == 0)
    def _():
        wb_ref[...] = w_ref[...].astype(BF16)

    o_ref[...] = jnp.dot(h_ref[...], wb_ref[...], preferred_element_type=F32)


def mm_in(h, w, layer, tn, tm=1024):
    _, k, n = w.shape
    return pl.pallas_call(
        _mm_in_kernel,
        grid=(n // tn, ROWS // tm),
        in_specs=[pl.BlockSpec((tm, k), lambda j, i: (i, 0)),
                  pl.BlockSpec((None, k, tn), lambda j, i: (layer, 0, j))],
        out_specs=pl.BlockSpec((tm, tn), lambda j, i: (i, j)),
        out_shape=jax.ShapeDtypeStruct((ROWS, n), F32),
        scratch_shapes=[pltpu.VMEM((k, tn), BF16)],
        compiler_params=_cparams(("arbitrary", "arbitrary")),
        name="mm_in",
    )(h, w)


def _residual_epilogue(x, acc, mod_ref, nmod_ref, g_ref, xo_ref, ho_ref, gate_idx, shift_idx, scale_idx, final):
    gate = mod_ref[0, gate_idx:gate_idx + 1, :]
    xn = x + gate * acc
    if final:
        ms = jnp.mean(xn * xn, axis=-1, keepdims=True)
        ho_ref[...] = xn * lax.rsqrt(ms + EPS) * g_ref[...]
    else:
        xo_ref[...] = xn
        shift = nmod_ref[0, shift_idx:shift_idx + 1, :]
        scale = nmod_ref[0, scale_idx:scale_idx + 1, :]
        ho_ref[...] = _modnorm(xn, g_ref[...], shift, scale).astype(BF16)


def _mm_out_kernel(*refs, n_parts, gate_idx, shift_idx, scale_idx):
    y_refs = refs[:n_parts]
    w_refs = refs[n_parts:2 * n_parts]
    x_ref, mod_ref, nmod_ref, g_ref, xo_ref, ho_ref = refs[2 * n_parts:2 * n_parts + 6]
    wb_refs = refs[2 * n_parts + 6:]

    @pl.when(pl.program_id(0) == 0)
    def _():
        for w_ref, wb_ref in zip(w_refs, wb_refs):
            wb_ref[...] = w_ref[...].astype(BF16)

    acc = None
    for y_ref, wb_ref in zip(y_refs, wb_refs):
        p = jnp.dot(y_ref[...], wb_ref[...], preferred_element_type=F32)
        acc = p if acc is None else acc + p
    _residual_epilogue(x_ref[...], acc, mod_ref, nmod_ref, g_ref, xo_ref, ho_ref,
                       gate_idx, shift_idx, scale_idx, False)


def mm_out(parts, w, layer_w, x, mods, layer, gains, gain_layer):
    tm = 1024
    kp = parts[0].shape[1]
    n_parts = len(parts)
    rowspec = pl.BlockSpec((tm, D), lambda i: (i, 0))
    return pl.pallas_call(
        functools.partial(_mm_out_kernel, n_parts=n_parts, gate_idx=2, shift_idx=3, scale_idx=4),
        grid=(ROWS // tm,),
        in_specs=[pl.BlockSpec((tm, kp), lambda i: (i, 0)) for _ in parts]
                 + [pl.BlockSpec((None, kp, D), lambda i, p=p: (layer_w, p, 0)) for p in range(n_parts)]
                 + [rowspec, _modspec(layer, tm), _modspec(layer, tm), _gainspec(gain_layer)],
        out_specs=[rowspec, rowspec],
        out_shape=[jax.ShapeDtypeStruct((ROWS, D), F32), jax.ShapeDtypeStruct((ROWS, D), BF16)],
        scratch_shapes=[pltpu.VMEM((kp, D), BF16) for _ in parts],
        compiler_params=_cparams(("arbitrary",)),
        name="mm_out",
    )(*parts, *([w] * n_parts), x, mods, mods, gains)


FFN_TF = 256
FFN_NF = D_FF // FFN_TF


def _ffn_kernel(h_ref, wg_ref, wu_ref, wd_ref, x_ref, mod_ref, nmod_ref, g_ref, xo_ref, ho_ref,
                wg_s, wu_s, wd_s, acc_ref):
    i = pl.program_id(0)
    f = pl.program_id(1)

    @pl.when(i == 0)
    def _():
        wg_s[f] = wg_ref[...].astype(BF16)
        wu_s[f] = wu_ref[...].astype(BF16)
        wd_s[f] = wd_ref[...].astype(BF16)

    @pl.when(f == 0)
    def _():
        acc_ref[...] = jnp.zeros_like(acc_ref)

    h = h_ref[...]
    gp = jnp.dot(h, wg_s[f], preferred_element_type=F32)
    up = jnp.dot(h, wu_s[f], preferred_element_type=F32)
    a = (_silu(gp) * up).astype(BF16)
    acc_ref[...] += jnp.dot(a, wd_s[f], preferred_element_type=F32)

    @pl.when(f == FFN_NF - 1)
    def _():
        _residual_epilogue(x_ref[...], acc_ref[...], mod_ref, nmod_ref, g_ref, xo_ref, ho_ref, 5, 0, 1, False)


def ffn(h, w_gu, w_dn, e, x, mods, layer, gains):
    tm = 1024
    tf, nf = FFN_TF, FFN_NF

    def once(i, f):
        return jnp.where(i == 0, f, nf - 1)

    rowspec = pl.BlockSpec((tm, D), lambda i, f: (i, 0))
    return pl.pallas_call(
        _ffn_kernel,
        grid=(ROWS // tm, nf),
        in_specs=[rowspec,
                  pl.BlockSpec((None, D, tf), lambda i, f: (e, 0, once(i, f))),
                  pl.BlockSpec((None, D, tf), lambda i, f: (e, 0, nf + once(i, f))),
                  pl.BlockSpec((None, tf, D), lambda i, f: (e, once(i, f), 0)),
                  rowspec, _modspec(layer, tm, 2), _modspec(layer + 1, tm, 2), _gainspec(layer + 1, 2)],
        out_specs=[rowspec, rowspec],
        out_shape=[jax.ShapeDtypeStruct((ROWS, D), F32), jax.ShapeDtypeStruct((ROWS, D), BF16)],
        scratch_shapes=[pltpu.VMEM((nf, D, tf), BF16),
                        pltpu.VMEM((nf, D, tf), BF16),
                        pltpu.VMEM((nf, tf, D), BF16),
                        pltpu.VMEM((tm, D), F32)],
        compiler_params=_cparams(("arbitrary", "arbitrary")),
        name="ffn",
    )(h, w_gu, w_gu, w_dn, x, mods, mods, gains)


MOE_TILE = 1024
MOE_CHUNK = 320


def _router_kernel(x_ref, g_ref, mod_ref, rw_ref, rb_ref, gates_ref, pos_ref, gates_t_ref, pos_t_ref, carry_ref):
    h = _modnorm(x_ref[...], g_ref[...], mod_ref[0, 3:4, :], mod_ref[0, 4:5, :])
    logits = _dot3(h, rw_ref[...])
    lane = lax.broadcasted_iota(jnp.int32, logits.shape, 1)
    sel = jnp.where(lane < N_EXPERTS, logits + rb_ref[...], -jnp.inf)
    m1 = jnp.max(sel, axis=-1, keepdims=True)
    i1 = jnp.min(jnp.where(sel == m1, lane, LANES), axis=-1, keepdims=True)
    sel2 = jnp.where(lane == i1, -jnp.inf, sel)
    m2 = jnp.max(sel2, axis=-1, keepdims=True)
    i2 = jnp.min(jnp.where(sel2 == m2, lane, LANES), axis=-1, keepdims=True)
    l1 = jnp.sum(jnp.where(lane == i1, logits, 0.0), axis=-1, keepdims=True)
    l2 = jnp.sum(jnp.where(lane == i2, logits, 0.0), axis=-1, keepdims=True)
    mx = jnp.maximum(l1, l2)
    e1 = jnp.exp(l1 - mx)
    e2 = jnp.exp(l2 - mx)
    den = e1 + e2
    gates = jnp.where(lane == i1, e1 / den, 0.0) + jnp.where(lane == i2, e2 / den, 0.0)
    gates_ref[...] = gates

    tm = gates.shape[0]
    routed = gates > 0.0
    tri = (lax.broadcasted_iota(jnp.int32, (tm, tm), 0) > lax.broadcasted_iota(jnp.int32, (tm, tm), 1))
    local = jnp.dot(jnp.where(tri, 1.0, 0.0).astype(BF16), jnp.where(routed, 1.0, 0.0).astype(BF16),
                    preferred_element_type=F32)

    @pl.when(pl.program_id(0) % (MOE_TILE // tm) == 0)
    def _():
        carry_ref[...] = jnp.zeros_like(carry_ref)

    pos = local + carry_ref[...]
    carry_ref[...] += jnp.sum(jnp.where(routed, 1.0, 0.0), axis=0, keepdims=True)
    pos_ref[...] = pos
    gates_t_ref[...] = gates.T[:SUBLANES]
    pos_t_ref[...] = pos.T[:SUBLANES]


def router(x, gains, mods, layer, rw_p, rb_p, o):
    tm = 512
    rowspec = pl.BlockSpec((tm, LANES), lambda i: (i, 0))
    colspec = pl.BlockSpec((SUBLANES, tm), lambda i: (0, i))
    return pl.pallas_call(
        _router_kernel,
        grid=(ROWS // tm,),
        in_specs=[pl.BlockSpec((tm, D), lambda i: (i, 0)),
                  _gainspec(layer), _modspec(layer, tm),
                  pl.BlockSpec((None, D, LANES), lambda i: (o, 0, 0)),
                  pl.BlockSpec((None, 1, LANES), lambda i: (o, 0, 0))],
        out_specs=[rowspec, rowspec, colspec, colspec],
        out_shape=[jax.ShapeDtypeStruct((ROWS, LANES), F32), jax.ShapeDtypeStruct((ROWS, LANES), F32),
                   jax.ShapeDtypeStruct((SUBLANES, ROWS), F32), jax.ShapeDtypeStruct((SUBLANES, ROWS), F32)],
        scratch_shapes=[pltpu.VMEM((1, LANES), F32)],
        compiler_params=_cparams(("arbitrary",)),
        name="router",
    )(x, gains, mods, rw_p, rb_p)


def _moe_kernel(cnt_ref, h_ref, gates_ref, pos_ref, gates_t_ref, pos_t_ref, wgu_ref, wdn_ref, o_ref, acc_ref):
    tile = pl.program_id(0)
    e = pl.program_id(1)
    tt = h_ref.shape[0]

    @pl.when(e == 0)
    def _():
        acc_ref[...] = jnp.zeros_like(acc_ref)

    lane = lax.broadcasted_iota(jnp.int32, (tt, LANES), 1)
    g_col = jnp.sum(jnp.where(lane == e, gates_ref[...], 0.0), axis=1, keepdims=True)
    p_col = jnp.sum(jnp.where(lane == e, pos_ref[...], 0.0), axis=1, keepdims=True)
    sub = lax.broadcasted_iota(jnp.int32, (SUBLANES, tt), 0)
    g_row = jnp.sum(jnp.where(sub == e, gates_t_ref[...], 0.0), axis=0, keepdims=True)
    p_row = jnp.sum(jnp.where(sub == e, pos_t_ref[...], 0.0), axis=0, keepdims=True)
    p_col = jnp.where(g_col > 0.0, p_col, -1.0)
    p_row = jnp.where(g_row > 0.0, p_row, -1.0)

    def chunk(ci, carry):
        base = (ci * MOE_CHUNK).astype(F32)
        take = p_row == lax.broadcasted_iota(jnp.int32, (MOE_CHUNK, tt), 0).astype(F32) + base
        xs = jnp.dot(jnp.where(take, 1.0, 0.0).astype(BF16), h_ref[...], preferred_element_type=F32)
        gu = jnp.dot(xs.astype(BF16), wgu_ref[...], preferred_element_type=F32)
        a = (_silu(gu[:, :D_EXPERT]) * gu[:, D_EXPERT:]).astype(BF16)
        y = jnp.dot(a, wdn_ref[...], preferred_element_type=F32)
        gate = jnp.sum(jnp.where(take, g_row, 0.0), axis=1, keepdims=True)
        put = p_col == lax.broadcasted_iota(jnp.int32, (tt, MOE_CHUNK), 1).astype(F32) + base
        acc_ref[...] += jnp.dot(jnp.where(put, 1.0, 0.0).astype(BF16), (gate * y).astype(BF16),
                                preferred_element_type=F32)
        return carry

    count = cnt_ref[tile * N_EXPERTS + e]
    lax.fori_loop(0, (count + MOE_CHUNK - 1) // MOE_CHUNK, chunk, 0)

    @pl.when(e == N_EXPERTS - 1)
    def _():
        o_ref[...] = acc_ref[...]


def moe_routed(h, routing, w_gu, w_dn, o):
    gates, pos, gates_t, pos_t = routing
    tt = MOE_TILE
    n_tiles = ROWS // tt
    counts = jnp.sum((gates[:, :N_EXPERTS] > 0.0).reshape(n_tiles, tt, N_EXPERTS), axis=1)
    counts = counts.astype(jnp.int32).reshape(n_tiles * N_EXPERTS)
    rowspec = pl.BlockSpec((tt, LANES), lambda t, e, c: (t, 0))
    colspec = pl.BlockSpec((SUBLANES, tt), lambda t, e, c: (0, t))
    grid_spec = pltpu.PrefetchScalarGridSpec(
        num_scalar_prefetch=1,
        grid=(n_tiles, N_EXPERTS),
        in_specs=[pl.BlockSpec((tt, D), lambda t, e, c: (t, 0)),
                  rowspec, rowspec, colspec, colspec,
                  pl.BlockSpec((None, None, D, 2 * D_EXPERT), lambda t, e, c: (o, e, 0, 0)),
                  pl.BlockSpec((None, None, D_EXPERT, D), lambda t, e, c: (o, e, 0, 0))],
        out_specs=pl.BlockSpec((tt, D), lambda t, e, c: (t, 0)),
        scratch_shapes=[pltpu.VMEM((tt, D), F32)])
    return pl.pallas_call(
        _moe_kernel,
        grid_spec=grid_spec,
        out_shape=jax.ShapeDtypeStruct((ROWS, D), F32),
        compiler_params=_cparams(("arbitrary", "arbitrary")),
        name="moe",
    )(counts, h, gates, pos, gates_t, pos_t, w_gu, w_dn)


def _resid_kernel(x_ref, y_ref, mod_ref, nmod_ref, g_ref, *out_refs, final):
    if not final:
        xo_ref, ho_ref = out_refs
        _residual_epilogue(x_ref[...], y_ref[...], mod_ref, nmod_ref, g_ref, xo_ref, ho_ref, 5, 0, 1, False)
        return
    n_ctx_blocks = ROWS_CTX // x_ref.shape[0]
    for ho_ref, mine in zip(out_refs, (pl.program_id(0) < n_ctx_blocks, pl.program_id(0) >= n_ctx_blocks)):
        @pl.when(mine)
        def _(ho_ref=ho_ref):
            _residual_epilogue(x_ref[...], y_ref[...], mod_ref, nmod_ref, g_ref, None, ho_ref, 5, 0, 1, True)


def resid(x, y, mods, layer, gains, final):
    tm = 512
    rowspec = pl.BlockSpec((tm, D), lambda i: (i, 0))
    if final:
        nb = ROWS_CTX // tm
        out_specs = [pl.BlockSpec((tm, D), lambda i: (jnp.minimum(i, nb - 1), 0)),
                     pl.BlockSpec((tm, D), lambda i: (jnp.maximum(i - nb, 0), 0))]
        out_shape = [jax.ShapeDtypeStruct((ROWS_CTX, D), F32), jax.ShapeDtypeStruct((ROWS_LAT, D), F32)]
        nmod, gain = _modspec(layer, tm), pl.BlockSpec((1, D), lambda i: (0, 0))
    else:
        out_specs = [rowspec, rowspec]
        out_shape = [jax.ShapeDtypeStruct((ROWS, D), F32), jax.ShapeDtypeStruct((ROWS, D), BF16)]
        nmod, gain = _modspec(layer + 1, tm), _gainspec(layer + 1)
    return pl.pallas_call(
        functools.partial(_resid_kernel, final=final),
        grid=(ROWS // tm,),
        in_specs=[rowspec, rowspec, _modspec(layer, tm), nmod, gain],
        out_specs=out_specs,
        out_shape=out_shape,
        compiler_params=_cparams(("arbitrary",)),
        name="resid",
    )(x, y, mods, mods, gains)


def _softmax_pv(scores, values):
    m = None
    for s in scores:
        mi = jnp.max(s, axis=-1, keepdims=True)
        m = mi if m is None else jnp.maximum(m, mi)
    num, den = None, None
    for s, v in zip(scores, values):
        p = jnp.exp(s - m)
        li = jnp.sum(p, axis=-1, keepdims=True)
        oi = jnp.dot(p.astype(BF16), v, preferred_element_type=F32)
        num = oi if num is None else num + oi
        den = li if den is None else den + li
    return num / den


def _qk(q, k):
    return lax.dot_general(q, k, (((1,), (1,)), ((), ())), preferred_element_type=F32)


def _head_rmsnorm(x, gain, n_heads):
    parts = []
    for h in range(n_heads):
        xh = x[:, h * HD:(h + 1) * HD]
        ms = jnp.mean(xh * xh, axis=-1, keepdims=True)
        parts.append(xh * lax.rsqrt(ms + EPS) * gain)
    return parts


def _rope128(x, cos, sin_signed):
    lane = lax.broadcasted_iota(jnp.int32, x.shape, 1)
    up = pltpu.roll(x, LANES - 16, 1)
    dn = pltpu.roll(x, 16, 1)
    partner = jnp.where((lane % 32) < 16, up, dn)
    return x * cos + partner * sin_signed


def _rope_tables():
    t = np.arange(T_LAT)
    n_f = HD // 4
    inv = ROPE_THETA ** (-np.arange(n_f, dtype=np.float32) / n_f)
    cos = np.zeros((T_LAT, HD), np.float32)
    sin = np.zeros((T_LAT, HD), np.float32)
    for half, pos in ((0, t // GRID_W), (1, t % GRID_W)):
        ang = pos[:, None].astype(np.float32) * inv[None, :]
        c, s = np.cos(ang), np.sin(ang)
        base = half * (HD // 2)
        cos[:, base:base + n_f] = c
        cos[:, base + n_f:base + 2 * n_f] = c
        sin[:, base:base + n_f] = -s
        sin[:, base + n_f:base + 2 * n_f] = s
    return np.tile(cos, (1, 2)), np.tile(sin, (1, 2))


def _attn_a_ctx_kernel(q_ref, kv_ref, qg_ref, kg_ref, *rest):
    o_ref, ko_ref, vo_ref = rest[-3:]
    q = q_ref[...]
    kv = kv_ref[...]
    k = kv[:, :A_KV_W]
    v = kv[:, A_KV_W:]
    kn = _head_rmsnorm(k, kg_ref[...], A_KV_HEADS)
    ko_ref[...] = jnp.concatenate(kn, axis=-1)
    vo_ref[...] = v
    qn = _head_rmsnorm(q, qg_ref[...], A_HEADS)
    group = A_HEADS // A_KV_HEADS
    t = q.shape[0]
    outs = []
    for g in range(A_KV_HEADS):
        qs = jnp.concatenate([(qn[g * group + i] * ATT_SCALE).astype(BF16) for i in range(group)], axis=0)
        o = _softmax_pv([_qk(qs, kn[g].astype(BF16))], [v[:, g * HD:(g + 1) * HD].astype(BF16)])
        outs += [o[i * t:(i + 1) * t] for i in range(group)]
    o_ref[...] = jnp.concatenate(outs, axis=-1).astype(BF16)


def attn_a_ctx(z, gq, gk, e, prev_caches):
    t = T_CTX
    n_even = gq.shape[0]
    cache_spec = pl.BlockSpec((None, None, t, A_KV_W), lambda b: (b, e, 0, 0))
    cache_shape = jax.ShapeDtypeStruct((N_CTX, n_even, t, A_KV_W), F32)
    extra = [] if prev_caches is None else list(prev_caches)
    aliases = {} if prev_caches is None else {4: 1, 5: 2}
    return pl.pallas_call(
        _attn_a_ctx_kernel,
        grid=(N_CTX,),
        in_specs=[pl.BlockSpec((t, A_W), lambda b: (b, 0)),
                  pl.BlockSpec((t, 2 * A_KV_W), lambda b: (b, A_W // (2 * A_KV_W))),
                  pl.BlockSpec((None, 1, HD), lambda b: (e, 0, 0)),
                  pl.BlockSpec((None, 1, HD), lambda b: (e, 0, 0))] + [_untouched() for _ in extra],
        out_specs=[pl.BlockSpec((t, A_W), lambda b: (b, 0)), cache_spec, cache_spec],
        out_shape=[jax.ShapeDtypeStruct((ROWS, A_W), BF16), cache_shape, cache_shape],
        input_output_aliases=aliases,
        compiler_params=_cparams(("arbitrary",)),
        name="attn_a_ctx",
    )(z, z, gq, gk, *extra)


A_LAT_TQ = 256


def _attn_a_lat_kernel(q_ref, kv_ref, ck_ref, cv_ref, qg_ref, kg_ref, cq_ref, sq_ref, ckk_ref, skk_ref,
                       prev_ref, o_ref, k_s, v_s):
    del prev_ref

    @pl.when(pl.program_id(1) == 0)
    def _():
        kv = kv_ref[...]
        kn = jnp.concatenate(_head_rmsnorm(kv[:, :A_KV_W], kg_ref[...], A_KV_HEADS), axis=-1)
        k_s[...] = _rope128(kn, ckk_ref[...], skk_ref[...]).astype(BF16)
        v_s[...] = kv[:, A_KV_W:].astype(BF16)

    qn = _head_rmsnorm(q_ref[...], qg_ref[...], A_HEADS)
    cos, sin = cq_ref[...], sq_ref[...]
    qr = []
    for c in range(A_HEADS // 2):
        slab = _rope128(jnp.concatenate(qn[2 * c:2 * c + 2], axis=-1), cos, sin)
        qr.append(slab[:, :HD])
        qr.append(slab[:, HD:])
    ck = ck_ref[...].astype(BF16)
    cv = cv_ref[...].astype(BF16)
    kk = k_s[...]
    vv = v_s[...]
    group = A_HEADS // A_KV_HEADS
    outs = []
    for h in range(A_HEADS):
        g = h // group
        sl = slice(g * HD, (g + 1) * HD)
        qh = (qr[h] * ATT_SCALE).astype(BF16)
        outs.append(_softmax_pv([_qk(qh, kk[:, sl]), _qk(qh, ck[:, sl])], [vv[:, sl], cv[:, sl]]))
    o_ref[...] = jnp.concatenate(outs, axis=-1).astype(BF16)


def attn_a_lat(z, ck, cv, gq, gk, e, y_prev):
    tq = A_LAT_TQ
    nq = T_LAT // tq
    cos, sin = _rope_tables()
    cos, sin = jnp.asarray(cos), jnp.asarray(sin)
    row0 = ROWS_CTX // tq
    seq0 = ROWS_CTX // T_LAT
    cache_spec = pl.BlockSpec((None, None, PAST, A_KV_W), lambda b, i: (b, e, 0, 0))
    return pl.pallas_call(
        _attn_a_lat_kernel,
        grid=(N_LAT, nq),
        in_specs=[pl.BlockSpec((tq, A_W), lambda b, i: (row0 + b * nq + i, 0)),
                  pl.BlockSpec((T_LAT, 2 * A_KV_W), lambda b, i: (seq0 + b, A_W // (2 * A_KV_W))),
                  cache_spec, cache_spec,
                  pl.BlockSpec((None, 1, HD), lambda b, i: (e, 0, 0)),
                  pl.BlockSpec((None, 1, HD), lambda b, i: (e, 0, 0)),
                  pl.BlockSpec((tq, LANES), lambda b, i: (i, 0)),
                  pl.BlockSpec((tq, LANES), lambda b, i: (i, 0)),
                  pl.BlockSpec((T_LAT, LANES), lambda b, i: (0, 0)),
                  pl.BlockSpec((T_LAT, LANES), lambda b, i: (0, 0)),
                  _untouched()],
        out_specs=pl.BlockSpec((tq, A_W), lambda b, i: (row0 + b * nq + i, 0)),
        out_shape=jax.ShapeDtypeStruct((ROWS, A_W), BF16),
        scratch_shapes=[pltpu.VMEM((T_LAT, A_KV_W), BF16),
                        pltpu.VMEM((T_LAT, A_KV_W), BF16)],
        input_output_aliases={10: 0},
        compiler_params=_cparams(("arbitrary", "arbitrary")),
        name="attn_a_lat",
    )(z, z, ck, cv, gq, gk, cos, sin, cos, sin, y_prev)


def _attn_c_ctx_kernel(q_ref, k_ref, v_ref, *rest):
    o_ref, ko_ref, vo_ref = rest[-3:]
    q = q_ref[...]
    k = k_ref[...]
    v = v_ref[...]
    ko_ref[...] = k
    vo_ref[...] = v
    outs = []
    for h in range(q.shape[1] // HD):
        sl = slice(h * HD, (h + 1) * HD)
        qh = (q[:, sl] * ATT_SCALE).astype(BF16)
        outs.append(_softmax_pv([_qk(qh, k[:, sl].astype(BF16))], [v[:, sl].astype(BF16)]))
    o_ref[...] = jnp.concatenate(outs, axis=-1).astype(BF16)


C_CTX_HEADS_PER_STEP = 8


def attn_c_ctx(z, o, n_odd, prev_caches):
    t = T_CTX
    wblk = C_CTX_HEADS_PER_STEP * HD
    nhp = C_W // wblk
    cache_spec = pl.BlockSpec((None, None, t, wblk), lambda b, p: (b, o, 0, p))
    cache_shape = jax.ShapeDtypeStruct((N_CTX, n_odd, t, C_W), F32)
    extra = [] if prev_caches is None else list(prev_caches)
    aliases = {} if prev_caches is None else {3: 1, 4: 2}
    return pl.pallas_call(
        _attn_c_ctx_kernel,
        grid=(N_CTX, nhp),
        in_specs=[pl.BlockSpec((t, wblk), lambda b, p: (b, p)),
                  pl.BlockSpec((t, wblk), lambda b, p: (b, nhp + p)),
                  pl.BlockSpec((t, wblk), lambda b, p: (b, 2 * nhp + p))] + [_untouched() for _ in extra],
        out_specs=[pl.BlockSpec((t, wblk), lambda b, p: (b, p)), cache_spec, cache_spec],
        out_shape=[jax.ShapeDtypeStruct((ROWS, C_W), BF16), cache_shape, cache_shape],
        input_output_aliases=aliases,
        compiler_params=_cparams(("arbitrary", "arbitrary")),
        name="attn_c_ctx",
    )(z, z, z, *extra)


NA_GRID_ROWS = T_LAT // GRID_W
NA_WIN = NA_ROWS * GRID_W
NA_DR = 2 * NA_ROWS - 1
NA_DC = 2 * NA_COLS - 1


def _na_row_start(r):
    return min(max(r - NA_ROWS // 2, 0), NA_GRID_ROWS - NA_ROWS)


def _na_groups():
    groups, r = [], 0
    while r < NA_GRID_ROWS:
        r1 = r
        while r1 + 1 < NA_GRID_ROWS and (_na_row_start(r1 + 1) == _na_row_start(r) or r1 + 1 - r < 4):
            r1 += 1
        lo = _na_row_start(r)
        hi = _na_row_start(r1) + NA_ROWS
        pairs = -(-(hi - lo) // 2)
        lo = min(lo, NA_GRID_ROWS - 2 * pairs)
        groups.append((r, r1, lo, pairs))
        r = r1 + 1
    return groups


def _na_pair_codes():
    codes = []
    plan = []
    for r0, r1, lo, pairs in _na_groups():
        rows = []
        for r in range(r0, r1 + 1):
            rs = _na_row_start(r)
            row = []
            for p in range(pairs):
                code = tuple((kr - r + NA_ROWS - 1) if rs <= kr < rs + NA_ROWS else None
                             for kr in (lo + 2 * p, lo + 2 * p + 1))
                if code not in codes:
                    codes.append(code)
                row.append(codes.index(code))
            rows.append(row)
        plan.append(rows)
    return codes, plan


def _na_bias_table(rpb):
    n_l = rpb.shape[0]
    col = np.arange(GRID_W)
    cs = np.clip(col - NA_COLS // 2, 0, GRID_W - NA_COLS)
    col_in = (col[None, :] >= cs[:, None]) & (col[None, :] < cs[:, None] + NA_COLS)
    period = GRID_W + 1
    seq = jnp.concatenate([rpb, jnp.zeros((n_l, C_HEADS, NA_DR, period - NA_DC), F32)], axis=-1)
    seq = jnp.roll(seq, -(NA_COLS - 1), axis=-1)
    tile = jnp.tile(seq, (1, 1, 1, GRID_W))[..., :GRID_W * GRID_W].reshape(n_l, C_HEADS, NA_DR, GRID_W, GRID_W)
    tile = jnp.where(jnp.asarray(col_in), tile, NEG_INF)
    masked = jnp.full((n_l, C_HEADS, GRID_W, GRID_W), NEG_INF, F32)
    pick = lambda dr: masked if dr is None else tile[:, :, dr]
    codes, _ = _na_pair_codes()
    return jnp.stack([jnp.concatenate([pick(a), pick(b)], axis=-1) for a, b in codes], axis=2)


def _attn_na_kernel(q_ref, k_ref, v_ref, ck_ref, cv_ref, bias_ref, prev_ref, o_ref):
    del prev_ref
    _, plan = _na_pair_codes()
    heads = []
    for h in range(LANES // HD):
        sl = slice(h * HD, (h + 1) * HD)
        q = (q_ref[:, sl] * ATT_SCALE).astype(BF16)
        k = k_ref[:, sl].astype(BF16)
        v = v_ref[:, sl].astype(BF16)
        ck = ck_ref[:, sl].astype(BF16)
        cv = cv_ref[:, sl].astype(BF16)
        rows = []
        for (r0, r1, lo, pairs), codes in zip(_na_groups(), plan):
            qg = q[r0 * GRID_W:(r1 + 1) * GRID_W]
            kw = k[lo * GRID_W:(lo + 2 * pairs) * GRID_W]
            vw = v[lo * GRID_W:(lo + 2 * pairs) * GRID_W]
            bias = jnp.concatenate(
                [jnp.concatenate([bias_ref[h, c] for c in row], axis=-1) for row in codes], axis=0)
            s_nb = _qk(qg, kw) + bias
            s_cx = _qk(qg, ck)
            rows.append(_softmax_pv([s_nb, s_cx], [vw, cv]))
        heads.append(jnp.concatenate(rows, axis=0))
    o_ref[...] = jnp.concatenate(heads, axis=-1).astype(BF16)


def attn_na(z, ck, cv, bias, o, y_prev):
    nhp = C_W // LANES
    hpb = LANES // HD
    seq0 = ROWS_CTX // T_LAT
    cache_spec = pl.BlockSpec((None, None, PAST, LANES), lambda p, b: (b, o, 0, p))
    return pl.pallas_call(
        _attn_na_kernel,
        grid=(nhp, N_LAT),
        in_specs=[pl.BlockSpec((T_LAT, LANES), lambda p, b: (seq0 + b, p)),
                  pl.BlockSpec((T_LAT, LANES), lambda p, b: (seq0 + b, nhp + p)),
                  pl.BlockSpec((T_LAT, LANES), lambda p, b: (seq0 + b, 2 * nhp + p)),
                  cache_spec, cache_spec,
                  pl.BlockSpec((None, hpb, bias.shape[2], GRID_W, LANES), lambda p, b: (o, p, 0, 0, 0)),
                  _untouched()],
        out_specs=pl.BlockSpec((T_LAT, LANES), lambda p, b: (seq0 + b, p)),
        out_shape=jax.ShapeDtypeStruct((ROWS, C_W), BF16),
        input_output_aliases={6: 0},
        compiler_params=_cparams(("arbitrary", "arbitrary")),
        name="attn_na",
    )(z, z, z, ck, cv, bias, y_prev)


def _seg_sum(x, n_heads):
    parts = []
    for h in range(n_heads):
        s = jnp.sum(x[:, h * HD:(h + 1) * HD], axis=-1, keepdims=True)
        parts.append(jnp.broadcast_to(s, (x.shape[0], HD)))
    return jnp.concatenate(parts, axis=-1)


PREP_TM = 256


def _seq_len_at(row_start):
    return jnp.where(row_start < ROWS_CTX, T_CTX, T_LAT)


def _rwkv_prep_kernel(z_ref, zprev_ref, znext_ref, mu_ref, kkw_ref, w0_ref, w2_ref, a0_ref, a2_ref, ka_ref,
                      rk_ref, g2_ref, o_rk_ref, o_wk_ref, o_bv_ref, g_ref, bonus_ref):
    z = z_ref[:, A_IN:]
    t = z.shape[0]
    start = pl.program_id(0) * t
    seq_len = _seq_len_at(start)
    pos = (start - jnp.where(start < ROWS_CTX, 0, ROWS_CTX)) % seq_len
    halo_prev = jnp.where(pos == 0, 0.0, zprev_ref[SUBLANES - 1:SUBLANES, A_IN:])
    halo_next = jnp.where(pos + t == seq_len, 0.0, znext_ref[0:1, A_IN:])
    row = lax.broadcasted_iota(jnp.int32, (t, 1), 0)
    prev = jnp.where(row == 0, halo_prev, pltpu.roll(z, 1, 0))
    nxt = jnp.where(row == t - 1, halo_next, pltpu.roll(z, t - 1, 0))
    m = z + mu_ref[...] * (0.5 * (prev + nxt) - z)
    r = m[:, :B_W]
    k = m[:, B_W:2 * B_W]
    v = m[:, 2 * B_W:3 * B_W]
    o = 3 * B_W
    wd = m[:, o:o + 2 * LORA_W]
    ad = m[:, o + 2 * LORA_W:o + 2 * LORA_W + 2 * LORA_A]
    gd = m[:, o + 2 * LORA_W + 2 * LORA_A:]

    kkr = k * kkw_ref[...]
    kk = kkr * lax.rsqrt(_seg_sum(kkr * kkr, B_HEADS) + 1e-12)

    wl = w0_ref[...] + _dot3(jnp.tanh(wd), w2_ref[...])
    decay = jnp.exp(-float(np.exp(-0.5)) * jax.nn.sigmoid(wl))
    a = jax.nn.sigmoid(a0_ref[...] + _dot3(ad, a2_ref[...]))
    k2 = jnp.concatenate([k, k], axis=-1)
    ka2 = jnp.concatenate([ka_ref[...], ka_ref[...]], axis=-1)
    kk2 = jnp.concatenate([kk, kk], axis=-1)

    kd = k2 * (1.0 + (a - 1.0) * ka2)
    bb = kk2 * a
    for h in range(B_HEADS):
        sl = slice(h * HD, (h + 1) * HD)
        o_rk_ref[pl.ds(h, t, stride=B_HEADS), :] = jnp.concatenate([r[:, sl], kk[:, sl]], axis=-1)
        for dr in range(2):
            sd = slice(dr * B_W + h * HD, dr * B_W + (h + 1) * HD)
            row = pl.ds(dr * B_HEADS + h, t, stride=2 * B_HEADS)
            o_wk_ref[row, :] = jnp.concatenate([decay[:, sd], kd[:, sd]], axis=-1)
            o_bv_ref[row, :] = jnp.concatenate([bb[:, sd], v[:, sl]], axis=-1)
    g_ref[...] = jnp.dot(jax.nn.sigmoid(gd).astype(BF16), g2_ref[...].astype(BF16), preferred_element_type=F32)
    bonus_ref[...] = _seg_sum(r * k * rk_ref[...], B_HEADS) * v


def rwkv_prep(z, params, e):
    tm = PREP_TM
    full = lambda a: pl.BlockSpec((None,) + a.shape[1:], lambda s: (e, 0, 0))
    out_rows = (B_HEADS, 2 * B_HEADS, 2 * B_HEADS)
    per = tm // SUBLANES
    last = ROWS // SUBLANES - 1
    rowspec = pl.BlockSpec((tm, B_W), lambda s: (s, 0))
    return pl.pallas_call(
        _rwkv_prep_kernel,
        grid=(ROWS // tm,),
        in_specs=[pl.BlockSpec((tm, EVEN_IN), lambda s: (s, 0)),
                  pl.BlockSpec((SUBLANES, EVEN_IN), lambda s: (jnp.maximum(s * per - 1, 0), 0)),
                  pl.BlockSpec((SUBLANES, EVEN_IN), lambda s: (jnp.minimum((s + 1) * per, last), 0))]
                 + [full(p) for p in params],
        out_specs=[pl.BlockSpec((tm * n, LANES), lambda s: (s, 0)) for n in out_rows] + [rowspec, rowspec],
        out_shape=[jax.ShapeDtypeStruct((ROWS * n, LANES), F32) for n in out_rows]
                  + [jax.ShapeDtypeStruct((ROWS, B_W), F32)] * 2,
        compiler_params=_cparams(("arbitrary",)),
        name="rwkv_prep",
    )(z, z, z, *params)


def rwkv_prep_params(prm):
    n_even = prm['b_mu'].shape[0]

    def block_diag(w):
        zero = jnp.zeros((n_even, LORA_W, B_W), F32)
        return jnp.concatenate([jnp.concatenate([w[:, 0], zero], 2), jnp.concatenate([zero, w[:, 1]], 2)], 1)

    row = lambda a: a.reshape(n_even, 1, -1)
    return (row(prm['b_mu']), row(prm['b_kk']), row(prm['b_w0']), block_diag(prm['b_w2']),
            row(prm['b_a0']), block_diag(prm['b_a2']), row(prm['b_ka']), row(prm['b_rk']), prm['b_g2'])


SCAN_CHAINS = 8


def _tree_sum(xs):
    while len(xs) > 1:
        xs = [xs[i] + xs[i + 1] for i in range(0, len(xs) - 1, 2)] + ([xs[-1]] if len(xs) % 2 else [])
    return xs[0]


def _scan_step(rk_b, wk_b, bv_b, vv, s_ref, ni):
    n_acc = max(1, SCAN_CHAINS // ni)

    def bcast(ref, row):
        return jnp.broadcast_to(ref[pl.ds(row, 1), :], (SUBLANES, LANES))

    sa = [[None] * n_acc for _ in range(ni)]
    for j in range(HD):
        kkj = bcast(rk_b, HD + j)
        for g in range(ni):
            p = s_ref[j, pl.ds(g * SUBLANES, SUBLANES), :] * kkj
            a = j % n_acc
            sa[g][a] = p if sa[g][a] is None else sa[g][a] + p
    sa = [-_tree_sum(x) for x in sa]
    ya = [[None] * n_acc for _ in range(ni)]
    for j in range(HD):
        wj = bcast(wk_b, j)
        kj = bcast(wk_b, HD + j)
        bj = bcast(bv_b, j)
        rj = bcast(rk_b, j)
        for g in range(ni):
            sl = pl.ds(g * SUBLANES, SUBLANES)
            s_new = s_ref[j, sl, :] * wj + sa[g] * bj + vv[g] * kj
            s_ref[j, sl, :] = s_new
            p = s_new * rj
            a = j % n_acc
            ya[g][a] = p if ya[g][a] is None else ya[g][a] + p
    return [_tree_sum(ya[g]) for g in range(ni)]


def _scan_pairs(tc, relayout, run):
    relayout(0, 0)

    def pair(k, carry):
        i0 = 2 * k
        relayout(i0 + 1, 1)
        run(i0, 0)
        relayout(jnp.minimum(i0 + 2, tc - 1), 0)
        run(i0 + 1, 1)
        return carry

    lax.fori_loop(0, tc // 2, pair, 0)


def _scan_ctx_kernel(rk_ref, wk_ref, bv_ref, y_ref, st_ref, rk0, wk0, bv0, rk1, wk1, bv1, s_ref, *, tc):
    d = pl.program_id(0)
    c = pl.program_id(1)
    ni = HD // SUBLANES
    bufs = ((rk0, wk0, bv0), (rk1, wk1, bv1))

    @pl.when(c == 0)
    def _():
        s_ref[...] = jnp.zeros_like(s_ref)

    def t_of(i):
        return i + d * (tc - 1 - 2 * i)

    def relayout(i, slot):
        t = t_of(i)
        for src, dst in zip((rk_ref, wk_ref, bv_ref), bufs[slot]):
            dst[...] = src[:, :, t].reshape(LANES, LANES).T

    def run(i, slot):
        rk_b, wk_b, bv_b = bufs[slot]
        vv = [bv_b[pl.ds(HD + g * SUBLANES, SUBLANES), :] for g in range(ni)]
        y = jnp.concatenate(_scan_step(rk_b, wk_b, bv_b, vv, s_ref, ni), axis=0)
        y_ref[:, :, t_of(i)] = y.T.reshape(CTX_GROUPS, ROW_SUB, B_HEADS, HD)

    _scan_pairs(tc, relayout, run)

    @pl.when(c == pl.num_programs(1) - 1)
    def _():
        st_ref[...] = s_ref[...]


def _rows5(x, heads):
    return x.reshape(ROW_GROUPS, ROW_SUB, T_CTX, heads, x.shape[-1])


def _operand_bufs():
    return [pltpu.VMEM((LANES, LANES), F32)] * 6


def rwkv_scan_ctx(prep, tc):
    rk, wk, bv = prep[:3]
    nc = T_CTX // tc

    def chunk(d, c):
        return c + d * (nc - 1 - 2 * c)

    blk = (CTX_GROUPS, ROW_SUB, tc, B_HEADS, LANES)
    shared = pl.BlockSpec(blk, lambda d, c: (0, 0, chunk(d, c), 0, 0))
    perdir = pl.BlockSpec(blk, lambda d, c: (0, 0, chunk(d, c), d, 0))
    yspec = pl.BlockSpec((CTX_GROUPS, ROW_SUB, tc, B_HEADS, HD), lambda d, c: (0, 0, chunk(d, c), d, 0))
    return pl.pallas_call(
        functools.partial(_scan_ctx_kernel, tc=tc),
        grid=(2, nc),
        in_specs=[shared, perdir, perdir],
        out_specs=[yspec, pl.BlockSpec((None, HD, HD, LANES), lambda d, c: (d, 0, 0, 0))],
        out_shape=[jax.ShapeDtypeStruct((ROW_GROUPS, ROW_SUB, T_CTX, 2 * B_HEADS, HD), F32),
                   jax.ShapeDtypeStruct((2, HD, HD, LANES), F32)],
        scratch_shapes=_operand_bufs() + [pltpu.VMEM((HD, HD, LANES), F32)],
        compiler_params=_cparams(("arbitrary", "arbitrary")),
        name="rwkv_scan_ctx",
    )(_rows5(rk, B_HEADS), _rows5(wk, 2 * B_HEADS), _rows5(bv, 2 * B_HEADS))


LAT_REP = LANES // (2 * N_LAT * B_HEADS)


def _scan_lat_kernel(rkf_ref, rkb_ref, wkf_ref, wkb_ref, bvf_ref, bvb_ref, s0_ref, yf_ref, yb_ref,
                     rk0, wk0, bv0, rk1, wk1, bv1, s_ref, y_s, *, tc):
    c = pl.program_id(0)
    n = N_LAT * B_HEADS
    nd = 2 * n
    ni = HD // (LAT_REP * SUBLANES)
    bufs = ((rk0, wk0, bv0), (rk1, wk1, bv1))

    @pl.when(c == 0)
    def _():
        s_ref[...] = s0_ref[...]

    def relayout(i, slot):
        srcs = ((rkf_ref, rkb_ref), (wkf_ref, wkb_ref), (bvf_ref, bvb_ref))
        for (fwd, bwd), dst in zip(srcs, bufs[slot]):
            m = jnp.concatenate([fwd[:, 0, i].reshape(n, LANES), bwd[:, 0, tc - 1 - i].reshape(n, LANES)], axis=0)
            dst[...] = jnp.concatenate([m] * LAT_REP, axis=0).T

    def run(i, slot):
        rk_b, wk_b, bv_b = bufs[slot]
        group = lax.broadcasted_iota(jnp.int32, (SUBLANES, LANES), 1) // nd
        vv = []
        for ig in range(ni):
            v = bv_b[pl.ds(HD + ig * SUBLANES, SUBLANES), :]
            for g in range(1, LAT_REP):
                v = jnp.where(group == g, bv_b[pl.ds(HD + (g * ni + ig) * SUBLANES, SUBLANES), :], v)
            vv.append(v)
        y_s[i] = jnp.concatenate(_scan_step(rk_b, wk_b, bv_b, vv, s_ref, ni), axis=0)

    _scan_pairs(tc, relayout, run)

    def store(i, carry):
        y = y_s[i]
        rows = [y] + [pltpu.roll(y, LANES - g * nd, 1) for g in range(1, LAT_REP)]
        yt = jnp.concatenate(rows, axis=0).T
        yf_ref[:, 0, i] = yt[:n].reshape(N_LAT, B_HEADS, HD)
        yb_ref[:, 0, tc - 1 - i] = yt[n:nd].reshape(N_LAT, B_HEADS, HD)
        return carry

    lax.fori_loop(0, tc, store, 0, unroll=8)


def rwkv_scan_lat(prep, s0f, s0b, tc):
    rk, wk, bv = prep[:3]
    nc = T_LAT // tc
    per_sub = T_CTX // tc
    g0 = CTX_GROUPS // N_LAT

    def fwd_map(head_block):
        return lambda c: (g0, c // per_sub, c % per_sub, head_block, 0)

    def bwd_map(head_block):
        return lambda c: (g0, (nc - 1 - c) // per_sub, (nc - 1 - c) % per_sub, head_block, 0)

    blk = (N_LAT, 1, tc, B_HEADS, LANES)
    yblk = (N_LAT, 1, tc, B_HEADS, HD)
    ni8 = HD // LAT_REP
    s0 = jnp.stack([s0f, s0b]).reshape(2, N_LAT, B_HEADS, LAT_REP, ni8, HD)
    s0 = s0.transpose(5, 4, 3, 0, 1, 2).reshape(HD, ni8, LANES)
    yshape = jax.ShapeDtypeStruct((N_LAT, ROW_SUB, T_CTX, B_HEADS, HD), F32)
    return pl.pallas_call(
        functools.partial(_scan_lat_kernel, tc=tc),
        grid=(nc,),
        in_specs=[pl.BlockSpec(blk, fwd_map(0)), pl.BlockSpec(blk, bwd_map(0)),
                  pl.BlockSpec(blk, fwd_map(0)), pl.BlockSpec(blk, bwd_map(1)),
                  pl.BlockSpec(blk, fwd_map(0)), pl.BlockSpec(blk, bwd_map(1)),
                  pl.BlockSpec((HD, ni8, LANES), lambda c: (0, 0, 0))],
        out_specs=[pl.BlockSpec(yblk, lambda c: (0, c // per_sub, c % per_sub, 0, 0)),
                   pl.BlockSpec(yblk, lambda c: (0, (nc - 1 - c) // per_sub, (nc - 1 - c) % per_sub, 0, 0))],
        out_shape=[yshape, yshape],
        scratch_shapes=_operand_bufs() + [pltpu.VMEM((HD, ni8, LANES), F32),
                                          pltpu.VMEM((tc, ni8, LANES), F32)],
        compiler_params=_cparams(("arbitrary",)),
        name="rwkv_scan_lat",
    )(_rows5(rk, B_HEADS), _rows5(rk, B_HEADS), _rows5(wk, 2 * B_HEADS), _rows5(wk, 2 * B_HEADS),
      _rows5(bv, 2 * B_HEADS), _rows5(bv, 2 * B_HEADS), s0)


def _rwkv_post_kernel(yc_ref, ylf_ref, ylb_ref, bonus_ref, g_ref, lng_ref, lnb_ref, o_ref, *, n_ctx_blocks):
    tm = o_ref.shape[0]

    def emit(load):
        outs = []
        for h in range(B_HEADS):
            sl = slice(h * HD, (h + 1) * HD)
            y = load(h)
            yc = y - jnp.mean(y, axis=-1, keepdims=True)
            var = jnp.mean(yc * yc, axis=-1, keepdims=True)
            yn = yc * lax.rsqrt(var + GN_EPS) * lng_ref[:, sl] + lnb_ref[:, sl]
            outs.append((yn + bonus_ref[:, sl]) * g_ref[:, sl])
        o_ref[...] = jnp.concatenate(outs, axis=-1).astype(BF16)

    @pl.when(pl.program_id(0) < n_ctx_blocks)
    def _():
        emit(lambda h: yc_ref[pl.ds(h, tm, stride=2 * B_HEADS), :]
             + yc_ref[pl.ds(B_HEADS + h, tm, stride=2 * B_HEADS), :])

    @pl.when(pl.program_id(0) >= n_ctx_blocks)
    def _():
        emit(lambda h: ylf_ref[pl.ds(h, tm, stride=B_HEADS), :] + ylb_ref[pl.ds(h, tm, stride=B_HEADS), :])


def rwkv_post(y_ctx, y_lat_f, y_lat_b, bonus, g, lng, lnb, e):
    tm = 256
    nb = ROWS_CTX // tm
    rowspec = pl.BlockSpec((tm, B_W), lambda i: (i, 0))
    vecspec = pl.BlockSpec((None, 1, B_W), lambda i: (e, 0, 0))
    latspec = pl.BlockSpec((tm * B_HEADS, HD), lambda i: (jnp.maximum(i - nb, 0), 0))
    return pl.pallas_call(
        functools.partial(_rwkv_post_kernel, n_ctx_blocks=nb),
        grid=(ROWS // tm,),
        in_specs=[pl.BlockSpec((tm * 2 * B_HEADS, HD), lambda i: (jnp.minimum(i, nb - 1), 0)), latspec, latspec,
                  rowspec, rowspec, vecspec, vecspec],
        out_specs=rowspec,
        out_shape=jax.ShapeDtypeStruct((ROWS, B_W), BF16),
        compiler_params=_cparams(("arbitrary",)),
        name="rwkv_post",
    )(y_ctx, y_lat_f, y_lat_b, bonus, g, lng, lnb)


def kernel(x_prompt, x_sample, cache_a_k, cache_a_v, state_b_fwd, state_b_bwd, cache_c_k, cache_c_v, c, c_ctx,
           ada_w, ada_b, norm1_g, norm2_g, final_norm_g, w_in_e, w_out_e, a_q_gain, a_k_gain, b_mu, b_w0, b_w2,
           b_a0, b_a2, b_g2, b_kk, b_ka, b_rk, b_ln_g, b_ln_b, ffn_w_gu, ffn_w_dn, w_in_o, w_out_o, c_rpb,
           router_w, router_b, moe_w_gu, moe_w_dn):
    n_even, n_odd = w_in_e.shape[0], w_in_o.shape[0]
    prep_params = rwkv_prep_params(dict(b_mu=b_mu, b_w0=b_w0, b_w2=b_w2, b_a0=b_a0, b_a2=b_a2, b_g2=b_g2,
                                        b_kk=b_kk, b_ka=b_ka, b_rk=b_rk))
    cond = jnp.zeros((N_MODS, D), F32).at[0].set(c_ctx).at[1:1 + N_LAT].set(c)
    mods = ada_all(cond, ada_w, ada_b).reshape(DEPTH, N_MODS, 6, D)
    g1 = norm1_g.reshape(DEPTH, 1, D)
    g2 = norm2_g.reshape(DEPTH, 1, D)
    gq = a_q_gain.reshape(n_even, 1, HD)
    gk = a_k_gain.reshape(n_even, 1, HD)
    lng = b_ln_g.reshape(n_even, 1, B_W)
    lnb = b_ln_b.reshape(n_even, 1, B_W)
    ck_a = cache_a_k.reshape(N_LAT, n_even, PAST, A_KV_W)
    cv_a = cache_a_v.reshape(N_LAT, n_even, PAST, A_KV_W)
    ck_c = cache_c_k.reshape(N_LAT, n_odd, PAST, C_W)
    cv_c = cache_c_v.reshape(N_LAT, n_odd, PAST, C_W)
    na_bias = _na_bias_table(c_rpb)
    rw_p = jnp.zeros((n_odd, D, LANES), F32).at[:, :, :N_EXPERTS].set(router_w)
    rb_p = jnp.zeros((n_odd, 1, LANES), F32).at[:, 0, :N_EXPERTS].set(router_b)
    moe_gu_b = moe_w_gu.astype(BF16)
    moe_dn_b = moe_w_dn.astype(BF16)

    x, h = first_norm(x_prompt, x_sample, g1, mods)

    a_caches, c_caches = None, None
    new_sf, new_sb = [], []
    y_final = None
    for l in range(DEPTH):
        if l % 2 == 0:
            e = l // 2
            z = mm_in(h, w_in_e, e, tn=EVEN_IN // 3)
            y_a, k_new, v_new = attn_a_ctx(z, gq, gk, e, a_caches)
            a_caches = (k_new, v_new)
            y_a = attn_a_lat(z, ck_a, cv_a, gq, gk, e, y_a)
            prep = rwkv_prep(z, prep_params, e)
            y_s, st = rwkv_scan_ctx(prep, tc=32)
            y_lf, y_lb = rwkv_scan_lat(prep, state_b_fwd[:, e], state_b_bwd[:, e], tc=64)
            st = st.reshape(2, HD, HD, N_CTX, B_HEADS).transpose(0, 3, 4, 2, 1)
            new_sf.append(st[0])
            new_sb.append(st[1])
            y_b = rwkv_post(y_s.reshape(ROWS * 2 * B_HEADS, HD), y_lf.reshape(ROWS_LAT * B_HEADS, HD),
                            y_lb.reshape(ROWS_LAT * B_HEADS, HD), prep[4], prep[3], lng, lnb, e)
            x, h = mm_out([y_a, y_b], w_out_e, e, x, mods, l, g2, l)
            x, h = ffn(h, ffn_w_gu, ffn_w_dn, e, x, mods, l, g1)
        else:
            o = l // 2
            z = mm_in(h, w_in_o, o, tn=C_W)
            y, k_new, v_new = attn_c_ctx(z, o, n_odd, c_caches)
            c_caches = (k_new, v_new)
            y = attn_na(z, ck_c, cv_c, na_bias, o, y)
            x, h = mm_out([y], w_out_o, o, x, mods, l, g2, l)
            routing = router(x, g2, mods, l, rw_p, rb_p, o)
            y_moe = moe_routed(h, routing, moe_gu_b, moe_dn_b, o)
            if l + 1 < DEPTH:
                x, h = resid(x, y_moe, mods, l, g1, False)
            else:
                y_final = resid(x, y_moe, mods, l, final_norm_g[None], True)

    y_prompt = y_final[0].reshape(N_CTX, T_CTX, D)
    y_sample = y_final[1].reshape(N_LAT, T_LAT, D)
    return (y_prompt, y_sample,
            a_caches[0].reshape(N_CTX, n_even, T_CTX, A_KV_HEADS, HD),
            a_caches[1].reshape(N_CTX, n_even, T_CTX, A_KV_HEADS, HD),
            jnp.stack(new_sf, axis=1), jnp.stack(new_sb, axis=1),
            c_caches[0].reshape(N_CTX, n_odd, T_CTX, C_HEADS, HD),
            c_caches[1].reshape(N_CTX, n_odd, T_CTX, C_HEADS, HD))
```

```python
import functools

import numpy as np
import jax
import jax.numpy as jnp
from jax import lax
from jax.experimental import pallas as pl
from jax.experimental.pallas import tpu as pltpu

F32 = jnp.float32
BF16 = jnp.bfloat16
HIGHEST = lax.Precision.HIGHEST

D = 1024
N_CTX, T_CTX = 16, 256
N_LAT, T_LAT = 2, 1024
ROWS_CTX = N_CTX * T_CTX
ROWS_LAT = N_LAT * T_LAT
ROWS = ROWS_CTX + ROWS_LAT
DEPTH = 4
GRID_W = 64
HD = 64
A_HEADS, A_KV_HEADS, B_HEADS, C_HEADS = 8, 2, 8, 16
A_W, A_KV_W, B_W, C_W = A_HEADS * HD, A_KV_HEADS * HD, B_HEADS * HD, C_HEADS * HD
A_IN = A_W + 2 * A_KV_W
LORA_W, LORA_A, LORA_G = 64, 64, 128
B_IN = 3 * B_W + 2 * LORA_W + 2 * LORA_A + LORA_G
EVEN_IN = A_IN + B_IN
PAST = 512
NA_ROWS, NA_COLS = 8, 16
D_FF = 2816
N_EXPERTS = 8
D_EXPERT = 1408
ROPE_THETA = 10000.0
EPS = 1e-6
GN_EPS = 64e-5
NEG_INF = -1e30
ATT_SCALE = HD ** -0.5

LANES = 128
SUBLANES = 8
VMEM_LIMIT = 56 * 1024 * 1024

N_MODS = 8

ROW_SUB = T_LAT // T_CTX
ROW_GROUPS = ROWS // (ROW_SUB * T_CTX)
CTX_GROUPS = N_CTX // ROW_SUB


def _cparams(sem):
    return pltpu.CompilerParams(dimension_semantics=sem, vmem_limit_bytes=VMEM_LIMIT)


def _untouched():
    return pl.BlockSpec(memory_space=pl.ANY)


def _mod_index(row_start):
    return jnp.where(row_start < ROWS_CTX, 0, 1 + (row_start - ROWS_CTX) // T_LAT)


def _modspec(layer, tm, nargs=1):
    if nargs == 1:
        return pl.BlockSpec((None, 1, 6, D), lambda i: (layer, _mod_index(i * tm), 0, 0))
    return pl.BlockSpec((None, 1, 6, D), lambda i, j: (layer, _mod_index(i * tm), 0, 0))


def _gainspec(layer, nargs=1):
    if nargs == 1:
        return pl.BlockSpec((None, 1, D), lambda i: (layer, 0, 0))
    return pl.BlockSpec((None, 1, D), lambda i, j: (layer, 0, 0))


def _modnorm(x, g, shift, scale):
    ms = jnp.mean(x * x, axis=-1, keepdims=True)
    return (x * lax.rsqrt(ms + EPS) * g) * (1.0 + scale) + shift


def _silu(x):
    return x * jax.nn.sigmoid(x)


def _ada_kernel(c_ref, w_ref, b_ref, o_ref):
    s = _silu(c_ref[...]).astype(BF16)
    o_ref[0] = jnp.dot(s, w_ref[0].astype(BF16), preferred_element_type=F32) + b_ref[0]


def ada_all(cond, ada_w, ada_b):
    tn = 1536
    n = 6 * D
    return pl.pallas_call(
        _ada_kernel,
        grid=(DEPTH, n // tn),
        in_specs=[pl.BlockSpec((N_MODS, D), lambda l, j: (0, 0)),
                  pl.BlockSpec((1, D, tn), lambda l, j: (l, 0, j)),
                  pl.BlockSpec((1, 1, tn), lambda l, j: (l, 0, j))],
        out_specs=pl.BlockSpec((1, N_MODS, tn), lambda l, j: (l, 0, j)),
        out_shape=jax.ShapeDtypeStruct((DEPTH, N_MODS, n), F32),
        compiler_params=_cparams(("arbitrary", "arbitrary")),
        name="ada",
    )(cond, ada_w, ada_b.reshape(DEPTH, 1, n))


def _first_norm_kernel(xp_ref, xs_ref, g_ref, mod_ref, x_ref, h_ref, *, n_ctx_blocks):
    i = pl.program_id(0)

    def emit(src):
        x = src[...]
        x_ref[...] = x
        h_ref[...] = _modnorm(x, g_ref[...], mod_ref[0, 0:1, :], mod_ref[0, 1:2, :]).astype(BF16)

    @pl.when(i < n_ctx_blocks)
    def _():
        emit(xp_ref)

    @pl.when(i >= n_ctx_blocks)
    def _():
        emit(xs_ref)


def first_norm(x_prompt, x_sample, gains, mods):
    tm = 1024
    nc = ROWS_CTX // tm
    rowspec = pl.BlockSpec((tm, D), lambda i: (i, 0))
    return pl.pallas_call(
        functools.partial(_first_norm_kernel, n_ctx_blocks=nc),
        grid=(ROWS // tm,),
        in_specs=[pl.BlockSpec((tm, D), lambda i: (jnp.minimum(i, nc - 1), 0)),
                  pl.BlockSpec((tm, D), lambda i: (jnp.maximum(i - nc, 0), 0)),
                  _gainspec(0), _modspec(0, tm)],
        out_specs=[rowspec, rowspec],
        out_shape=[jax.ShapeDtypeStruct((ROWS, D), F32), jax.ShapeDtypeStruct((ROWS, D), BF16)],
        compiler_params=_cparams(("arbitrary",)),
        name="first_norm",
    )(x_prompt.reshape(ROWS_CTX, D), x_sample.reshape(ROWS_LAT, D), gains, mods)


def _mm_in_kernel(h_ref, w_ref, o_ref, wb_ref):
    @pl.when(pl.program_id(1) == 0)
    def _():
        wb_ref[...] = w_ref[...].astype(BF16)

    o_ref[...] = jnp.dot(h_ref[...], wb_ref[...], preferred_element_type=F32)


def mm_in(h, w, layer, tn, tm=2048):
    _, k, n = w.shape
    return pl.pallas_call(
        _mm_in_kernel,
        grid=(n // tn, ROWS // tm),
        in_specs=[pl.BlockSpec((tm, k), lambda j, i: (i, 0)),
                  pl.BlockSpec((None, k, tn), lambda j, i: (layer, 0, j))],
        out_specs=pl.BlockSpec((tm, tn), lambda j, i: (i, j)),
        out_shape=jax.ShapeDtypeStruct((ROWS, n), F32),
        scratch_shapes=[pltpu.VMEM((k, tn), BF16)],
        compiler_params=_cparams(("arbitrary", "arbitrary")),
        name="mm_in",
    )(h, w)


def _residual_epilogue(x, acc, mod_ref, nmod_ref, g_ref, xo_ref, ho_ref, gate_idx, shift_idx, scale_idx, final):
    gate = mod_ref[0, gate_idx:gate_idx + 1, :]
    xn = x + gate * acc
    if final:
        ms = jnp.mean(xn * xn, axis=-1, keepdims=True)
        ho_ref[...] = xn * lax.rsqrt(ms + EPS) * g_ref[...]
    else:
        xo_ref[...] = xn
        shift = nmod_ref[0, shift_idx:shift_idx + 1, :]
        scale = nmod_ref[0, scale_idx:scale_idx + 1, :]
        ho_ref[...] = _modnorm(xn, g_ref[...], shift, scale).astype(BF16)


def _mm_out_kernel(*refs, n_parts, gate_idx, shift_idx, scale_idx):
    y_refs = refs[:n_parts]
    w_refs = refs[n_parts:2 * n_parts]
    x_ref, mod_ref, nmod_ref, g_ref, xo_ref, ho_ref = refs[2 * n_parts:2 * n_parts + 6]
    wb_refs = refs[2 * n_parts + 6:]

    @pl.when(pl.program_id(0) == 0)
    def _():
        for w_ref, wb_ref in zip(w_refs, wb_refs):
            wb_ref[...] = w_ref[...].astype(BF16)

    acc = None
    for y_ref, wb_ref in zip(y_refs, wb_refs):
        p = jnp.dot(y_ref[...], wb_ref[...], preferred_element_type=F32)
        acc = p if acc is None else acc + p
    _residual_epilogue(x_ref[...], acc, mod_ref, nmod_ref, g_ref, xo_ref, ho_ref,
                       gate_idx, shift_idx, scale_idx, False)


def mm_out(parts, w, layer_w, x, mods, layer, gains, gain_layer):
    tm = 1024
    kp = parts[0].shape[1]
    n_parts = len(parts)
    rowspec = pl.BlockSpec((tm, D), lambda i: (i, 0))
    return pl.pallas_call(
        functools.partial(_mm_out_kernel, n_parts=n_parts, gate_idx=2, shift_idx=3, scale_idx=4),
        grid=(ROWS // tm,),
        in_specs=[pl.BlockSpec((tm, kp), lambda i: (i, 0)) for _ in parts]
                 + [pl.BlockSpec((None, kp, D), lambda i, p=p: (layer_w, p, 0)) for p in range(n_parts)]
                 + [rowspec, _modspec(layer, tm), _modspec(layer, tm), _gainspec(gain_layer)],
        out_specs=[rowspec, rowspec],
        out_shape=[jax.ShapeDtypeStruct((ROWS, D), F32), jax.ShapeDtypeStruct((ROWS, D), BF16)],
        scratch_shapes=[pltpu.VMEM((kp, D), BF16) for _ in parts],
        compiler_params=_cparams(("arbitrary",)),
        name="mm_out",
    )(*parts, *([w] * n_parts), x, mods, mods, gains)


FFN_TF = 256
FFN_NF = D_FF // FFN_TF


def _ffn_kernel(h_ref, wg_ref, wu_ref, wd_ref, x_ref, mod_ref, nmod_ref, g_ref, xo_ref, ho_ref,
                wg_s, wu_s, wd_s, acc_ref):
    i = pl.program_id(0)
    f = pl.program_id(1)

    @pl.when(i == 0)
    def _():
        wg_s[f] = wg_ref[...].astype(BF16)
        wu_s[f] = wu_ref[...].astype(BF16)
        wd_s[f] = wd_ref[...].astype(BF16)

    @pl.when(f == 0)
    def _():
        acc_ref[...] = jnp.zeros_like(acc_ref)

    h = h_ref[...]
    gp = jnp.dot(h, wg_s[f], preferred_element_type=F32)
    up = jnp.dot(h, wu_s[f], preferred_element_type=F32)
    a = (_silu(gp) * up).astype(BF16)
    acc_ref[...] += jnp.dot(a, wd_s[f], preferred_element_type=F32)

    @pl.when(f == FFN_NF - 1)
    def _():
        _residual_epilogue(x_ref[...], acc_ref[...], mod_ref, nmod_ref, g_ref, xo_ref, ho_ref, 5, 0, 1, False)


def ffn(h, w_gu, w_dn, e, x, mods, layer, gains):
    tm = 1024
    tf, nf = FFN_TF, FFN_NF

    def once(i, f):
        return jnp.where(i == 0, f, nf - 1)

    rowspec = pl.BlockSpec((tm, D), lambda i, f: (i, 0))
    return pl.pallas_call(
        _ffn_kernel,
        grid=(ROWS // tm, nf),
        in_specs=[rowspec,
                  pl.BlockSpec((None, D, tf), lambda i, f: (e, 0, once(i, f))),
                  pl.BlockSpec((None, D, tf), lambda i, f: (e, 0, nf + once(i, f))),
                  pl.BlockSpec((None, tf, D), lambda i, f: (e, once(i, f), 0)),
                  rowspec, _modspec(layer, tm, 2), _modspec(layer + 1, tm, 2), _gainspec(layer + 1, 2)],
        out_specs=[rowspec, rowspec],
        out_shape=[jax.ShapeDtypeStruct((ROWS, D), F32), jax.ShapeDtypeStruct((ROWS, D), BF16)],
        scratch_shapes=[pltpu.VMEM((nf, D, tf), BF16),
                        pltpu.VMEM((nf, D, tf), BF16),
                        pltpu.VMEM((nf, tf, D), BF16),
                        pltpu.VMEM((tm, D), F32)],
        compiler_params=_cparams(("arbitrary", "arbitrary")),
        name="ffn",
    )(h, w_gu, w_gu, w_dn, x, mods, mods, gains)


MOE_TILE = 1024
MOE_CHUNK = 320


def _router_kernel(x_ref, g_ref, mod_ref, rw_ref, rb_ref, gates_ref, pos_ref, gates_t_ref, pos_t_ref, carry_ref):
    h = _modnorm(x_ref[...], g_ref[...], mod_ref[0, 3:4, :], mod_ref[0, 4:5, :])
    logits = jnp.dot(h, rw_ref[...], precision=HIGHEST, preferred_element_type=F32)
    lane = lax.broadcasted_iota(jnp.int32, logits.shape, 1)
    sel = jnp.where(lane < N_EXPERTS, logits + rb_ref[...], -jnp.inf)
    m1 = jnp.max(sel, axis=-1, keepdims=True)
    i1 = jnp.min(jnp.where(sel == m1, lane, LANES), axis=-1, keepdims=True)
    sel2 = jnp.where(lane == i1, -jnp.inf, sel)
    m2 = jnp.max(sel2, axis=-1, keepdims=True)
    i2 = jnp.min(jnp.where(sel2 == m2, lane, LANES), axis=-1, keepdims=True)
    l1 = jnp.sum(jnp.where(lane == i1, logits, 0.0), axis=-1, keepdims=True)
    l2 = jnp.sum(jnp.where(lane == i2, logits, 0.0), axis=-1, keepdims=True)
    mx = jnp.maximum(l1, l2)
    e1 = jnp.exp(l1 - mx)
    e2 = jnp.exp(l2 - mx)
    den = e1 + e2
    gates = jnp.where(lane == i1, e1 / den, 0.0) + jnp.where(lane == i2, e2 / den, 0.0)
    gates_ref[...] = gates

    tm = gates.shape[0]
    routed = gates > 0.0
    tri = (lax.broadcasted_iota(jnp.int32, (tm, tm), 0) > lax.broadcasted_iota(jnp.int32, (tm, tm), 1))
    local = jnp.dot(jnp.where(tri, 1.0, 0.0).astype(BF16), jnp.where(routed, 1.0, 0.0).astype(BF16),
                    preferred_element_type=F32)

    @pl.when(pl.program_id(0) % (MOE_TILE // tm) == 0)
    def _():
        carry_ref[...] = jnp.zeros_like(carry_ref)

    pos = local + carry_ref[...]
    carry_ref[...] += jnp.sum(jnp.where(routed, 1.0, 0.0), axis=0, keepdims=True)
    pos_ref[...] = pos
    gates_t_ref[...] = gates.T[:SUBLANES]
    pos_t_ref[...] = pos.T[:SUBLANES]


def router(x, gains, mods, layer, rw_p, rb_p, o):
    tm = 512
    rowspec = pl.BlockSpec((tm, LANES), lambda i: (i, 0))
    colspec = pl.BlockSpec((SUBLANES, tm), lambda i: (0, i))
    return pl.pallas_call(
        _router_kernel,
        grid=(ROWS // tm,),
        in_specs=[pl.BlockSpec((tm, D), lambda i: (i, 0)),
                  _gainspec(layer), _modspec(layer, tm),
                  pl.BlockSpec((None, D, LANES), lambda i: (o, 0, 0)),
                  pl.BlockSpec((None, 1, LANES), lambda i: (o, 0, 0))],
        out_specs=[rowspec, rowspec, colspec, colspec],
        out_shape=[jax.ShapeDtypeStruct((ROWS, LANES), F32), jax.ShapeDtypeStruct((ROWS, LANES), F32),
                   jax.ShapeDtypeStruct((SUBLANES, ROWS), F32), jax.ShapeDtypeStruct((SUBLANES, ROWS), F32)],
        scratch_shapes=[pltpu.VMEM((1, LANES), F32)],
        compiler_params=_cparams(("arbitrary",)),
        name="router",
    )(x, gains, mods, rw_p, rb_p)


def _moe_kernel(cnt_ref, h_ref, gates_ref, pos_ref, gates_t_ref, pos_t_ref, wgu_ref, wdn_ref, o_ref, acc_ref):
    tile = pl.program_id(0)
    e = pl.program_id(1)
    tt = h_ref.shape[0]

    @pl.when(e == 0)
    def _():
        acc_ref[...] = jnp.zeros_like(acc_ref)

    lane = lax.broadcasted_iota(jnp.int32, (tt, LANES), 1)
    g_col = jnp.sum(jnp.where(lane == e, gates_ref[...], 0.0), axis=1, keepdims=True)
    p_col = jnp.sum(jnp.where(lane == e, pos_ref[...], 0.0), axis=1, keepdims=True)
    sub = lax.broadcasted_iota(jnp.int32, (SUBLANES, tt), 0)
    g_row = jnp.sum(jnp.where(sub == e, gates_t_ref[...], 0.0), axis=0, keepdims=True)
    p_row = jnp.sum(jnp.where(sub == e, pos_t_ref[...], 0.0), axis=0, keepdims=True)
    p_col = jnp.where(g_col > 0.0, p_col, -1.0)
    p_row = jnp.where(g_row > 0.0, p_row, -1.0)

    def chunk(ci, carry):
        base = (ci * MOE_CHUNK).astype(F32)
        take = p_row == lax.broadcasted_iota(jnp.int32, (MOE_CHUNK, tt), 0).astype(F32) + base
        xs = jnp.dot(jnp.where(take, 1.0, 0.0).astype(BF16), h_ref[...], preferred_element_type=F32)
        gu = jnp.dot(xs.astype(BF16), wgu_ref[...], preferred_element_type=F32)
        a = (_silu(gu[:, :D_EXPERT]) * gu[:, D_EXPERT:]).astype(BF16)
        y = jnp.dot(a, wdn_ref[...], preferred_element_type=F32)
        gate = jnp.sum(jnp.where(take, g_row, 0.0), axis=1, keepdims=True)
        put = p_col == lax.broadcasted_iota(jnp.int32, (tt, MOE_CHUNK), 1).astype(F32) + base
        acc_ref[...] += jnp.dot(jnp.where(put, 1.0, 0.0).astype(BF16), (gate * y).astype(BF16),
                                preferred_element_type=F32)
        return carry

    count = cnt_ref[tile * N_EXPERTS + e]
    lax.fori_loop(0, (count + MOE_CHUNK - 1) // MOE_CHUNK, chunk, 0)

    @pl.when(e == N_EXPERTS - 1)
    def _():
        o_ref[...] = acc_ref[...]


def moe_routed(h, routing, w_gu, w_dn, o):
    gates, pos, gates_t, pos_t = routing
    tt = MOE_TILE
    n_tiles = ROWS // tt
    counts = jnp.sum((gates[:, :N_EXPERTS] > 0.0).reshape(n_tiles, tt, N_EXPERTS), axis=1)
    counts = counts.astype(jnp.int32).reshape(n_tiles * N_EXPERTS)
    rowspec = pl.BlockSpec((tt, LANES), lambda t, e, c: (t, 0))
    colspec = pl.BlockSpec((SUBLANES, tt), lambda t, e, c: (0, t))
    grid_spec = pltpu.PrefetchScalarGridSpec(
        num_scalar_prefetch=1,
        grid=(n_tiles, N_EXPERTS),
        in_specs=[pl.BlockSpec((tt, D), lambda t, e, c: (t, 0)),
                  rowspec, rowspec, colspec, colspec,
                  pl.BlockSpec((None, None, D, 2 * D_EXPERT), lambda t, e, c: (o, e, 0, 0)),
                  pl.BlockSpec((None, None, D_EXPERT, D), lambda t, e, c: (o, e, 0, 0))],
        out_specs=pl.BlockSpec((tt, D), lambda t, e, c: (t, 0)),
        scratch_shapes=[pltpu.VMEM((tt, D), F32)])
    return pl.pallas_call(
        _moe_kernel,
        grid_spec=grid_spec,
        out_shape=jax.ShapeDtypeStruct((ROWS, D), F32),
        compiler_params=_cparams(("arbitrary", "arbitrary")),
        name="moe",
    )(counts, h, gates, pos, gates_t, pos_t, w_gu, w_dn)


def _resid_kernel(x_ref, y_ref, mod_ref, nmod_ref, g_ref, *out_refs, final):
    if not final:
        xo_ref, ho_ref = out_refs
        _residual_epilogue(x_ref[...], y_ref[...], mod_ref, nmod_ref, g_ref, xo_ref, ho_ref, 5, 0, 1, False)
        return
    n_ctx_blocks = ROWS_CTX // x_ref.shape[0]
    for ho_ref, mine in zip(out_refs, (pl.program_id(0) < n_ctx_blocks, pl.program_id(0) >= n_ctx_blocks)):
        @pl.when(mine)
        def _(ho_ref=ho_ref):
            _residual_epilogue(x_ref[...], y_ref[...], mod_ref, nmod_ref, g_ref, None, ho_ref, 5, 0, 1, True)


def resid(x, y, mods, layer, gains, final):
    tm = 1024
    rowspec = pl.BlockSpec((tm, D), lambda i: (i, 0))
    if final:
        nb = ROWS_CTX // tm
        out_specs = [pl.BlockSpec((tm, D), lambda i: (jnp.minimum(i, nb - 1), 0)),
                     pl.BlockSpec((tm, D), lambda i: (jnp.maximum(i - nb, 0), 0))]
        out_shape = [jax.ShapeDtypeStruct((ROWS_CTX, D), F32), jax.ShapeDtypeStruct((ROWS_LAT, D), F32)]
        nmod, gain = _modspec(layer, tm), pl.BlockSpec((1, D), lambda i: (0, 0))
    else:
        out_specs = [rowspec, rowspec]
        out_shape = [jax.ShapeDtypeStruct((ROWS, D), F32), jax.ShapeDtypeStruct((ROWS, D), BF16)]
        nmod, gain = _modspec(layer + 1, tm), _gainspec(layer + 1)
    return pl.pallas_call(
        functools.partial(_resid_kernel, final=final),
        grid=(ROWS // tm,),
        in_specs=[rowspec, rowspec, _modspec(layer, tm), nmod, gain],
        out_specs=out_specs,
        out_shape=out_shape,
        compiler_params=_cparams(("arbitrary",)),
        name="resid",
    )(x, y, mods, mods, gains)


def _softmax_pv(scores, values):
    m = None
    for s in scores:
        mi = jnp.max(s, axis=-1, keepdims=True)
        m = mi if m is None else jnp.maximum(m, mi)
    num, den = None, None
    for s, v in zip(scores, values):
        p = jnp.exp(s - m)
        li = jnp.sum(p, axis=-1, keepdims=True)
        oi = jnp.dot(p.astype(BF16), v, preferred_element_type=F32)
        num = oi if num is None else num + oi
        den = li if den is None else den + li
    return num / den


def _qk(q, k):
    return lax.dot_general(q, k, (((1,), (1,)), ((), ())), preferred_element_type=F32)


def _head_rmsnorm(x, gain, n_heads):
    parts = []
    for h in range(n_heads):
        xh = x[:, h * HD:(h + 1) * HD]
        ms = jnp.mean(xh * xh, axis=-1, keepdims=True)
        parts.append(xh * lax.rsqrt(ms + EPS) * gain)
    return parts


def _rope128(x, cos, sin_signed):
    lane = lax.broadcasted_iota(jnp.int32, x.shape, 1)
    up = pltpu.roll(x, LANES - 16, 1)
    dn = pltpu.roll(x, 16, 1)
    partner = jnp.where((lane % 32) < 16, up, dn)
    return x * cos + partner * sin_signed


def _rope_tables():
    t = np.arange(T_LAT)
    n_f = HD // 4
    inv = ROPE_THETA ** (-np.arange(n_f, dtype=np.float32) / n_f)
    cos = np.zeros((T_LAT, HD), np.float32)
    sin = np.zeros((T_LAT, HD), np.float32)
    for half, pos in ((0, t // GRID_W), (1, t % GRID_W)):
        ang = pos[:, None].astype(np.float32) * inv[None, :]
        c, s = np.cos(ang), np.sin(ang)
        base = half * (HD // 2)
        cos[:, base:base + n_f] = c
        cos[:, base + n_f:base + 2 * n_f] = c
        sin[:, base:base + n_f] = -s
        sin[:, base + n_f:base + 2 * n_f] = s
    return np.tile(cos, (1, 2)), np.tile(sin, (1, 2))


def _attn_a_ctx_kernel(q_ref, kv_ref, qg_ref, kg_ref, *rest):
    o_ref, ko_ref, vo_ref = rest[-3:]
    q = q_ref[...]
    kv = kv_ref[...]
    k = kv[:, :A_KV_W]
    v = kv[:, A_KV_W:]
    kn = _head_rmsnorm(k, kg_ref[...], A_KV_HEADS)
    ko_ref[...] = jnp.concatenate(kn, axis=-1)
    vo_ref[...] = v
    qn = _head_rmsnorm(q, qg_ref[...], A_HEADS)
    group = A_HEADS // A_KV_HEADS
    t = q.shape[0]
    outs = []
    for g in range(A_KV_HEADS):
        qs = jnp.concatenate([(qn[g * group + i] * ATT_SCALE).astype(BF16) for i in range(group)], axis=0)
        o = _softmax_pv([_qk(qs, kn[g].astype(BF16))], [v[:, g * HD:(g + 1) * HD].astype(BF16)])
        outs += [o[i * t:(i + 1) * t] for i in range(group)]
    o_ref[...] = jnp.concatenate(outs, axis=-1).astype(BF16)


def attn_a_ctx(z, gq, gk, e, prev_caches):
    t = T_CTX
    n_even = gq.shape[0]
    cache_spec = pl.BlockSpec((None, None, t, A_KV_W), lambda b: (b, e, 0, 0))
    cache_shape = jax.ShapeDtypeStruct((N_CTX, n_even, t, A_KV_W), F32)
    extra = [] if prev_caches is None else list(prev_caches)
    aliases = {} if prev_caches is None else {4: 1, 5: 2}
    return pl.pallas_call(
        _attn_a_ctx_kernel,
        grid=(N_CTX,),
        in_specs=[pl.BlockSpec((t, A_W), lambda b: (b, 0)),
                  pl.BlockSpec((t, 2 * A_KV_W), lambda b: (b, A_W // (2 * A_KV_W))),
                  pl.BlockSpec((None, 1, HD), lambda b: (e, 0, 0)),
                  pl.BlockSpec((None, 1, HD), lambda b: (e, 0, 0))] + [_untouched() for _ in extra],
        out_specs=[pl.BlockSpec((t, A_W), lambda b: (b, 0)), cache_spec, cache_spec],
        out_shape=[jax.ShapeDtypeStruct((ROWS, A_W), BF16), cache_shape, cache_shape],
        input_output_aliases=aliases,
        compiler_params=_cparams(("arbitrary",)),
        name="attn_a_ctx",
    )(z, z, gq, gk, *extra)


A_LAT_TQ = 256


def _attn_a_lat_kernel(q_ref, kv_ref, ck_ref, cv_ref, qg_ref, kg_ref, cq_ref, sq_ref, ckk_ref, skk_ref,
                       prev_ref, o_ref, k_s, v_s):
    del prev_ref

    @pl.when(pl.program_id(1) == 0)
    def _():
        kv = kv_ref[...]
        kn = jnp.concatenate(_head_rmsnorm(kv[:, :A_KV_W], kg_ref[...], A_KV_HEADS), axis=-1)
        k_s[...] = _rope128(kn, ckk_ref[...], skk_ref[...]).astype(BF16)
        v_s[...] = kv[:, A_KV_W:].astype(BF16)

    qn = _head_rmsnorm(q_ref[...], qg_ref[...], A_HEADS)
    cos, sin = cq_ref[...], sq_ref[...]
    qr = []
    for c in range(A_HEADS // 2):
        slab = _rope128(jnp.concatenate(qn[2 * c:2 * c + 2], axis=-1), cos, sin)
        qr.append(slab[:, :HD])
        qr.append(slab[:, HD:])
    ck = ck_ref[...].astype(BF16)
    cv = cv_ref[...].astype(BF16)
    kk = k_s[...]
    vv = v_s[...]
    group = A_HEADS // A_KV_HEADS
    outs = []
    for h in range(A_HEADS):
        g = h // group
        sl = slice(g * HD, (g + 1) * HD)
        qh = (qr[h] * ATT_SCALE).astype(BF16)
        outs.append(_softmax_pv([_qk(qh, kk[:, sl]), _qk(qh, ck[:, sl])], [vv[:, sl], cv[:, sl]]))
    o_ref[...] = jnp.concatenate(outs, axis=-1).astype(BF16)


def attn_a_lat(z, ck, cv, gq, gk, e, y_prev):
    tq = A_LAT_TQ
    nq = T_LAT // tq
    cos, sin = _rope_tables()
    cos, sin = jnp.asarray(cos), jnp.asarray(sin)
    row0 = ROWS_CTX // tq
    seq0 = ROWS_CTX // T_LAT
    cache_spec = pl.BlockSpec((None, None, PAST, A_KV_W), lambda b, i: (b, e, 0, 0))
    return pl.pallas_call(
        _attn_a_lat_kernel,
        grid=(N_LAT, nq),
        in_specs=[pl.BlockSpec((tq, A_W), lambda b, i: (row0 + b * nq + i, 0)),
                  pl.BlockSpec((T_LAT, 2 * A_KV_W), lambda b, i: (seq0 + b, A_W // (2 * A_KV_W))),
                  cache_spec, cache_spec,
                  pl.BlockSpec((None, 1, HD), lambda b, i: (e, 0, 0)),
                  pl.BlockSpec((None, 1, HD), lambda b, i: (e, 0, 0)),
                  pl.BlockSpec((tq, LANES), lambda b, i: (i, 0)),
                  pl.BlockSpec((tq, LANES), lambda b, i: (i, 0)),
                  pl.BlockSpec((T_LAT, LANES), lambda b, i: (0, 0)),
                  pl.BlockSpec((T_LAT, LANES), lambda b, i: (0, 0)),
                  _untouched()],
        out_specs=pl.BlockSpec((tq, A_W), lambda b, i: (row0 + b * nq + i, 0)),
        out_shape=jax.ShapeDtypeStruct((ROWS, A_W), BF16),
        scratch_shapes=[pltpu.VMEM((T_LAT, A_KV_W), BF16),
                        pltpu.VMEM((T_LAT, A_KV_W), BF16)],
        input_output_aliases={10: 0},
        compiler_params=_cparams(("arbitrary", "arbitrary")),
        name="attn_a_lat",
    )(z, z, ck, cv, gq, gk, cos, sin, cos, sin, y_prev)


def _attn_c_ctx_kernel(q_ref, k_ref, v_ref, *rest):
    o_ref, ko_ref, vo_ref = rest[-3:]
    q = q_ref[...]
    k = k_ref[...]
    v = v_ref[...]
    ko_ref[...] = k
    vo_ref[...] = v
    outs = []
    for h in range(q.shape[1] // HD):
        sl = slice(h * HD, (h + 1) * HD)
        qh = (q[:, sl] * ATT_SCALE).astype(BF16)
        outs.append(_softmax_pv([_qk(qh, k[:, sl].astype(BF16))], [v[:, sl].astype(BF16)]))
    o_ref[...] = jnp.concatenate(outs, axis=-1).astype(BF16)


C_CTX_HEADS_PER_STEP = 8


def attn_c_ctx(z, o, n_odd, prev_caches):
    t = T_CTX
    wblk = C_CTX_HEADS_PER_STEP * HD
    nhp = C_W // wblk
    cache_spec = pl.BlockSpec((None, None, t, wblk), lambda b, p: (b, o, 0, p))
    cache_shape = jax.ShapeDtypeStruct((N_CTX, n_odd, t, C_W), F32)
    extra = [] if prev_caches is None else list(prev_caches)
    aliases = {} if prev_caches is None else {3: 1, 4: 2}
    return pl.pallas_call(
        _attn_c_ctx_kernel,
        grid=(N_CTX, nhp),
        in_specs=[pl.BlockSpec((t, wblk), lambda b, p: (b, p)),
                  pl.BlockSpec((t, wblk), lambda b, p: (b, nhp + p)),
                  pl.BlockSpec((t, wblk), lambda b, p: (b, 2 * nhp + p))] + [_untouched() for _ in extra],
        out_specs=[pl.BlockSpec((t, wblk), lambda b, p: (b, p)), cache_spec, cache_spec],
        out_shape=[jax.ShapeDtypeStruct((ROWS, C_W), BF16), cache_shape, cache_shape],
        input_output_aliases=aliases,
        compiler_params=_cparams(("arbitrary", "arbitrary")),
        name="attn_c_ctx",
    )(z, z, z, *extra)


NA_GRID_ROWS = T_LAT // GRID_W
NA_DR = 2 * NA_ROWS - 1
NA_DC = 2 * NA_COLS - 1


def _na_row_start(r):
    return min(max(r - NA_ROWS // 2, 0), NA_GRID_ROWS - NA_ROWS)


def _na_groups():
    groups, r = [], 0
    while r < NA_GRID_ROWS:
        r1 = r
        while r1 + 1 < NA_GRID_ROWS and (_na_row_start(r1 + 1) == _na_row_start(r) or r1 + 1 - r < 4):
            r1 += 1
        lo = _na_row_start(r)
        hi = _na_row_start(r1) + NA_ROWS
        pairs = -(-(hi - lo) // 2)
        lo = min(lo, NA_GRID_ROWS - 2 * pairs)
        groups.append((r, r1, lo, pairs))
        r = r1 + 1
    return groups


def _na_pair_codes():
    codes = []
    plan = []
    for r0, r1, lo, pairs in _na_groups():
        rows = []
        for r in range(r0, r1 + 1):
            rs = _na_row_start(r)
            row = []
            for p in range(pairs):
                code = tuple((kr - r + NA_ROWS - 1) if rs <= kr < rs + NA_ROWS else None
                             for kr in (lo + 2 * p, lo + 2 * p + 1))
                if code not in codes:
                    codes.append(code)
                row.append(codes.index(code))
            rows.append(row)
        plan.append(rows)
    return codes, plan


def _na_bias_table(rpb):
    n_l = rpb.shape[0]
    col = np.arange(GRID_W)
    cs = np.clip(col - NA_COLS // 2, 0, GRID_W - NA_COLS)
    col_in = (col[None, :] >= cs[:, None]) & (col[None, :] < cs[:, None] + NA_COLS)
    period = GRID_W + 1
    seq = jnp.concatenate([rpb, jnp.zeros((n_l, C_HEADS, NA_DR, period - NA_DC), F32)], axis=-1)
    seq = jnp.roll(seq, -(NA_COLS - 1), axis=-1)
    tile = jnp.tile(seq, (1, 1, 1, GRID_W))[..., :GRID_W * GRID_W].reshape(n_l, C_HEADS, NA_DR, GRID_W, GRID_W)
    tile = jnp.where(jnp.asarray(col_in), tile, NEG_INF)
    masked = jnp.full((n_l, C_HEADS, GRID_W, GRID_W), NEG_INF, F32)
    pick = lambda dr: masked if dr is None else tile[:, :, dr]
    codes, _ = _na_pair_codes()
    return jnp.stack([jnp.concatenate([pick(a), pick(b)], axis=-1) for a, b in codes], axis=2)


def _attn_na_kernel(q_ref, k_ref, v_ref, ck_ref, cv_ref, bias_ref, prev_ref, o_ref):
    del prev_ref
    _, plan = _na_pair_codes()
    heads = []
    for h in range(LANES // HD):
        sl = slice(h * HD, (h + 1) * HD)
        q = (q_ref[:, sl] * ATT_SCALE).astype(BF16)
        k = k_ref[:, sl].astype(BF16)
        v = v_ref[:, sl].astype(BF16)
        ck = ck_ref[:, sl].astype(BF16)
        cv = cv_ref[:, sl].astype(BF16)
        rows = []
        for (r0, r1, lo, pairs), codes in zip(_na_groups(), plan):
            qg = q[r0 * GRID_W:(r1 + 1) * GRID_W]
            kw = k[lo * GRID_W:(lo + 2 * pairs) * GRID_W]
            vw = v[lo * GRID_W:(lo + 2 * pairs) * GRID_W]
            bias = jnp.concatenate(
                [jnp.concatenate([bias_ref[h, c] for c in row], axis=-1) for row in codes], axis=0)
            s_nb = _qk(qg, kw) + bias
            s_cx = _qk(qg, ck)
            rows.append(_softmax_pv([s_nb, s_cx], [vw, cv]))
        heads.append(jnp.concatenate(rows, axis=0))
    o_ref[...] = jnp.concatenate(heads, axis=-1).astype(BF16)


def attn_na(z, ck, cv, bias, o, y_prev):
    nhp = C_W // LANES
    hpb = LANES // HD
    seq0 = ROWS_CTX // T_LAT
    cache_spec = pl.BlockSpec((None, None, PAST, LANES), lambda p, b: (b, o, 0, p))
    return pl.pallas_call(
        _attn_na_kernel,
        grid=(nhp, N_LAT),
        in_specs=[pl.BlockSpec((T_LAT, LANES), lambda p, b: (seq0 + b, p)),
                  pl.BlockSpec((T_LAT, LANES), lambda p, b: (seq0 + b, nhp + p)),
                  pl.BlockSpec((T_LAT, LANES), lambda p, b: (seq0 + b, 2 * nhp + p)),
                  cache_spec, cache_spec,
                  pl.BlockSpec((None, hpb, bias.shape[2], GRID_W, LANES), lambda p, b: (o, p, 0, 0, 0)),
                  _untouched()],
        out_specs=pl.BlockSpec((T_LAT, LANES), lambda p, b: (seq0 + b, p)),
        out_shape=jax.ShapeDtypeStruct((ROWS, C_W), BF16),
        input_output_aliases={6: 0},
        compiler_params=_cparams(("arbitrary", "arbitrary")),
        name="attn_na",
    )(z, z, z, ck, cv, bias, y_prev)


def _seg_sum(x, n_heads):
    parts = []
    for h in range(n_heads):
        s = jnp.sum(x[:, h * HD:(h + 1) * HD], axis=-1, keepdims=True)
        parts.append(jnp.broadcast_to(s, (x.shape[0], HD)))
    return jnp.concatenate(parts, axis=-1)


PREP_TM = 256


def _seq_len_at(row_start):
    return jnp.where(row_start < ROWS_CTX, T_CTX, T_LAT)


def _rwkv_prep_kernel(z_ref, zprev_ref, znext_ref, mu_ref, kkw_ref, w0_ref, w2_ref, a0_ref, a2_ref, ka_ref,
                      rk_ref, g2_ref, o_rk_ref, o_wk_ref, o_bv_ref, g_ref, bonus_ref):
    z = z_ref[:, A_IN:]
    t = z.shape[0]
    start = pl.program_id(0) * t
    seq_len = _seq_len_at(start)
    pos = (start - jnp.where(start < ROWS_CTX, 0, ROWS_CTX)) % seq_len
    halo_prev = jnp.where(pos == 0, 0.0, zprev_ref[SUBLANES - 1:SUBLANES, A_IN:])
    halo_next = jnp.where(pos + t == seq_len, 0.0, znext_ref[0:1, A_IN:])
    row = lax.broadcasted_iota(jnp.int32, (t, 1), 0)
    prev = jnp.where(row == 0, halo_prev, pltpu.roll(z, 1, 0))
    nxt = jnp.where(row == t - 1, halo_next, pltpu.roll(z, t - 1, 0))
    m = z + mu_ref[...] * (0.5 * (prev + nxt) - z)
    r = m[:, :B_W]
    k = m[:, B_W:2 * B_W]
    v = m[:, 2 * B_W:3 * B_W]
    o = 3 * B_W
    wd = m[:, o:o + 2 * LORA_W]
    ad = m[:, o + 2 * LORA_W:o + 2 * LORA_W + 2 * LORA_A]
    gd = m[:, o + 2 * LORA_W + 2 * LORA_A:]

    kkr = k * kkw_ref[...]
    kk = kkr * lax.rsqrt(_seg_sum(kkr * kkr, B_HEADS) + 1e-12)

    wl = w0_ref[...] + jnp.dot(jnp.tanh(wd), w2_ref[...], precision=HIGHEST, preferred_element_type=F32)
    decay = jnp.exp(-float(np.exp(-0.5)) * jax.nn.sigmoid(wl))
    a = jax.nn.sigmoid(a0_ref[...] + jnp.dot(ad, a2_ref[...], precision=HIGHEST, preferred_element_type=F32))
    k2 = jnp.concatenate([k, k], axis=-1)
    ka2 = jnp.concatenate([ka_ref[...], ka_ref[...]], axis=-1)
    kk2 = jnp.concatenate([kk, kk], axis=-1)

    kd = k2 * (1.0 + (a - 1.0) * ka2)
    bb = kk2 * a
    for h in range(B_HEADS):
        sl = slice(h * HD, (h + 1) * HD)
        o_rk_ref[pl.ds(h, t, stride=B_HEADS), :] = jnp.concatenate([r[:, sl], kk[:, sl]], axis=-1)
        for dr in range(2):
            sd = slice(dr * B_W + h * HD, dr * B_W + (h + 1) * HD)
            row = pl.ds(dr * B_HEADS + h, t, stride=2 * B_HEADS)
            o_wk_ref[row, :] = jnp.concatenate([decay[:, sd], kd[:, sd]], axis=-1)
            o_bv_ref[row, :] = jnp.concatenate([bb[:, sd], v[:, sl]], axis=-1)
    g_ref[...] = jnp.dot(jax.nn.sigmoid(gd).astype(BF16), g2_ref[...].astype(BF16), preferred_element_type=F32)
    bonus_ref[...] = _seg_sum(r * k * rk_ref[...], B_HEADS) * v


def rwkv_prep(z, params, e):
    tm = PREP_TM
    full = lambda a: pl.BlockSpec((None,) + a.shape[1:], lambda s: (e, 0, 0))
    out_rows = (B_HEADS, 2 * B_HEADS, 2 * B_HEADS)
    per = tm // SUBLANES
    last = ROWS // SUBLANES - 1
    rowspec = pl.BlockSpec((tm, B_W), lambda s: (s, 0))
    return pl.pallas_call(
        _rwkv_prep_kernel,
        grid=(ROWS // tm,),
        in_specs=[pl.BlockSpec((tm, EVEN_IN), lambda s: (s, 0)),
                  pl.BlockSpec((SUBLANES, EVEN_IN), lambda s: (jnp.maximum(s * per - 1, 0), 0)),
                  pl.BlockSpec((SUBLANES, EVEN_IN), lambda s: (jnp.minimum((s + 1) * per, last), 0))]
                 + [full(p) for p in params],
        out_specs=[pl.BlockSpec((tm * n, LANES), lambda s: (s, 0)) for n in out_rows] + [rowspec, rowspec],
        out_shape=[jax.ShapeDtypeStruct((ROWS * n, LANES), F32) for n in out_rows]
                  + [jax.ShapeDtypeStruct((ROWS, B_W), F32)] * 2,
        compiler_params=_cparams(("arbitrary",)),
        name="rwkv_prep",
    )(z, z, z, *params)


def rwkv_prep_params(prm):
    n_even = prm['b_mu'].shape[0]

    def block_diag(w):
        zero = jnp.zeros((n_even, LORA_W, B_W), F32)
        return jnp.concatenate([jnp.concatenate([w[:, 0], zero], 2), jnp.concatenate([zero, w[:, 1]], 2)], 1)

    row = lambda a: a.reshape(n_even, 1, -1)
    return (row(prm['b_mu']), row(prm['b_kk']), row(prm['b_w0']), block_diag(prm['b_w2']),
            row(prm['b_a0']), block_diag(prm['b_a2']), row(prm['b_ka']), row(prm['b_rk']), prm['b_g2'])


SCAN_CHAINS = 8


def _tree_sum(xs):
    while len(xs) > 1:
        xs = [xs[i] + xs[i + 1] for i in range(0, len(xs) - 1, 2)] + ([xs[-1]] if len(xs) % 2 else [])
    return xs[0]


def _scan_step(rk_b, wk_b, bv_b, vv, s_ref, ni):
    n_acc = max(1, SCAN_CHAINS // ni)

    def bcast(ref, row):
        return jnp.broadcast_to(ref[pl.ds(row, 1), :], (SUBLANES, LANES))

    sa = [[None] * n_acc for _ in range(ni)]
    for j in range(HD):
        kkj = bcast(rk_b, HD + j)
        for g in range(ni):
            p = s_ref[j, pl.ds(g * SUBLANES, SUBLANES), :] * kkj
            a = j % n_acc
            sa[g][a] = p if sa[g][a] is None else sa[g][a] + p
    sa = [-_tree_sum(x) for x in sa]
    ya = [[None] * n_acc for _ in range(ni)]
    for j in range(HD):
        wj = bcast(wk_b, j)
        kj = bcast(wk_b, HD + j)
        bj = bcast(bv_b, j)
        rj = bcast(rk_b, j)
        for g in range(ni):
            sl = pl.ds(g * SUBLANES, SUBLANES)
            s_new = s_ref[j, sl, :] * wj + sa[g] * bj + vv[g] * kj
            s_ref[j, sl, :] = s_new
            p = s_new * rj
            a = j % n_acc
            ya[g][a] = p if ya[g][a] is None else ya[g][a] + p
    return [_tree_sum(ya[g]) for g in range(ni)]


def _scan_pairs(tc, relayout, run):
    relayout(0, 0)

    def pair(k, carry):
        i0 = 2 * k
        relayout(i0 + 1, 1)
        run(i0, 0)
        relayout(jnp.minimum(i0 + 2, tc - 1), 0)
        run(i0 + 1, 1)
        return carry

    lax.fori_loop(0, tc // 2, pair, 0)


def _scan_ctx_kernel(rk_ref, wk_ref, bv_ref, y_ref, st_ref, rk0, wk0, bv0, rk1, wk1, bv1, s_ref, *, tc):
    d = pl.program_id(0)
    c = pl.program_id(1)
    ni = HD // SUBLANES
    bufs = ((rk0, wk0, bv0), (rk1, wk1, bv1))

    @pl.when(c == 0)
    def _():
        s_ref[...] = jnp.zeros_like(s_ref)

    def t_of(i):
        return i + d * (tc - 1 - 2 * i)

    def relayout(i, slot):
        t = t_of(i)
        for src, dst in zip((rk_ref, wk_ref, bv_ref), bufs[slot]):
            dst[...] = src[:, :, t].reshape(LANES, LANES).T

    def run(i, slot):
        rk_b, wk_b, bv_b = bufs[slot]
        vv = [bv_b[pl.ds(HD + g * SUBLANES, SUBLANES), :] for g in range(ni)]
        y = jnp.concatenate(_scan_step(rk_b, wk_b, bv_b, vv, s_ref, ni), axis=0)
        y_ref[:, :, t_of(i)] = y.T.reshape(CTX_GROUPS, ROW_SUB, B_HEADS, HD)

    _scan_pairs(tc, relayout, run)

    @pl.when(c == pl.num_programs(1) - 1)
    def _():
        st_ref[...] = s_ref[...]


def _rows5(x, heads):
    return x.reshape(ROW_GROUPS, ROW_SUB, T_CTX, heads, x.shape[-1])


def _operand_bufs():
    return [pltpu.VMEM((LANES, LANES), F32)] * 6


def rwkv_scan_ctx(prep, tc):
    rk, wk, bv = prep[:3]
    nc = T_CTX // tc

    def chunk(d, c):
        return c + d * (nc - 1 - 2 * c)

    blk = (CTX_GROUPS, ROW_SUB, tc, B_HEADS, LANES)
    shared = pl.BlockSpec(blk, lambda d, c: (0, 0, chunk(d, c), 0, 0))
    perdir = pl.BlockSpec(blk, lambda d, c: (0, 0, chunk(d, c), d, 0))
    yspec = pl.BlockSpec((CTX_GROUPS, ROW_SUB, tc, B_HEADS, HD), lambda d, c: (0, 0, chunk(d, c), d, 0))
    return pl.pallas_call(
        functools.partial(_scan_ctx_kernel, tc=tc),
        grid=(2, nc),
        in_specs=[shared, perdir, perdir],
        out_specs=[yspec, pl.BlockSpec((None, HD, HD, LANES), lambda d, c: (d, 0, 0, 0))],
        out_shape=[jax.ShapeDtypeStruct((ROW_GROUPS, ROW_SUB, T_CTX, 2 * B_HEADS, HD), F32),
                   jax.ShapeDtypeStruct((2, HD, HD, LANES), F32)],
        scratch_shapes=_operand_bufs() + [pltpu.VMEM((HD, HD, LANES), F32)],
        compiler_params=_cparams(("arbitrary", "arbitrary")),
        name="rwkv_scan_ctx",
    )(_rows5(rk, B_HEADS), _rows5(wk, 2 * B_HEADS), _rows5(bv, 2 * B_HEADS))


LAT_REP = LANES // (2 * N_LAT * B_HEADS)


def _scan_lat_kernel(rkf_ref, rkb_ref, wkf_ref, wkb_ref, bvf_ref, bvb_ref, s0_ref, yf_ref, yb_ref,
                     rk0, wk0, bv0, rk1, wk1, bv1, s_ref, y_s, *, tc):
    c = pl.program_id(0)
    n = N_LAT * B_HEADS
    nd = 2 * n
    ni = HD // (LAT_REP * SUBLANES)
    bufs = ((rk0, wk0, bv0), (rk1, wk1, bv1))

    @pl.when(c == 0)
    def _():
        s_ref[...] = s0_ref[...]

    def relayout(i, slot):
        srcs = ((rkf_ref, rkb_ref), (wkf_ref, wkb_ref), (bvf_ref, bvb_ref))
        for (fwd, bwd), dst in zip(srcs, bufs[slot]):
            m = jnp.concatenate([fwd[:, 0, i].reshape(n, LANES), bwd[:, 0, tc - 1 - i].reshape(n, LANES)], axis=0)
            dst[...] = jnp.concatenate([m] * LAT_REP, axis=0).T

    def run(i, slot):
        rk_b, wk_b, bv_b = bufs[slot]
        group = lax.broadcasted_iota(jnp.int32, (SUBLANES, LANES), 1) // nd
        vv = []
        for ig in range(ni):
            v = bv_b[pl.ds(HD + ig * SUBLANES, SUBLANES), :]
            for g in range(1, LAT_REP):
                v = jnp.where(group == g, bv_b[pl.ds(HD + (g * ni + ig) * SUBLANES, SUBLANES), :], v)
            vv.append(v)
        y_s[i] = jnp.concatenate(_scan_step(rk_b, wk_b, bv_b, vv, s_ref, ni), axis=0)

    _scan_pairs(tc, relayout, run)

    def store(i, carry):
        y = y_s[i]
        rows = [y] + [pltpu.roll(y, LANES - g * nd, 1) for g in range(1, LAT_REP)]
        yt = jnp.concatenate(rows, axis=0).T
        yf_ref[:, 0, i] = yt[:n].reshape(N_LAT, B_HEADS, HD)
        yb_ref[:, 0, tc - 1 - i] = yt[n:nd].reshape(N_LAT, B_HEADS, HD)
        return carry

    lax.fori_loop(0, tc, store, 0, unroll=8)


def rwkv_scan_lat(prep, s0f, s0b, tc):
    rk, wk, bv = prep[:3]
    nc = T_LAT // tc
    per_sub = T_CTX // tc
    g0 = CTX_GROUPS // N_LAT

    def fwd_map(head_block):
        return lambda c: (g0, c // per_sub, c % per_sub, head_block, 0)

    def bwd_map(head_block):
        return lambda c: (g0, (nc - 1 - c) // per_sub, (nc - 1 - c) % per_sub, head_block, 0)

    blk = (N_LAT, 1, tc, B_HEADS, LANES)
    yblk = (N_LAT, 1, tc, B_HEADS, HD)
    ni8 = HD // LAT_REP
    s0 = jnp.stack([s0f, s0b]).reshape(2, N_LAT, B_HEADS, LAT_REP, ni8, HD)
    s0 = s0.transpose(5, 4, 3, 0, 1, 2).reshape(HD, ni8, LANES)
    yshape = jax.ShapeDtypeStruct((N_LAT, ROW_SUB, T_CTX, B_HEADS, HD), F32)
    return pl.pallas_call(
        functools.partial(_scan_lat_kernel, tc=tc),
        grid=(nc,),
        in_specs=[pl.BlockSpec(blk, fwd_map(0)), pl.BlockSpec(blk, bwd_map(0)),
                  pl.BlockSpec(blk, fwd_map(0)), pl.BlockSpec(blk, bwd_map(1)),
                  pl.BlockSpec(blk, fwd_map(0)), pl.BlockSpec(blk, bwd_map(1)),
                  pl.BlockSpec((HD, ni8, LANES), lambda c: (0, 0, 0))],
        out_specs=[pl.BlockSpec(yblk, lambda c: (0, c // per_sub, c % per_sub, 0, 0)),
                   pl.BlockSpec(yblk, lambda c: (0, (nc - 1 - c) // per_sub, (nc - 1 - c) % per_sub, 0, 0))],
        out_shape=[yshape, yshape],
        scratch_shapes=_operand_bufs() + [pltpu.VMEM((HD, ni8, LANES), F32),
                                          pltpu.VMEM((tc, ni8, LANES), F32)],
        compiler_params=_cparams(("arbitrary",)),
        name="rwkv_scan_lat",
    )(_rows5(rk, B_HEADS), _rows5(rk, B_HEADS), _rows5(wk, 2 * B_HEADS), _rows5(wk, 2 * B_HEADS),
      _rows5(bv, 2 * B_HEADS), _rows5(bv, 2 * B_HEADS), s0)


def _rwkv_post_kernel(yc_ref, ylf_ref, ylb_ref, bonus_ref, g_ref, lng_ref, lnb_ref, o_ref, *, n_ctx_blocks):
    tm = o_ref.shape[0]

    def emit(load):
        outs = []
        for h in range(B_HEADS):
            sl = slice(h * HD, (h + 1) * HD)
            y = load(h)
            yc = y - jnp.mean(y, axis=-1, keepdims=True)
            var = jnp.mean(yc * yc, axis=-1, keepdims=True)
            yn = yc * lax.rsqrt(var + GN_EPS) * lng_ref[:, sl] + lnb_ref[:, sl]
            outs.append((yn + bonus_ref[:, sl]) * g_ref[:, sl])
        o_ref[...] = jnp.concatenate(outs, axis=-1).astype(BF16)

    @pl.when(pl.program_id(0) < n_ctx_blocks)
    def _():
        emit(lambda h: yc_ref[pl.ds(h, tm, stride=2 * B_HEADS), :]
             + yc_ref[pl.ds(B_HEADS + h, tm, stride=2 * B_HEADS), :])

    @pl.when(pl.program_id(0) >= n_ctx_blocks)
    def _():
        emit(lambda h: ylf_ref[pl.ds(h, tm, stride=B_HEADS), :] + ylb_ref[pl.ds(h, tm, stride=B_HEADS), :])


def rwkv_post(y_ctx, y_lat_f, y_lat_b, bonus, g, lng, lnb, e):
    tm = 256
    nb = ROWS_CTX // tm
    rowspec = pl.BlockSpec((tm, B_W), lambda i: (i, 0))
    vecspec = pl.BlockSpec((None, 1, B_W), lambda i: (e, 0, 0))
    latspec = pl.BlockSpec((tm * B_HEADS, HD), lambda i: (jnp.maximum(i - nb, 0), 0))
    return pl.pallas_call(
        functools.partial(_rwkv_post_kernel, n_ctx_blocks=nb),
        grid=(ROWS // tm,),
        in_specs=[pl.BlockSpec((tm * 2 * B_HEADS, HD), lambda i: (jnp.minimum(i, nb - 1), 0)), latspec, latspec,
                  rowspec, rowspec, vecspec, vecspec],
        out_specs=rowspec,
        out_shape=jax.ShapeDtypeStruct((ROWS, B_W), BF16),
        compiler_params=_cparams(("arbitrary",)),
        name="rwkv_post",
    )(y_ctx, y_lat_f, y_lat_b, bonus, g, lng, lnb)


def kernel(x_prompt, x_sample, cache_a_k, cache_a_v, state_b_fwd, state_b_bwd, cache_c_k, cache_c_v, c, c_ctx,
           ada_w, ada_b, norm1_g, norm2_g, final_norm_g, w_in_e, w_out_e, a_q_gain, a_k_gain, b_mu, b_w0, b_w2,
           b_a0, b_a2, b_g2, b_kk, b_ka, b_rk, b_ln_g, b_ln_b, ffn_w_gu, ffn_w_dn, w_in_o, w_out_o, c_rpb,
           router_w, router_b, moe_w_gu, moe_w_dn):
    n_even, n_odd = w_in_e.shape[0], w_in_o.shape[0]
    prep_params = rwkv_prep_params(dict(b_mu=b_mu, b_w0=b_w0, b_w2=b_w2, b_a0=b_a0, b_a2=b_a2, b_g2=b_g2,
                                        b_kk=b_kk, b_ka=b_ka, b_rk=b_rk))
    cond = jnp.zeros((N_MODS, D), F32).at[0].set(c_ctx).at[1:1 + N_LAT].set(c)
    mods = ada_all(cond, ada_w, ada_b).reshape(DEPTH, N_MODS, 6, D)
    g1 = norm1_g.reshape(DEPTH, 1, D)
    g2 = norm2_g.reshape(DEPTH, 1, D)
    gq = a_q_gain.reshape(n_even, 1, HD)
    gk = a_k_gain.reshape(n_even, 1, HD)
    lng = b_ln_g.reshape(n_even, 1, B_W)
    lnb = b_ln_b.reshape(n_even, 1, B_W)
    ck_a = cache_a_k.reshape(N_LAT, n_even, PAST, A_KV_W)
    cv_a = cache_a_v.reshape(N_LAT, n_even, PAST, A_KV_W)
    ck_c = cache_c_k.reshape(N_LAT, n_odd, PAST, C_W)
    cv_c = cache_c_v.reshape(N_LAT, n_odd, PAST, C_W)
    na_bias = _na_bias_table(c_rpb)
    rw_p = jnp.zeros((n_odd, D, LANES), F32).at[:, :, :N_EXPERTS].set(router_w)
    rb_p = jnp.zeros((n_odd, 1, LANES), F32).at[:, 0, :N_EXPERTS].set(router_b)
    moe_gu_b = moe_w_gu.astype(BF16)
    moe_dn_b = moe_w_dn.astype(BF16)

    x, h = first_norm(x_prompt, x_sample, g1, mods)

    a_caches, c_caches = None, None
    new_sf, new_sb = [], []
    y_final = None
    for l in range(DEPTH):
        if l % 2 == 0:
            e = l // 2
            z = mm_in(h, w_in_e, e, tn=EVEN_IN // 3)
            y_a, k_new, v_new = attn_a_ctx(z, gq, gk, e, a_caches)
            a_caches = (k_new, v_new)
            y_a = attn_a_lat(z, ck_a, cv_a, gq, gk, e, y_a)
            prep = rwkv_prep(z, prep_params, e)
            y_s, st = rwkv_scan_ctx(prep, tc=32)
            y_lf, y_lb = rwkv_scan_lat(prep, state_b_fwd[:, e], state_b_bwd[:, e], tc=64)
            st = st.reshape(2, HD, HD, N_CTX, B_HEADS).transpose(0, 3, 4, 2, 1)
            new_sf.append(st[0])
            new_sb.append(st[1])
            y_b = rwkv_post(y_s.reshape(ROWS * 2 * B_HEADS, HD), y_lf.reshape(ROWS_LAT * B_HEADS, HD),
                            y_lb.reshape(ROWS_LAT * B_HEADS, HD), prep[4], prep[3], lng, lnb, e)
            x, h = mm_out([y_a, y_b], w_out_e, e, x, mods, l, g2, l)
            x, h = ffn(h, ffn_w_gu, ffn_w_dn, e, x, mods, l, g1)
        else:
            o = l // 2
            z = mm_in(h, w_in_o, o, tn=C_W)
            y, k_new, v_new = attn_c_ctx(z, o, n_odd, c_caches)
            c_caches = (k_new, v_new)
            y = attn_na(z, ck_c, cv_c, na_bias, o, y)
            x, h = mm_out([y], w_out_o, o, x, mods, l, g2, l)
            routing = router(x, g2, mods, l, rw_p, rb_p, o)
            y_moe = moe_routed(h, routing, moe_gu_b, moe_dn_b, o)
            if l + 1 < DEPTH:
                x, h = resid(x, y_moe, mods, l, g1, False)
            else:
                y_final = resid(x, y_moe, mods, l, final_norm_g[None], True)

    y_prompt = y_final[0].reshape(N_CTX, T_CTX, D)
    y_sample = y_final[1].reshape(N_LAT, T_LAT, D)
    return (y_prompt, y_sample,
            a_caches[0].reshape(N_CTX, n_even, T_CTX, A_KV_HEADS, HD),
            a_caches[1].reshape(N_CTX, n_even, T_CTX, A_KV_HEADS, HD),
            jnp.stack(new_sf, axis=1), jnp.stack(new_sb, axis=1),
            c_caches[0].reshape(N_CTX, n_odd, T_CTX, C_HEADS, HD),
            c_caches[1].reshape(N_CTX, n_odd, T_CTX, C_HEADS, HD))
```
